```python
import math
import jax
import jax.numpy as jnp
from jax import lax
import numpy as np


D_MODEL = 1024
BATCH = 8
SEQ = 4096
DEPTH = 2

HEAD_DIM = D_MODEL // 16
DIL_GROUPS = ((128, 1), (512, 4), (2048, 16))
DIL_HEADS_PER_GROUP = 4
DIL_HEADS = len(DIL_GROUPS) * DIL_HEADS_PER_GROUP
DIL_WIDTH = DIL_HEADS * HEAD_DIM
DIL_OUT = DIL_HEADS_PER_GROUP * HEAD_DIM
DIL_BLOCK = 128
SB_HEADS = 8
SB_WIDTH = SB_HEADS * HEAD_DIM
SB_BLOCK = 128
CONV_CH = D_MODEL // 2
CONV_WIDTH = 31
N_BRANCH = 3
IN_SPLITS = (3 * DIL_WIDTH, 3 * DIL_WIDTH + 3 * SB_WIDTH, 3 * DIL_WIDTH + 3 * SB_WIDTH + 2 * CONV_CH)
IN_COLS = IN_SPLITS[-1] + N_BRANCH * D_MODEL
D_FF = ((8 * D_MODEL // 3 + 127) // 128) * 128
N_EXPERTS = 8
TOP_K = 2
D_FF_EXPERT = D_FF
MOE_BLOCK = 256
N_DENSE = (DEPTH + 1) // 2
N_MOE = DEPTH // 2
EPS = 1e-6
ALIBI_MAX_BIAS = 8.0

kernel_name = 'hybrid_dilated_stickbreak_conformer_moe'


def rms_norm(x, g):
    xf = x.astype(jnp.float32)
    y = xf * lax.rsqrt(jnp.mean(xf * xf, axis=-1, keepdims=True) + EPS)
    return y.astype(x.dtype) * g


def layer_norm(x, g, b):
    xf = x.astype(jnp.float32)
    mu = jnp.mean(xf, axis=-1, keepdims=True)
    var = jnp.mean(jnp.square(xf - mu), axis=-1, keepdims=True)
    return ((xf - mu) * lax.rsqrt(var + EPS)).astype(x.dtype) * g + b


def alibi_slopes(n_heads):
    return 2.0 ** (-ALIBI_MAX_BIAS * jnp.arange(1, n_heads + 1, dtype=jnp.float32) / n_heads)


def dilated_window_attention(q, k, v, slopes, window, dilation):
    bsz, seq, nh, dh = q.shape
    reach = window // dilation
    span = DIL_BLOCK * dilation
    seq_pad = -(-seq // span) * span
    sub_len = seq_pad // dilation
    nb = sub_len // DIL_BLOCK

    def to_blocks(t):
        t = jnp.pad(t, ((0, 0), (0, seq_pad - seq), (0, 0), (0, 0)))
        t = t.reshape(bsz, sub_len, dilation, nh, dh).transpose(0, 2, 3, 1, 4)
        return t.reshape(bsz, dilation, nh, nb, DIL_BLOCK, dh)

    def with_prev(t):
        prev = jnp.concatenate([jnp.zeros_like(t[:, :, :, :1]), t[:, :, :, :-1]], axis=3)
        return jnp.concatenate([prev, t], axis=4)

    qb = to_blocks(q)
    kk = with_prev(to_blocks(k))
    vv = with_prev(to_blocks(v))
    s = jnp.einsum('brhnqc,brhnkc->brhnqk', qb, kk).astype(jnp.float32) / math.sqrt(dh)
    qi = jnp.arange(DIL_BLOCK)[:, None] + DIL_BLOCK
    ki = jnp.arange(2 * DIL_BLOCK)[None, :]
    dist = qi - ki
    first = (jnp.arange(nb) == 0)[:, None, None] & (ki < DIL_BLOCK)[None]
    valid = (dist >= 0) & (dist <= reach) & ~first
    bias = -slopes[:, None, None, None] * (dist * dilation).astype(jnp.float32)
    s = jnp.where(valid, s + bias, -jnp.inf)
    m = jnp.max(s, axis=-1, keepdims=True)
    p = jnp.exp(s - m)
    denom = jnp.sum(p, axis=-1, keepdims=True)
    o = jnp.einsum('brhnqk,brhnkc->brhnqc', (p / denom).astype(v.dtype), vv)
    lse = (m + jnp.log(denom))[..., 0]

    def unblock(t):
        t = t.reshape(bsz, dilation, nh, sub_len, *t.shape[5:])
        t = jnp.moveaxis(t, 3, 1)
        return t.reshape(bsz, seq_pad, nh, *t.shape[4:])[:, :seq]

    return unblock(o), unblock(lse)


def stick_breaking_attention(q, k, v):
    bsz, seq, nh, dh = q.shape
    n_blocks = seq // SB_BLOCK
    qb = q.reshape(bsz, n_blocks, SB_BLOCK, nh, dh).transpose(1, 0, 3, 2, 4)
    kt = k.transpose(0, 2, 1, 3)
    vt = v.transpose(0, 2, 1, 3)
    kpos = jnp.arange(seq)
    scale = 1.0 / math.sqrt(dh)

    def one_block(args):
        q_blk, blk = args
        z = jnp.einsum('bhqc,bhkc->bhqk', q_blk, kt).astype(jnp.float32) * scale
        qpos = blk * SB_BLOCK + jnp.arange(SB_BLOCK)
        before = kpos[None, :] < qpos[:, None]
        log_keep = jnp.where(before, jax.nn.log_sigmoid(-z), 0.0)
        log_later = lax.cumsum(log_keep, axis=3, reverse=True) - log_keep
        a = jnp.where(before, jnp.exp(jax.nn.log_sigmoid(z) + log_later), 0.0)
        return jnp.einsum('bhqk,bhkc->bhqc', a.astype(vt.dtype), vt)

    o = lax.map(one_block, (qb, jnp.arange(n_blocks)))
    return o.transpose(1, 0, 3, 2, 4).reshape(bsz, seq, nh * dh)


def conformer_conv(u_val, u_gate, conv_w, conv_b, norm_g, norm_b):
    u = u_val * jax.nn.sigmoid(u_gate)
    u = lax.conv_general_dilated(u, conv_w[:, None, :], window_strides=(1,),
                                 padding=[(CONV_WIDTH - 1, 0)],
                                 dimension_numbers=('NWC', 'WIO', 'NWC'),
                                 feature_group_count=CONV_CH) + conv_b
    return jax.nn.silu(layer_norm(u, norm_g, norm_b))


def hybrid_mixer(h, w_in, q_norm_g, k_norm_g, conv_w, conv_b, conv_norm_g, conv_norm_b,
                 w_branch_a, w_branch_b, w_branch_c, w_out):
    bsz, seq, _ = h.shape
    proj = jnp.einsum('bsd,de->bse', h, w_in)
    qkv_a, qkv_b, glu, gate_logits = jnp.split(proj, list(IN_SPLITS), axis=-1)

    qkv_a = qkv_a.reshape(bsz, seq, 3, DIL_HEADS, HEAD_DIM)
    qa = rms_norm(qkv_a[:, :, 0], q_norm_g)
    ka = rms_norm(qkv_a[:, :, 1], k_norm_g)
    va = qkv_a[:, :, 2]
    slopes = alibi_slopes(DIL_HEADS)
    outs, lses = [], []
    for g, (window, dilation) in enumerate(DIL_GROUPS):
        sl = slice(g * DIL_HEADS_PER_GROUP, (g + 1) * DIL_HEADS_PER_GROUP)
        o_g, lse_g = dilated_window_attention(qa[:, :, sl], ka[:, :, sl], va[:, :, sl],
                                              slopes[sl], window, dilation)
        outs.append(o_g)
        lses.append(lse_g)
    o_groups = jnp.stack(outs)
    w_groups = jax.nn.softmax(jnp.stack(lses), axis=0)
    o_a = jnp.einsum('gbsh,gbshc->bshc', w_groups.astype(o_groups.dtype), o_groups)
    o_a = o_a.reshape(bsz, seq, DIL_OUT)

    qkv_b = qkv_b.reshape(bsz, seq, 3, SB_HEADS, HEAD_DIM)
    o_b = stick_breaking_attention(qkv_b[:, :, 0], qkv_b[:, :, 1], qkv_b[:, :, 2])

    glu_val, glu_gate = jnp.split(glu, 2, axis=-1)
    o_c = conformer_conv(glu_val, glu_gate, conv_w, conv_b, conv_norm_g, conv_norm_b)

    gates = jax.nn.sigmoid(gate_logits.reshape(bsz, seq, N_BRANCH, D_MODEL))
    merged = (gates[:, :, 0] * jnp.einsum('bsc,cd->bsd', o_a, w_branch_a)
              + gates[:, :, 1] * jnp.einsum('bsc,cd->bsd', o_b, w_branch_b)
              + gates[:, :, 2] * jnp.einsum('bsc,cd->bsd', o_c, w_branch_c))
    return jnp.einsum('bsd,de->bse', merged, w_out)


def swiglu(h, w_gate, w_up, w_down):
    a = jnp.einsum('bsd,df->bsf', h, w_gate)
    u = jnp.einsum('bsd,df->bsf', h, w_up)
    return jnp.einsum('bsf,fd->bsd', jax.nn.silu(a) * u, w_down)


def routed_swiglu(h, w_router, b_router, w_exp_gate, w_exp_up, w_exp_down):
    bsz, seq, d = h.shape
    t = h.reshape(-1, d)
    n = t.shape[0]
    logits = jnp.einsum('nd,de->ne', t, w_router).astype(jnp.float32) + b_router
    top_logits, top_idx = lax.top_k(logits, TOP_K)
    gates = jax.nn.softmax(top_logits, axis=-1).astype(h.dtype)
    e_flat = top_idx.reshape(-1)
    g_flat = gates.reshape(-1)
    tok_flat = jnp.arange(n * TOP_K) // TOP_K
    order = jnp.argsort(e_flat)
    e_sorted, tok_sorted, g_sorted = e_flat[order], tok_flat[order], g_flat[order]
    counts = jnp.bincount(e_flat, length=N_EXPERTS)
    start = jnp.cumsum(counts) - counts
    padded = (counts + MOE_BLOCK - 1) // MOE_BLOCK * MOE_BLOCK
    pend = jnp.cumsum(padded)
    pstart = pend - padded
    dest = pstart[e_sorted] + (jnp.arange(n * TOP_K) - start[e_sorted])
    n_blocks = -(-(n * TOP_K) // MOE_BLOCK) + N_EXPERTS
    rows = n_blocks * MOE_BLOCK
    row_tok = jnp.zeros((rows,), jnp.int32).at[dest].set(tok_sorted.astype(jnp.int32))
    row_gate = jnp.zeros((rows,), h.dtype).at[dest].set(g_sorted)
    block_expert = jnp.clip(jnp.searchsorted(pend, jnp.arange(n_blocks) * MOE_BLOCK, side='right'),
                            0, N_EXPERTS - 1)
    xs = t[row_tok].reshape(n_blocks, MOE_BLOCK, d)

    def expert_block(args):
        xb, e = args
        a = xb @ w_exp_gate[e]
        u = xb @ w_exp_up[e]
        return (jax.nn.silu(a) * u) @ w_exp_down[e]

    ys = lax.map(expert_block, (xs, block_expert)).reshape(rows, d) * row_gate[:, None]
    out = jax.ops.segment_sum(ys, row_tok, num_segments=n)
    return out.reshape(bsz, seq, d)


def setup_inputs(seed: int = 0) -> dict:
    key = jax.random.key(seed)
    ks = jax.random.split(key, 22)
    L = DEPTH

    def nrm(k, shape, scale):
        return jax.random.normal(k, shape, jnp.float32) * scale

    def gain(k, shape):
        return 1.0 + 0.02 * jax.random.normal(k, shape, jnp.float32)

    return {
        'x': nrm(ks[0], (BATCH, SEQ, D_MODEL), 1.0),
        'attn_norm_g': gain(ks[1], (L, D_MODEL)),
        'w_in': nrm(ks[2], (L, D_MODEL, IN_COLS), D_MODEL ** -0.5),
        'q_norm_g': gain(ks[3], (L, HEAD_DIM)),
        'k_norm_g': gain(ks[4], (L, HEAD_DIM)),
        'conv_w': nrm(ks[5], (L, CONV_WIDTH, CONV_CH), CONV_WIDTH ** -0.5),
        'conv_b': nrm(ks[6], (L, CONV_CH), 0.02),
        'conv_norm_g': gain(ks[7], (L, CONV_CH)),
        'conv_norm_b': nrm(ks[8], (L, CONV_CH), 0.02),
        'w_branch_a': nrm(ks[9], (L, DIL_OUT, D_MODEL), DIL_OUT ** -0.5),
        'w_branch_b': nrm(ks[10], (L, SB_WIDTH, D_MODEL), SB_WIDTH ** -0.5),
        'w_branch_c': nrm(ks[11], (L, CONV_CH, D_MODEL), CONV_CH ** -0.5),
        'w_out': nrm(ks[12], (L, D_MODEL, D_MODEL), D_MODEL ** -0.5),
        'ffn_norm_g': gain(ks[13], (L, D_MODEL)),
        'w_ffn_gate': nrm(ks[14], (N_DENSE, D_MODEL, D_FF), D_MODEL ** -0.5),
        'w_ffn_up': nrm(ks[15], (N_DENSE, D_MODEL, D_FF), D_MODEL ** -0.5),
        'w_ffn_down': nrm(ks[16], (N_DENSE, D_FF, D_MODEL), D_FF ** -0.5),
        'w_router': nrm(ks[17], (N_MOE, D_MODEL, N_EXPERTS), D_MODEL ** -0.5),
        'b_router': nrm(ks[18], (N_MOE, N_EXPERTS), 0.01),
        'w_exp_gate': nrm(ks[19], (N_MOE, N_EXPERTS, D_MODEL, D_FF_EXPERT), D_MODEL ** -0.5),
        'w_exp_up': nrm(ks[20], (N_MOE, N_EXPERTS, D_MODEL, D_FF_EXPERT), D_MODEL ** -0.5),
        'w_exp_down': nrm(ks[21], (N_MOE, N_EXPERTS, D_FF_EXPERT, D_MODEL), D_FF_EXPERT ** -0.5),
    }


def reference(x, attn_norm_g, w_in, q_norm_g, k_norm_g, conv_w, conv_b, conv_norm_g, conv_norm_b,
              w_branch_a, w_branch_b, w_branch_c, w_out, ffn_norm_g, w_ffn_gate, w_ffn_up,
              w_ffn_down, w_router, b_router, w_exp_gate, w_exp_up, w_exp_down):
    for layer in range(DEPTH):
        h = rms_norm(x, attn_norm_g[layer])
        x = x + hybrid_mixer(h, w_in[layer], q_norm_g[layer], k_norm_g[layer], conv_w[layer],
                             conv_b[layer], conv_norm_g[layer], conv_norm_b[layer],
                             w_branch_a[layer], w_branch_b[layer], w_branch_c[layer], w_out[layer])
        h = rms_norm(x, ffn_norm_g[layer])
        i = layer // 2
        if layer % 2 == 0:
            x = x + swiglu(h, w_ffn_gate[i], w_ffn_up[i], w_ffn_down[i])
        else:
            x = x + routed_swiglu(h, w_router[i], b_router[i], w_exp_gate[i], w_exp_up[i],
                                  w_exp_down[i])
    return x
```

```python
import functools
import math

import jax
import jax.numpy as jnp
from jax import lax
from jax.experimental import pallas as pl
from jax.experimental.pallas import tpu as pltpu

F32 = jnp.float32
BF16 = jnp.bfloat16

D_MODEL = 1024
HEAD_DIM = 64
DIL_GROUPS = ((128, 1), (512, 4), (2048, 16))
DIL_HEADS_PER_GROUP = 4
DIL_HEADS = len(DIL_GROUPS) * DIL_HEADS_PER_GROUP
DIL_WIDTH = DIL_HEADS * HEAD_DIM
DIL_OUT = DIL_HEADS_PER_GROUP * HEAD_DIM
DIL_BLOCK = 128
SB_HEADS = 8
SB_WIDTH = SB_HEADS * HEAD_DIM
CONV_CH = D_MODEL // 2
CONV_WIDTH = 31
N_BRANCH = 3
IN_COLS = 3 * DIL_WIDTH + 3 * SB_WIDTH + 2 * CONV_CH + N_BRANCH * D_MODEL
D_FF = 2816
N_EXPERTS = 8
TOP_K = 2
EPS = 1e-6
ALIBI_MAX_BIAS = 8.0
NEG_BIG = -1e30

COL_GATES = 0
COL_GLU = COL_GATES + N_BRANCH * D_MODEL
COL_SB = COL_GLU + 2 * CONV_CH
COL_DIL = COL_SB + 3 * SB_WIDTH

LANES = 128
SUBLANES = 8
VMEM_LIMIT = 56 * 1024 * 1024

SB_TQ = 256
SB_TK = 128
CONV_ROWS = 64
CONV_PAD = 32
FF_CHUNK = 1408
MOE_TM = 512
COMBINE_TT = 256


def _cparams(*sem):
    return pltpu.CompilerParams(dimension_semantics=sem, vmem_limit_bytes=VMEM_LIMIT)


def _sigmoid(x):
    return 1.0 / (1.0 + jnp.exp(-x))


def _norm_matmul_kernel(x_ref, g_ref, w_ref, o_ref):
    x = x_ref[...]
    ms = jnp.mean(x * x, axis=-1, keepdims=True)
    h = (x * lax.rsqrt(ms + EPS)) * g_ref[...]
    o_ref[...] = jnp.dot(h.astype(BF16), w_ref[...], preferred_element_type=F32).astype(o_ref.dtype)


def _norm_matmul(x, g, w, *, tm, tn):
    n, d = x.shape
    e = w.shape[1]
    return pl.pallas_call(
        _norm_matmul_kernel,
        grid=(e // tn, n // tm),
        in_specs=[pl.BlockSpec((tm, d), lambda j, i: (i, 0)),
                  pl.BlockSpec((1, d), lambda j, i: (0, 0)),
                  pl.BlockSpec((d, tn), lambda j, i: (0, j))],
        out_specs=pl.BlockSpec((tm, tn), lambda j, i: (i, j)),
        out_shape=jax.ShapeDtypeStruct((n, e), BF16),
        compiler_params=_cparams("arbitrary", "arbitrary"),
        name="norm_in_proj",
    )(x, g.reshape(1, d), w)


def _same_head_matrix(w):
    r = lax.broadcasted_iota(jnp.int32, (w, w), 0) // HEAD_DIM
    c = lax.broadcasted_iota(jnp.int32, (w, w), 1) // HEAD_DIM
    return (r == c).astype(F32)


def _head_rms_scale(t, same_head):
    ssq = jnp.dot(t * t, same_head, preferred_element_type=F32, precision=lax.Precision.HIGHEST)
    return lax.rsqrt(ssq * (1.0 / HEAD_DIM) + EPS)


def _dil_attn_kernel(q_ref, k_ref, v_ref, qg_ref, kg_ref, bias_ref, bias0_ref, o_ref, lse_ref,
                     kn_scr, *, nb):
    blk = DIL_BLOCK
    qg = qg_ref[...]
    kg = kg_ref[...]
    same_head = _same_head_matrix(DIL_OUT)

    def norm_k(n, c):
        rows = pl.ds(pl.multiple_of(n * blk, blk), blk)
        k = k_ref[0, rows, :].astype(F32)
        kn_scr[rows, :] = (k * _head_rms_scale(k, same_head) * kg).astype(BF16)
        return c

    lax.fori_loop(0, nb, norm_k, 0)

    def one_block(n, first):
        rows = pl.ds(pl.multiple_of(n * blk, blk), blk)
        q = q_ref[0, rows, :].astype(F32)
        qn = q * _head_rms_scale(q, same_head) * qg
        qn = (qn * (1.0 / math.sqrt(HEAD_DIM))).astype(BF16)
        if first:
            krows = rows
        else:
            krows = pl.ds(pl.multiple_of(n * blk - blk, blk), 2 * blk)
        kk = kn_scr[krows, :]
        vv = v_ref[0, krows, :]
        outs, lses = [], []
        for h in range(DIL_HEADS_PER_GROUP):
            cs = slice(h * HEAD_DIM, (h + 1) * HEAD_DIM)
            s = lax.dot_general(qn[:, cs], kk[:, cs], (((1,), (1,)), ((), ())),
                                preferred_element_type=F32)
            s = s + (bias0_ref[h] if first else bias_ref[h])
            m = jnp.max(s, axis=-1, keepdims=True)
            p = jnp.exp(s - m)
            den = jnp.sum(p, axis=-1, keepdims=True)
            o = jnp.dot(p.astype(BF16), vv[:, cs], preferred_element_type=F32) / den
            outs.append(o)
            lses.append(jnp.broadcast_to(m + jnp.log(den), (blk, HEAD_DIM)))
        o_ref[0, rows, :] = jnp.concatenate(outs, axis=-1).astype(o_ref.dtype)
        lse_ref[0, rows, :] = jnp.concatenate(lses, axis=-1)

    one_block(0, True)

    def body(n, c):
        one_block(n, False)
        return c

    lax.fori_loop(1, nb, body, 0)


def _dil_bias_tables(group, window, dilation):
    reach = window // dilation
    assert reach <= DIL_BLOCK
    slopes = 2.0 ** (-ALIBI_MAX_BIAS * jnp.arange(1, DIL_HEADS + 1, dtype=F32) / DIL_HEADS)
    slopes = slopes[group * DIL_HEADS_PER_GROUP:(group + 1) * DIL_HEADS_PER_GROUP]
    qi = jnp.arange(DIL_BLOCK)[:, None] + DIL_BLOCK
    ki = jnp.arange(2 * DIL_BLOCK)[None, :]
    dist = qi - ki
    valid = (dist >= 0) & (dist <= reach)
    bias = -slopes[:, None, None] * (dist * dilation).astype(F32)[None]
    bias = jnp.where(valid[None], bias, NEG_BIG)
    return bias, bias[:, :, DIL_BLOCK:]


def _dil_attention(proj, q_gain, k_gain, group, window, dilation, bsz, seq):
    e = proj.shape[-1]
    sub = seq // dilation
    nb = sub // DIL_BLOCK
    assert seq % (DIL_BLOCK * dilation) == 0
    view = proj.reshape(bsz, sub, dilation * e)
    w = DIL_OUT
    ecols = e // w

    def col(base):
        off = (base + group * w) // w
        return lambda b, r: (b, 0, r * ecols + off)

    bias, bias0 = _dil_bias_tables(group, window, dilation)
    gain = lambda g: jnp.tile(g, DIL_HEADS_PER_GROUP).reshape(1, w)
    const2 = lambda b, r: (0, 0)
    const3 = lambda b, r: (0, 0, 0)
    o, lse = pl.pallas_call(
        functools.partial(_dil_attn_kernel, nb=nb),
        grid=(bsz, dilation),
        in_specs=[pl.BlockSpec((1, sub, w), col(COL_DIL)),
                  pl.BlockSpec((1, sub, w), col(COL_DIL + DIL_WIDTH)),
                  pl.BlockSpec((1, sub, w), col(COL_DIL + 2 * DIL_WIDTH)),
                  pl.BlockSpec((1, w), const2),
                  pl.BlockSpec((1, w), const2),
                  pl.BlockSpec(bias.shape, const3),
                  pl.BlockSpec(bias0.shape, const3)],
        out_specs=[pl.BlockSpec((1, sub, w), lambda b, r: (b, 0, r)),
                   pl.BlockSpec((1, sub, w), lambda b, r: (b, 0, r))],
        out_shape=[jax.ShapeDtypeStruct((bsz, sub, dilation * w), BF16),
                   jax.ShapeDtypeStruct((bsz, sub, dilation * w), F32)],
        scratch_shapes=[pltpu.VMEM((sub, w), BF16)],
        compiler_params=_cparams("arbitrary", "arbitrary"),
        name=f"dilated_attn_g{group}",
    )(view, view, view, gain(q_gain), gain(k_gain), bias, bias0)
    return o.reshape(bsz * seq, w), lse.reshape(bsz * seq, w)


def _sb_attn_kernel(q_ref, k_ref, v_ref, o_ref, acc_ref):
    tq, tk = SB_TQ, SB_TK
    qi = pl.program_id(2)
    lane = lax.broadcasted_iota(jnp.int32, (tq, LANES), 1)
    q = q_ref[0] * (1.0 / math.sqrt(HEAD_DIM))
    zero = jnp.zeros_like(q)
    qh = (jnp.where(lane < HEAD_DIM, q, zero), jnp.where(lane >= HEAD_DIM, q, zero))
    tri = (lax.broadcasted_iota(jnp.int32, (tk, tk), 0)
           >= lax.broadcasted_iota(jnp.int32, (tk, tk), 1)).astype(BF16)
    acc_ref[...] = jnp.zeros_like(acc_ref)

    def block(kb, run, rel):
        krows = pl.ds(pl.multiple_of(kb * tk, tk), tk)
        kk = k_ref[0, krows, :]
        vv = v_ref[0, krows, :]
        if rel is not None:
            before = (lax.broadcasted_iota(jnp.int32, (tq, tk), 1) + rel * tk
                      < lax.broadcasted_iota(jnp.int32, (tq, tk), 0))
        new_run = []
        for h in range(2):
            z = lax.dot_general(qh[h], kk, (((1,), (1,)), ((), ())), preferred_element_type=F32)
            log_keep = -(jnp.maximum(z, 0.0) + jnp.log(1.0 + jnp.exp(-jnp.abs(z))))
            if rel is not None:
                log_keep = jnp.where(before, log_keep, 0.0)
            hi = log_keep.astype(BF16)
            lo = (log_keep - hi.astype(F32)).astype(BF16)
            suffix = (jnp.dot(hi, tri, preferred_element_type=F32)
                      + jnp.dot(lo, tri, preferred_element_type=F32))
            a = jnp.exp(z + suffix + run[h])
            if rel is not None:
                a = jnp.where(before, a, 0.0)
            acc_ref[h] += jnp.dot(a.astype(BF16), vv, preferred_element_type=F32)
            new_run.append(run[h] + jnp.sum(log_keep, axis=-1, keepdims=True))
        return tuple(new_run)

    run = (jnp.zeros((tq, 1), F32), jnp.zeros((tq, 1), F32))
    n_diag = tq // tk
    for rel in reversed(range(n_diag)):
        run = block(qi * n_diag + rel, run, rel)

    n_full = qi * n_diag

    def body(it, run):
        return block(n_full - 1 - it, run, None)

    lax.fori_loop(0, n_full, body, run)
    o_ref[0] = jnp.where(lane < HEAD_DIM, acc_ref[0], acc_ref[1]).astype(o_ref.dtype)


def _sb_attention(proj, bsz, seq):
    e = proj.shape[-1]
    view = proj.reshape(bsz, seq, e)
    pairs = SB_WIDTH // LANES

    def col(base):
        off = base // LANES
        return off

    qo, ko, vo = col(COL_SB), col(COL_SB + SB_WIDTH), col(COL_SB + 2 * SB_WIDTH)
    out = pl.pallas_call(
        _sb_attn_kernel,
        grid=(bsz, pairs, seq // SB_TQ),
        in_specs=[pl.BlockSpec((1, SB_TQ, LANES), lambda b, p, i: (b, i, qo + p)),
                  pl.BlockSpec((1, seq, LANES), lambda b, p, i: (b, 0, ko + p)),
                  pl.BlockSpec((1, seq, LANES), lambda b, p, i: (b, 0, vo + p))],
        out_specs=pl.BlockSpec((1, SB_TQ, LANES), lambda b, p, i: (b, i, p)),
        out_shape=jax.ShapeDtypeStruct((bsz, seq, SB_WIDTH), BF16),
        scratch_shapes=[pltpu.VMEM((2, SB_TQ, LANES), F32)],
        compiler_params=_cparams("arbitrary", "arbitrary", "arbitrary"),
        name="stick_breaking_attn",
    )(view, view, view)
    return out.reshape(bsz * seq, SB_WIDTH)


def _conv_kernel(val_ref, gate_ref, w_ref, b_ref, g_ref, beta_ref, o_ref, u_scr, *, seq):
    tr, pad, half = CONV_ROWS, CONV_PAD, CONV_CH // 2
    u_scr[pl.ds(0, pad), :] = jnp.zeros((pad, CONV_CH), F32)

    def glu(i, c):
        rows = pl.ds(pl.multiple_of(i * 256, 256), 256)
        val = val_ref[0, rows, :].astype(F32)
        gate = gate_ref[0, rows, :].astype(F32)
        u_scr[pl.ds(pl.multiple_of(pad + i * 256, SUBLANES), 256), :] = val * _sigmoid(gate)
        return c

    lax.fori_loop(0, seq // 256, glu, 0)

    def tile(i, c):
        t0 = pl.multiple_of(i * tr, tr)
        parts = []
        for ch in range(2):
            cs = slice(ch * half, (ch + 1) * half)
            win = u_scr[pl.ds(t0, tr + pad), cs]
            acc = jnp.zeros((tr, half), F32)
            first = pad - (CONV_WIDTH - 1)
            for r in range(SUBLANES):
                offs = [o for o in range(first, first + CONV_WIDTH) if o % SUBLANES == r]
                shifted = win[r:offs[-1] + tr, :]
                for o in offs:
                    w = o - first
                    acc = acc + shifted[o - r:o - r + tr, :] * w_ref[w:w + 1, cs]
            parts.append(acc)
        y = jnp.concatenate(parts, axis=-1) + b_ref[...]
        mu = jnp.mean(y, axis=-1, keepdims=True)
        yc = y - mu
        var = jnp.mean(yc * yc, axis=-1, keepdims=True)
        yn = yc * lax.rsqrt(var + EPS) * g_ref[...] + beta_ref[...]
        o_ref[0, pl.ds(t0, tr), :] = (yn * _sigmoid(yn)).astype(o_ref.dtype)
        return c

    lax.fori_loop(0, seq // tr, tile, 0)


def _conformer_conv(proj, conv_w, conv_b, norm_g, norm_b, bsz, seq):
    e = proj.shape[-1]
    view = proj.reshape(bsz, seq, e)
    c = CONV_CH
    voff, goff = COL_GLU // c, (COL_GLU + c) // c
    const = lambda b: (0, 0)
    out = pl.pallas_call(
        functools.partial(_conv_kernel, seq=seq),
        grid=(bsz,),
        in_specs=[pl.BlockSpec((1, seq, c), lambda b: (b, 0, voff)),
                  pl.BlockSpec((1, seq, c), lambda b: (b, 0, goff)),
                  pl.BlockSpec((CONV_WIDTH, c), const),
                  pl.BlockSpec((1, c), const), pl.BlockSpec((1, c), const), pl.BlockSpec((1, c), const)],
        out_specs=pl.BlockSpec((1, seq, c), lambda b: (b, 0, 0)),
        out_shape=jax.ShapeDtypeStruct((bsz, seq, c), BF16),
        scratch_shapes=[pltpu.VMEM((seq + CONV_PAD, c), F32)],
        compiler_params=_cparams("arbitrary"),
        name="conformer_conv",
    )(view, view, conv_w, conv_b.reshape(1, c), norm_g.reshape(1, c), norm_b.reshape(1, c))
    return out.reshape(bsz * seq, c)


def _merge_kernel(o1_ref, o2_ref, o3_ref, l1_ref, l2_ref, l3_ref, ob_ref, oc_ref,
                  ga_ref, gb_ref, gc_ref, x_ref, wa_ref, wb_ref, wc_ref, wo_ref, out_ref):
    l1, l2, l3 = l1_ref[...], l2_ref[...], l3_ref[...]
    m = jnp.maximum(jnp.maximum(l1, l2), l3)
    e1, e2, e3 = jnp.exp(l1 - m), jnp.exp(l2 - m), jnp.exp(l3 - m)
    o_a = (e1 * o1_ref[...].astype(F32) + e2 * o2_ref[...].astype(F32)
           + e3 * o3_ref[...].astype(F32)) / (e1 + e2 + e3)
    ya = jnp.dot(o_a.astype(BF16), wa_ref[...], preferred_element_type=F32)
    yb = jnp.dot(ob_ref[...], wb_ref[...], preferred_element_type=F32)
    yc = jnp.dot(oc_ref[...], wc_ref[...], preferred_element_type=F32)
    merged = (_sigmoid(ga_ref[...].astype(F32)) * ya + _sigmoid(gb_ref[...].astype(F32)) * yb
              + _sigmoid(gc_ref[...].astype(F32)) * yc)
    out_ref[...] = x_ref[...] + jnp.dot(merged.astype(BF16), wo_ref[...], preferred_element_type=F32)


def _merge(o_groups, lse_groups, o_b, o_c, proj, x, wa, wb, wc, wo, *, tm):
    n, d = x.shape
    row = lambda w, j=0: pl.BlockSpec((tm, w), lambda i: (i, j))
    full = lambda a: pl.BlockSpec(a.shape, lambda i: (0, 0), pipeline_mode=pl.Buffered(1))
    g0 = COL_GATES // d
    return pl.pallas_call(
        _merge_kernel,
        grid=(n // tm,),
        in_specs=[row(DIL_OUT), row(DIL_OUT), row(DIL_OUT), row(DIL_OUT), row(DIL_OUT), row(DIL_OUT),
                  row(SB_WIDTH), row(CONV_CH), row(d, g0), row(d, g0 + 1), row(d, g0 + 2), row(d),
                  full(wa), full(wb), full(wc), full(wo)],
        out_specs=row(d),
        out_shape=jax.ShapeDtypeStruct((n, d), F32),
        compiler_params=_cparams("arbitrary"),
        name="branch_merge_out_proj",
    )(*o_groups, *lse_groups, o_b, o_c, proj, proj, proj, x, wa, wb, wc, wo)


def _swiglu_tile(h, wg_ref, wu_ref, wd_ref):
    y = None
    for c in range(D_FF // FF_CHUNK):
        cs = slice(c * FF_CHUNK, (c + 1) * FF_CHUNK)
        a = jnp.dot(h, wg_ref[:, cs], preferred_element_type=F32)
        u = jnp.dot(h, wu_ref[:, cs], preferred_element_type=F32)
        act = (a * _sigmoid(a) * u).astype(BF16)
        part = jnp.dot(act, wd_ref[cs, :], preferred_element_type=F32)
        y = part if y is None else y + part
    return y


def _dense_ffn_kernel(x_ref, g_ref, wg_ref, wu_ref, wd_ref, o_ref):
    x = x_ref[...]
    ms = jnp.mean(x * x, axis=-1, keepdims=True)
    h = ((x * lax.rsqrt(ms + EPS)) * g_ref[...]).astype(BF16)
    o_ref[...] = x + _swiglu_tile(h, wg_ref, wu_ref, wd_ref)


def _dense_ffn(x, g, wg, wu, wd, *, tm):
    n, d = x.shape
    full = lambda a: pl.BlockSpec(a.shape, lambda i: (0, 0), pipeline_mode=pl.Buffered(1))
    return pl.pallas_call(
        _dense_ffn_kernel,
        grid=(n // tm,),
        in_specs=[pl.BlockSpec((tm, d), lambda i: (i, 0)), pl.BlockSpec((1, d), lambda i: (0, 0)),
                  full(wg), full(wu), full(wd)],
        out_specs=pl.BlockSpec((tm, d), lambda i: (i, 0)),
        out_shape=jax.ShapeDtypeStruct((n, d), F32),
        compiler_params=_cparams("arbitrary"),
        name="dense_swiglu",
    )(x, g.reshape(1, d), wg, wu, wd)


def _router_kernel(x_ref, g_ref, wr_ref, br_ref, h_ref, route_ref):
    x = x_ref[...]
    ms = jnp.mean(x * x, axis=-1, keepdims=True)
    h = (x * lax.rsqrt(ms + EPS)) * g_ref[...]
    h_ref[...] = h
    logits = jnp.dot(h, wr_ref[...], preferred_element_type=F32,
                     precision=lax.Precision.HIGHEST) + br_ref[...]
    lane = lax.broadcasted_iota(jnp.int32, logits.shape, 1)
    m1 = jnp.max(logits, axis=-1, keepdims=True)
    i1 = jnp.min(jnp.where(logits == m1, lane, LANES), axis=-1, keepdims=True)
    rest = jnp.where(lane == i1, -jnp.inf, logits)
    m2 = jnp.max(rest, axis=-1, keepdims=True)
    i2 = jnp.min(jnp.where(rest == m2, lane, LANES), axis=-1, keepdims=True)
    e2 = jnp.exp(m2 - m1)
    g1 = 1.0 / (1.0 + e2)
    g2 = e2 / (1.0 + e2)
    route = jnp.where(lane == 0, i1.astype(F32),
                      jnp.where(lane == 1, i2.astype(F32),
                                jnp.where(lane == 2, g1, jnp.where(lane == 3, g2, 0.0))))
    route_ref[...] = route


def _router(x, g, w_router, b_router, *, tm):
    n, d = x.shape
    wr = jnp.zeros((d, LANES), F32).at[:, :N_EXPERTS].set(w_router)
    br = jnp.full((1, LANES), NEG_BIG, F32).at[0, :N_EXPERTS].set(b_router)
    return pl.pallas_call(
        _router_kernel,
        grid=(n // tm,),
        in_specs=[pl.BlockSpec((tm, d), lambda i: (i, 0)), pl.BlockSpec((1, d), lambda i: (0, 0)),
                  pl.BlockSpec((d, LANES), lambda i: (0, 0)), pl.BlockSpec((1, LANES), lambda i: (0, 0))],
        out_specs=[pl.BlockSpec((tm, d), lambda i: (i, 0)), pl.BlockSpec((tm, LANES), lambda i: (i, 0))],
        out_shape=[jax.ShapeDtypeStruct((n, d), F32), jax.ShapeDtypeStruct((n, LANES), F32)],
        compiler_params=_cparams("arbitrary"),
        name="router_top2",
    )(x, g.reshape(1, d), wr, br)


def _row_gather_start(idx_ref, blk, src_hbm, dst, sem, rows):
    def body(r, c):
        tok = idx_ref[blk, r]
        pltpu.make_async_copy(src_hbm.at[pl.ds(tok, 1), :], dst.at[pl.ds(r, 1), :], sem).start()
        return c

    lax.fori_loop(0, rows, body, 0, unroll=8)


def _row_gather_wait(src_hbm, dst, sem, rows):
    pltpu.make_async_copy(src_hbm.at[pl.ds(0, rows), :], dst, sem).wait()


def _expert_kernel(be_ref, used_ref, tok_ref, h_hbm, wg_ref, wu_ref, wd_ref, y_ref, xbuf, sem):
    i = pl.program_id(0)
    used = used_ref[0]
    slot = i % 2

    @pl.when(i == 0)
    def _():
        _row_gather_start(tok_ref, 0, h_hbm, xbuf.at[0], sem.at[0], MOE_TM)

    @pl.when(i + 1 < used)
    def _():
        _row_gather_start(tok_ref, i + 1, h_hbm, xbuf.at[1 - slot], sem.at[1 - slot], MOE_TM)

    @pl.when(i < used)
    def _():
        _row_gather_wait(h_hbm, xbuf.at[slot], sem.at[slot], MOE_TM)
        y_ref[...] = _swiglu_tile(xbuf[slot].astype(BF16), wg_ref, wu_ref, wd_ref)

    @pl.when(i >= used)
    def _():
        y_ref[...] = jnp.zeros_like(y_ref)


def _experts(h, block_expert, used, row_tok, wg, wu, wd):
    n, d = h.shape
    n_blocks = row_tok.shape[0]
    f = wg.shape[-1]
    one = pl.Buffered(1)
    grid_spec = pltpu.PrefetchScalarGridSpec(
        num_scalar_prefetch=3,
        grid=(n_blocks,),
        in_specs=[pl.BlockSpec(memory_space=pl.ANY),
                  pl.BlockSpec((None, d, f), lambda i, be, us, tk: (be[i], 0, 0), pipeline_mode=one),
                  pl.BlockSpec((None, d, f), lambda i, be, us, tk: (be[i], 0, 0), pipeline_mode=one),
                  pl.BlockSpec((None, f, d), lambda i, be, us, tk: (be[i], 0, 0), pipeline_mode=one)],
        out_specs=pl.BlockSpec((MOE_TM, d), lambda i, be, us, tk: (i, 0)),
        scratch_shapes=[pltpu.VMEM((2, MOE_TM, d), F32), pltpu.SemaphoreType.DMA((2,))],
    )
    return pl.pallas_call(
        _expert_kernel,
        grid_spec=grid_spec,
        out_shape=jax.ShapeDtypeStruct((n_blocks * MOE_TM, d), F32),
        compiler_params=_cparams("arbitrary"),
        name="expert_swiglu",
    )(block_expert, used, row_tok, h, wg, wu, wd)


def _combine_kernel(p0_ref, p1_ref, ys_hbm, x_ref, route_ref, o_ref, buf, sem):
    i = pl.program_id(0)
    nsteps = pl.num_programs(0)
    slot = i % 2
    tt = COMBINE_TT

    def start(step, s):
        _row_gather_start(p0_ref, step, ys_hbm, buf.at[s, 0], sem.at[s, 0], tt)
        _row_gather_start(p1_ref, step, ys_hbm, buf.at[s, 1], sem.at[s, 1], tt)

    @pl.when(i == 0)
    def _():
        start(0, 0)

    @pl.when(i + 1 < nsteps)
    def _():
        start(i + 1, 1 - slot)

    _row_gather_wait(ys_hbm, buf.at[slot, 0], sem.at[slot, 0], tt)
    _row_gather_wait(ys_hbm, buf.at[slot, 1], sem.at[slot, 1], tt)
    route = route_ref[...]
    o_ref[...] = x_ref[...] + route[:, 2:3] * buf[slot, 0] + route[:, 3:4] * buf[slot, 1]


def _combine(x, ys, route, pos0, pos1):
    n, d = x.shape
    tt = COMBINE_TT
    grid_spec = pltpu.PrefetchScalarGridSpec(
        num_scalar_prefetch=2,
        grid=(n // tt,),
        in_specs=[pl.BlockSpec(memory_space=pl.ANY),
                  pl.BlockSpec((tt, d), lambda i, a, b: (i, 0)),
                  pl.BlockSpec((tt, LANES), lambda i, a, b: (i, 0))],
        out_specs=pl.BlockSpec((tt, d), lambda i, a, b: (i, 0)),
        scratch_shapes=[pltpu.VMEM((2, 2, tt, d), F32), pltpu.SemaphoreType.DMA((2, 2))],
    )
    return pl.pallas_call(
        _combine_kernel,
        grid_spec=grid_spec,
        out_shape=jax.ShapeDtypeStruct((n, d), F32),
        compiler_params=_cparams("arbitrary"),
        name="expert_combine",
    )(pos0.reshape(n // tt, tt), pos1.reshape(n // tt, tt), ys, x, route)


def _routed_ffn(x, g, w_router, b_router, wg, wu, wd, *, tm):
    n, d = x.shape
    h, route = _router(x, g, w_router, b_router, tm=tm)
    e_flat = route[:, :TOP_K].astype(jnp.int32).reshape(-1)
    onehot = (e_flat[:, None] == jnp.arange(N_EXPERTS)[None, :]).astype(jnp.int32)
    csum = jnp.cumsum(onehot, axis=0)
    counts = csum[-1]
    rank = jnp.sum((csum - onehot) * onehot, axis=1)
    padded = (counts + MOE_TM - 1) // MOE_TM * MOE_TM
    pend = jnp.cumsum(padded)
    pstart = pend - padded
    dest = (pstart[e_flat] + rank).astype(jnp.int32)
    n_blocks = (n * TOP_K) // MOE_TM + N_EXPERTS
    tok = (jnp.arange(n * TOP_K, dtype=jnp.int32) // TOP_K)
    row_tok = jnp.zeros((n_blocks * MOE_TM,), jnp.int32).at[dest].set(tok).reshape(n_blocks, MOE_TM)
    block_expert = jnp.clip(jnp.searchsorted(pend, jnp.arange(n_blocks) * MOE_TM, side='right'),
                            0, N_EXPERTS - 1).astype(jnp.int32)
    used = (pend[-1:] // MOE_TM).astype(jnp.int32)
    ys = _experts(h, block_expert, used, row_tok, wg, wu, wd)
    pos = dest.reshape(n, TOP_K)
    return _combine(x, ys, route, pos[:, 0], pos[:, 1])


def kernel(x, attn_norm_g, w_in, q_norm_g, k_norm_g, conv_w, conv_b, conv_norm_g, conv_norm_b,
           w_branch_a, w_branch_b, w_branch_c, w_out, ffn_norm_g, w_ffn_gate, w_ffn_up,
           w_ffn_down, w_router, b_router, w_exp_gate, w_exp_up, w_exp_down):
    bsz, seq, d = x.shape
    depth = attn_norm_g.shape[0]
    n = bsz * seq
    tm = 512
    xf = x.reshape(n, d)
    s_dil, s_sb, s_glu = 3 * DIL_WIDTH, 3 * DIL_WIDTH + 3 * SB_WIDTH, 3 * DIL_WIDTH + 3 * SB_WIDTH + 2 * CONV_CH
    for layer in range(depth):
        wl = w_in[layer]
        w_perm = jnp.concatenate([wl[:, s_glu:], wl[:, s_sb:s_glu], wl[:, s_dil:s_sb], wl[:, :s_dil]],
                                 axis=1).astype(BF16)
        proj = _norm_matmul(xf, attn_norm_g[layer], w_perm, tm=tm, tn=IN_COLS // 2)
        o_groups, lse_groups = [], []
        for gi, (window, dilation) in enumerate(DIL_GROUPS):
            o_g, lse_g = _dil_attention(proj, q_norm_g[layer], k_norm_g[layer], gi, window, dilation,
                                        bsz, seq)
            o_groups.append(o_g)
            lse_groups.append(lse_g)
        o_b = _sb_attention(proj, bsz, seq)
        o_c = _conformer_conv(proj, conv_w[layer], conv_b[layer], conv_norm_g[layer],
                              conv_norm_b[layer], bsz, seq)
        xf = _merge(o_groups, lse_groups, o_b, o_c, proj, xf,
                    w_branch_a[layer].astype(BF16), w_branch_b[layer].astype(BF16),
                    w_branch_c[layer].astype(BF16), w_out[layer].astype(BF16), tm=tm)
        i = layer // 2
        if layer % 2 == 0:
            xf = _dense_ffn(xf, ffn_norm_g[layer], w_ffn_gate[i].astype(BF16), w_ffn_up[i].astype(BF16),
                            w_ffn_down[i].astype(BF16), tm=tm)
        else:
            xf = _routed_ffn(xf, ffn_norm_g[layer], w_router[i], b_router[i],
                             w_exp_gate[i].astype(BF16), w_exp_up[i].astype(BF16),
                             w_exp_down[i].astype(BF16), tm=tm)
    return xf.reshape(bsz, seq, d)
```

```python
import functools
import math

import jax
import jax.numpy as jnp
from jax import lax
from jax.experimental import pallas as pl
from jax.experimental.pallas import tpu as pltpu

F32 = jnp.float32
BF16 = jnp.bfloat16

D_MODEL = 1024
HEAD_DIM = 64
DIL_GROUPS = ((128, 1), (512, 4), (2048, 16))
DIL_HEADS_PER_GROUP = 4
DIL_HEADS = len(DIL_GROUPS) * DIL_HEADS_PER_GROUP
DIL_WIDTH = DIL_HEADS * HEAD_DIM
DIL_OUT = DIL_HEADS_PER_GROUP * HEAD_DIM
DIL_BLOCK = 128
SB_HEADS = 8
SB_WIDTH = SB_HEADS * HEAD_DIM
CONV_CH = D_MODEL // 2
CONV_WIDTH = 31
N_BRANCH = 3
IN_COLS = 3 * DIL_WIDTH + 3 * SB_WIDTH + 2 * CONV_CH + N_BRANCH * D_MODEL
D_FF = 2816
N_EXPERTS = 8
TOP_K = 2
EPS = 1e-6
ALIBI_MAX_BIAS = 8.0
NEG_BIG = -1e30

COL_GATES = 0
COL_GLU = COL_GATES + N_BRANCH * D_MODEL
COL_SB = COL_GLU + 2 * CONV_CH
COL_DIL = COL_SB + 3 * SB_WIDTH

LANES = 128
SUBLANES = 8
VMEM_LIMIT = 56 * 1024 * 1024

SB_TQ = 512
SB_TK = 128
CONV_ROWS = 64
CONV_PAD = 32
FF_CHUNK = 1408
MOE_TM = 512
COMBINE_TT = 256


def _cparams(*sem):
    return pltpu.CompilerParams(dimension_semantics=sem, vmem_limit_bytes=VMEM_LIMIT)


def _sigmoid(x):
    return 1.0 / (1.0 + jnp.exp(-x))


def _norm_matmul_kernel(x_ref, g_ref, w_ref, o_ref):
    x = x_ref[...]
    ms = jnp.mean(x * x, axis=-1, keepdims=True)
    h = (x * lax.rsqrt(ms + EPS)) * g_ref[...]
    o_ref[...] = jnp.dot(h.astype(BF16), w_ref[...], preferred_element_type=F32).astype(o_ref.dtype)


def _norm_matmul(x, g, w, *, tm, tn):
    n, d = x.shape
    e = w.shape[1]
    return pl.pallas_call(
        _norm_matmul_kernel,
        grid=(e // tn, n // tm),
        in_specs=[pl.BlockSpec((tm, d), lambda j, i: (i, 0)),
                  pl.BlockSpec((1, d), lambda j, i: (0, 0)),
                  pl.BlockSpec((d, tn), lambda j, i: (0, j))],
        out_specs=pl.BlockSpec((tm, tn), lambda j, i: (i, j)),
        out_shape=jax.ShapeDtypeStruct((n, e), BF16),
        compiler_params=_cparams("arbitrary", "arbitrary"),
        name="norm_in_proj",
    )(x, g.reshape(1, d), w)


def _same_head_matrix(w):
    r = lax.broadcasted_iota(jnp.int32, (w, w), 0) // HEAD_DIM
    c = lax.broadcasted_iota(jnp.int32, (w, w), 1) // HEAD_DIM
    return (r == c).astype(F32)


def _head_rms_scale(t, same_head):
    ssq = jnp.dot(t * t, same_head, preferred_element_type=F32, precision=lax.Precision.HIGHEST)
    return lax.rsqrt(ssq * (1.0 / HEAD_DIM) + EPS)


def _dil_attn_kernel(q_ref, k_ref, v_ref, qg_ref, kg_ref, bias_ref, bias0_ref, o_ref, lse_ref,
                     kn_scr, *, nb):
    blk = DIL_BLOCK
    qg = qg_ref[...]
    kg = kg_ref[...]
    same_head = _same_head_matrix(DIL_OUT)

    def norm_k(n, c):
        rows = pl.ds(pl.multiple_of(n * blk, blk), blk)
        k = k_ref[0, rows, :].astype(F32)
        kn_scr[rows, :] = (k * _head_rms_scale(k, same_head) * kg).astype(BF16)
        return c

    lax.fori_loop(0, nb, norm_k, 0)

    def one_block(n, first):
        rows = pl.ds(pl.multiple_of(n * blk, blk), blk)
        q = q_ref[0, rows, :].astype(F32)
        qn = q * _head_rms_scale(q, same_head) * qg
        qn = (qn * (1.0 / math.sqrt(HEAD_DIM))).astype(BF16)
        if first:
            krows = rows
        else:
            krows = pl.ds(pl.multiple_of(n * blk - blk, blk), 2 * blk)
        kk = kn_scr[krows, :]
        vv = v_ref[0, krows, :]
        outs, lses = [], []
        for h in range(DIL_HEADS_PER_GROUP):
            cs = slice(h * HEAD_DIM, (h + 1) * HEAD_DIM)
            s = lax.dot_general(qn[:, cs], kk[:, cs], (((1,), (1,)), ((), ())),
                                preferred_element_type=F32)
            s = s + (bias0_ref[h] if first else bias_ref[h])
            m = jnp.max(s, axis=-1, keepdims=True)
            p = jnp.exp(s - m)
            den = jnp.sum(p, axis=-1, keepdims=True)
            o = jnp.dot(p.astype(BF16), vv[:, cs], preferred_element_type=F32) / den
            outs.append(o)
            lses.append(jnp.broadcast_to(m + jnp.log(den), (blk, HEAD_DIM)))
        o_ref[0, rows, :] = jnp.concatenate(outs, axis=-1).astype(o_ref.dtype)
        lse_ref[0, rows, :] = jnp.concatenate(lses, axis=-1)

    one_block(0, True)

    def body(n, c):
        one_block(n, False)
        return c

    lax.fori_loop(1, nb, body, 0)


def _dil_bias_tables(group, window, dilation):
    reach = window // dilation
    assert reach <= DIL_BLOCK
    slopes = 2.0 ** (-ALIBI_MAX_BIAS * jnp.arange(1, DIL_HEADS + 1, dtype=F32) / DIL_HEADS)
    slopes = slopes[group * DIL_HEADS_PER_GROUP:(group + 1) * DIL_HEADS_PER_GROUP]
    qi = jnp.arange(DIL_BLOCK)[:, None] + DIL_BLOCK
    ki = jnp.arange(2 * DIL_BLOCK)[None, :]
    dist = qi - ki
    valid = (dist >= 0) & (dist <= reach)
    bias = -slopes[:, None, None] * (dist * dilation).astype(F32)[None]
    bias = jnp.where(valid[None], bias, NEG_BIG)
    return bias, bias[:, :, DIL_BLOCK:]


def _dil_attention(proj, q_gain, k_gain, group, window, dilation, bsz, seq):
    e = proj.shape[-1]
    sub = seq // dilation
    nb = sub // DIL_BLOCK
    assert seq % (DIL_BLOCK * dilation) == 0
    view = proj.reshape(bsz, sub, dilation * e)
    w = DIL_OUT
    ecols = e // w

    def col(base):
        off = (base + group * w) // w
        return lambda b, r: (b, 0, r * ecols + off)

    bias, bias0 = _dil_bias_tables(group, window, dilation)
    gain = lambda g: jnp.tile(g, DIL_HEADS_PER_GROUP).reshape(1, w)
    const2 = lambda b, r: (0, 0)
    const3 = lambda b, r: (0, 0, 0)
    o, lse = pl.pallas_call(
        functools.partial(_dil_attn_kernel, nb=nb),
        grid=(bsz, dilation),
        in_specs=[pl.BlockSpec((1, sub, w), col(COL_DIL)),
                  pl.BlockSpec((1, sub, w), col(COL_DIL + DIL_WIDTH)),
                  pl.BlockSpec((1, sub, w), col(COL_DIL + 2 * DIL_WIDTH)),
                  pl.BlockSpec((1, w), const2),
                  pl.BlockSpec((1, w), const2),
                  pl.BlockSpec(bias.shape, const3),
                  pl.BlockSpec(bias0.shape, const3)],
        out_specs=[pl.BlockSpec((1, sub, w), lambda b, r: (b, 0, r)),
                   pl.BlockSpec((1, sub, w), lambda b, r: (b, 0, r))],
        out_shape=[jax.ShapeDtypeStruct((bsz, sub, dilation * w), BF16),
                   jax.ShapeDtypeStruct((bsz, sub, dilation * w), F32)],
        scratch_shapes=[pltpu.VMEM((sub, w), BF16)],
        compiler_params=_cparams("arbitrary", "arbitrary"),
        name=f"dilated_attn_g{group}",
    )(view, view, view, gain(q_gain), gain(k_gain), bias, bias0)
    return o.reshape(bsz * seq, w), lse.reshape(bsz * seq, w)


def _sb_attn_kernel(q_ref, k_ref, v_ref, o_ref, vcat_scr, acc_scr, *, seq):
    tq, tk, nblk = SB_TQ, SB_TK, SB_TQ // SB_TK
    qi = pl.program_id(2)

    @pl.when(qi == 0)
    def _():
        chan = lax.broadcasted_iota(jnp.int32, (LANES, tk), 0)

        def build(kb, c):
            vt = v_ref[0, pl.ds(pl.multiple_of(kb * tk, tk), tk), :].astype(F32).T
            vcat_scr[kb] = jnp.concatenate(
                [jnp.where(chan < HEAD_DIM, vt, 0.0), jnp.where(chan >= HEAD_DIM, vt, 0.0)],
                axis=1).astype(BF16)
            return c

        lax.fori_loop(0, seq // tk, build, 0)

    lane = lax.broadcasted_iota(jnp.int32, (tq, LANES), 1)
    q = q_ref[0] * (1.0 / math.sqrt(HEAD_DIM))
    zero = jnp.zeros_like(q)
    qcat = jnp.concatenate([jnp.where(lane < HEAD_DIM, q, zero), jnp.where(lane >= HEAD_DIM, q, zero)],
                           axis=0)
    qcat_t = qcat.astype(F32).T.astype(BF16)
    neg_tri = jnp.where(lax.broadcasted_iota(jnp.int32, (tk, tk), 1)
                        >= lax.broadcasted_iota(jnp.int32, (tk, tk), 0), -1.0, 0.0).astype(BF16)
    acc_scr[...] = jnp.zeros_like(acc_scr)

    def scores(kb, rel):
        kk = k_ref[0, pl.ds(pl.multiple_of(kb * tk, tk), tk), :]
        zt = lax.dot_general(kk, qcat, (((1,), (1,)), ((), ())), preferred_element_type=F32)
        neg_abs = lax.bitcast_convert_type(
            lax.bitcast_convert_type(zt, jnp.uint32) | jnp.uint32(0x80000000), F32)
        sp = jnp.maximum(zt, 0.0) + jnp.log(1.0 + jnp.exp(neg_abs))
        before = None
        if rel is not None:
            kpos = lax.broadcasted_iota(jnp.int32, (tk, 2 * tq), 0) + rel * tk
            qpos = lax.broadcasted_iota(jnp.int32, (tk, 2 * tq), 1) & (tq - 1)
            before = kpos < qpos
            sp = jnp.where(before, sp, 0.0)
        return kk, zt[0:1, :], sp.astype(BF16), before

    def weights(state, run):
        kk, zt0, sp, before = state
        arg = jnp.dot(jnp.concatenate([neg_tri, kk], axis=1),
                      jnp.concatenate([sp, qcat_t], axis=0), preferred_element_type=F32)
        a = jnp.exp(arg - run)
        if before is not None:
            a = jnp.where(before, a, 0.0)
        a = a.astype(BF16)
        acat = jnp.concatenate([a[:, :tq], a[:, tq:]], axis=0)
        return acat, run + (zt0 - arg[0:1, :])

    def values(kb, acat):
        acc_scr[...] += jnp.dot(vcat_scr[kb], acat, preferred_element_type=F32)

    def run_blocks(blocks, run):
        n = len(blocks)
        st, ac = [None] * n, [None] * n
        for step in range(n + 2):
            if step < n:
                st[step] = scores(*blocks[step])
            if 0 <= step - 1 < n:
                ac[step - 1], run = weights(st[step - 1], run)
            if 0 <= step - 2 < n:
                values(blocks[step - 2][0], ac[step - 2])
        return run

    run = jnp.zeros((1, 2 * tq), F32)
    run = run_blocks([(qi * nblk + rel, rel) for rel in reversed(range(nblk))], run)

    def chunk(it, run):
        base = (qi - 1 - it) * nblk
        return run_blocks([(base + j, None) for j in reversed(range(nblk))], run)

    lax.fori_loop(0, qi, chunk, run)
    o_ref[0] = acc_scr[...].T.astype(o_ref.dtype)


def _sb_attention(proj, bsz, seq):
    e = proj.shape[-1]
    view = proj.reshape(bsz, seq, e)
    pairs = SB_WIDTH // LANES
    qo, ko, vo = COL_SB // LANES, (COL_SB + SB_WIDTH) // LANES, (COL_SB + 2 * SB_WIDTH) // LANES
    out = pl.pallas_call(
        functools.partial(_sb_attn_kernel, seq=seq),
        grid=(bsz, pairs, seq // SB_TQ),
        in_specs=[pl.BlockSpec((1, SB_TQ, LANES), lambda b, p, i: (b, i, qo + p)),
                  pl.BlockSpec((1, seq, LANES), lambda b, p, i: (b, 0, ko + p)),
                  pl.BlockSpec((1, seq, LANES), lambda b, p, i: (b, 0, vo + p))],
        out_specs=pl.BlockSpec((1, SB_TQ, LANES), lambda b, p, i: (b, i, p)),
        out_shape=jax.ShapeDtypeStruct((bsz, seq, SB_WIDTH), BF16),
        scratch_shapes=[pltpu.VMEM((seq // SB_TK, LANES, 2 * SB_TK), BF16),
                        pltpu.VMEM((LANES, SB_TQ), F32)],
        compiler_params=_cparams("arbitrary", "arbitrary", "arbitrary"),
        name="stick_breaking_attn",
    )(view, view, view)
    return out.reshape(bsz * seq, SB_WIDTH)


def _conv_kernel(val_ref, gate_ref, w_ref, b_ref, g_ref, beta_ref, o_ref, u_scr, *, seq):
    tr, pad, half = CONV_ROWS, CONV_PAD, CONV_CH // 2
    u_scr[pl.ds(0, pad), :] = jnp.zeros((pad, CONV_CH), F32)

    def glu(i, c):
        rows = pl.ds(pl.multiple_of(i * 256, 256), 256)
        val = val_ref[0, rows, :].astype(F32)
        gate = gate_ref[0, rows, :].astype(F32)
        u_scr[pl.ds(pl.multiple_of(pad + i * 256, SUBLANES), 256), :] = val * _sigmoid(gate)
        return c

    lax.fori_loop(0, seq // 256, glu, 0)

    def tile(i, c):
        t0 = pl.multiple_of(i * tr, tr)
        parts = []
        for ch in range(2):
            cs = slice(ch * half, (ch + 1) * half)
            win = u_scr[pl.ds(t0, tr + pad), cs]
            acc = jnp.zeros((tr, half), F32)
            first = pad - (CONV_WIDTH - 1)
            for r in range(SUBLANES):
                offs = [o for o in range(first, first + CONV_WIDTH) if o % SUBLANES == r]
                shifted = win[r:offs[-1] + tr, :]
                for o in offs:
                    w = o - first
                    acc = acc + shifted[o - r:o - r + tr, :] * w_ref[w:w + 1, cs]
            parts.append(acc)
        y = jnp.concatenate(parts, axis=-1) + b_ref[...]
        mu = jnp.mean(y, axis=-1, keepdims=True)
        yc = y - mu
        var = jnp.mean(yc * yc, axis=-1, keepdims=True)
        yn = yc * lax.rsqrt(var + EPS) * g_ref[...] + beta_ref[...]
        o_ref[0, pl.ds(t0, tr), :] = (yn * _sigmoid(yn)).astype(o_ref.dtype)
        return c

    lax.fori_loop(0, seq // tr, tile, 0)


def _conformer_conv(proj, conv_w, conv_b, norm_g, norm_b, bsz, seq):
    e = proj.shape[-1]
    view = proj.reshape(bsz, seq, e)
    c = CONV_CH
    voff, goff = COL_GLU // c, (COL_GLU + c) // c
    const = lambda b: (0, 0)
    out = pl.pallas_call(
        functools.partial(_conv_kernel, seq=seq),
        grid=(bsz,),
        in_specs=[pl.BlockSpec((1, seq, c), lambda b: (b, 0, voff)),
                  pl.BlockSpec((1, seq, c), lambda b: (b, 0, goff)),
                  pl.BlockSpec((CONV_WIDTH, c), const),
                  pl.BlockSpec((1, c), const), pl.BlockSpec((1, c), const), pl.BlockSpec((1, c), const)],
        out_specs=pl.BlockSpec((1, seq, c), lambda b: (b, 0, 0)),
        out_shape=jax.ShapeDtypeStruct((bsz, seq, c), BF16),
        scratch_shapes=[pltpu.VMEM((seq + CONV_PAD, c), F32)],
        compiler_params=_cparams("arbitrary"),
        name="conformer_conv",
    )(view, view, conv_w, conv_b.reshape(1, c), norm_g.reshape(1, c), norm_b.reshape(1, c))
    return out.reshape(bsz * seq, c)


def _merge_kernel(o1_ref, o2_ref, o3_ref, l1_ref, l2_ref, l3_ref, ob_ref, oc_ref,
                  ga_ref, gb_ref, gc_ref, x_ref, wa_ref, wb_ref, wc_ref, wo_ref, out_ref):
    l1, l2, l3 = l1_ref[...], l2_ref[...], l3_ref[...]
    m = jnp.maximum(jnp.maximum(l1, l2), l3)
    e1, e2, e3 = jnp.exp(l1 - m), jnp.exp(l2 - m), jnp.exp(l3 - m)
    o_a = (e1 * o1_ref[...].astype(F32) + e2 * o2_ref[...].astype(F32)
           + e3 * o3_ref[...].astype(F32)) / (e1 + e2 + e3)
    ya = jnp.dot(o_a.astype(BF16), wa_ref[...], preferred_element_type=F32)
    yb = jnp.dot(ob_ref[...], wb_ref[...], preferred_element_type=F32)
    yc = jnp.dot(oc_ref[...], wc_ref[...], preferred_element_type=F32)
    merged = (_sigmoid(ga_ref[...].astype(F32)) * ya + _sigmoid(gb_ref[...].astype(F32)) * yb
              + _sigmoid(gc_ref[...].astype(F32)) * yc)
    out_ref[...] = x_ref[...] + jnp.dot(merged.astype(BF16), wo_ref[...], preferred_element_type=F32)


def _merge(o_groups, lse_groups, o_b, o_c, proj, x, wa, wb, wc, wo, *, tm):
    n, d = x.shape
    row = lambda w, j=0: pl.BlockSpec((tm, w), lambda i: (i, j))
    full = lambda a: pl.BlockSpec(a.shape, lambda i: (0, 0), pipeline_mode=pl.Buffered(1))
    g0 = COL_GATES // d
    return pl.pallas_call(
        _merge_kernel,
        grid=(n // tm,),
        in_specs=[row(DIL_OUT), row(DIL_OUT), row(DIL_OUT), row(DIL_OUT), row(DIL_OUT), row(DIL_OUT),
                  row(SB_WIDTH), row(CONV_CH), row(d, g0), row(d, g0 + 1), row(d, g0 + 2), row(d),
                  full(wa), full(wb), full(wc), full(wo)],
        out_specs=row(d),
        out_shape=jax.ShapeDtypeStruct((n, d), F32),
        compiler_params=_cparams("arbitrary"),
        name="branch_merge_out_proj",
    )(*o_groups, *lse_groups, o_b, o_c, proj, proj, proj, x, wa, wb, wc, wo)


def _swiglu_tile(h, wg_ref, wu_ref, wd_ref):
    y = None
    for c in range(D_FF // FF_CHUNK):
        cs = slice(c * FF_CHUNK, (c + 1) * FF_CHUNK)
        a = jnp.dot(h, wg_ref[:, cs], preferred_element_type=F32)
        u = jnp.dot(h, wu_ref[:, cs], preferred_element_type=F32)
        act = (a * _sigmoid(a) * u).astype(BF16)
        part = jnp.dot(act, wd_ref[cs, :], preferred_element_type=F32)
        y = part if y is None else y + part
    return y


def _dense_ffn_kernel(x_ref, g_ref, wg_ref, wu_ref, wd_ref, o_ref):
    x = x_ref[...]
    ms = jnp.mean(x * x, axis=-1, keepdims=True)
    h = ((x * lax.rsqrt(ms + EPS)) * g_ref[...]).astype(BF16)
    o_ref[...] = x + _swiglu_tile(h, wg_ref, wu_ref, wd_ref)


def _dense_ffn(x, g, wg, wu, wd, *, tm):
    n, d = x.shape
    full = lambda a: pl.BlockSpec(a.shape, lambda i: (0, 0), pipeline_mode=pl.Buffered(1))
    return pl.pallas_call(
        _dense_ffn_kernel,
        grid=(n // tm,),
        in_specs=[pl.BlockSpec((tm, d), lambda i: (i, 0)), pl.BlockSpec((1, d), lambda i: (0, 0)),
                  full(wg), full(wu), full(wd)],
        out_specs=pl.BlockSpec((tm, d), lambda i: (i, 0)),
        out_shape=jax.ShapeDtypeStruct((n, d), F32),
        compiler_params=_cparams("arbitrary"),
        name="dense_swiglu",
    )(x, g.reshape(1, d), wg, wu, wd)


def _router_kernel(x_ref, g_ref, wr_ref, br_ref, h_ref, route_ref):
    x = x_ref[...]
    ms = jnp.mean(x * x, axis=-1, keepdims=True)
    h = (x * lax.rsqrt(ms + EPS)) * g_ref[...]
    h_ref[...] = h
    logits = jnp.dot(h, wr_ref[...], preferred_element_type=F32,
                     precision=lax.Precision.HIGHEST) + br_ref[...]
    lane = lax.broadcasted_iota(jnp.int32, logits.shape, 1)
    m1 = jnp.max(logits, axis=-1, keepdims=True)
    i1 = jnp.min(jnp.where(logits == m1, lane, LANES), axis=-1, keepdims=True)
    rest = jnp.where(lane == i1, -jnp.inf, logits)
    m2 = jnp.max(rest, axis=-1, keepdims=True)
    i2 = jnp.min(jnp.where(rest == m2, lane, LANES), axis=-1, keepdims=True)
    e2 = jnp.exp(m2 - m1)
    g1 = 1.0 / (1.0 + e2)
    g2 = e2 / (1.0 + e2)
    route = jnp.where(lane == 0, i1.astype(F32),
                      jnp.where(lane == 1, i2.astype(F32),
                                jnp.where(lane == 2, g1, jnp.where(lane == 3, g2, 0.0))))
    route_ref[...] = route


def _router(x, g, w_router, b_router, *, tm):
    n, d = x.shape
    wr = jnp.zeros((d, LANES), F32).at[:, :N_EXPERTS].set(w_router)
    br = jnp.full((1, LANES), NEG_BIG, F32).at[0, :N_EXPERTS].set(b_router)
    return pl.pallas_call(
        _router_kernel,
        grid=(n // tm,),
        in_specs=[pl.BlockSpec((tm, d), lambda i: (i, 0)), pl.BlockSpec((1, d), lambda i: (0, 0)),
                  pl.BlockSpec((d, LANES), lambda i: (0, 0)), pl.BlockSpec((1, LANES), lambda i: (0, 0))],
        out_specs=[pl.BlockSpec((tm, d), lambda i: (i, 0)), pl.BlockSpec((tm, LANES), lambda i: (i, 0))],
        out_shape=[jax.ShapeDtypeStruct((n, d), F32), jax.ShapeDtypeStruct((n, LANES), F32)],
        compiler_params=_cparams("arbitrary"),
        name="router_top2",
    )(x, g.reshape(1, d), wr, br)


def _row_gather_start(idx_ref, blk, src_hbm, dst, sem, rows):
    def body(r, c):
        tok = idx_ref[blk, r]
        pltpu.make_async_copy(src_hbm.at[pl.ds(tok, 1), :], dst.at[pl.ds(r, 1), :], sem).start()
        return c

    lax.fori_loop(0, rows, body, 0, unroll=8)


def _row_gather_wait(src_hbm, dst, sem, rows):
    pltpu.make_async_copy(src_hbm.at[pl.ds(0, rows), :], dst, sem).wait()


def _expert_kernel(be_ref, used_ref, tok_ref, h_hbm, wg_ref, wu_ref, wd_ref, y_ref, xbuf, sem):
    i = pl.program_id(0)
    used = used_ref[0]
    slot = i % 2

    @pl.when(i == 0)
    def _():
        _row_gather_start(tok_ref, 0, h_hbm, xbuf.at[0], sem.at[0], MOE_TM)

    @pl.when(i + 1 < used)
    def _():
        _row_gather_start(tok_ref, i + 1, h_hbm, xbuf.at[1 - slot], sem.at[1 - slot], MOE_TM)

    @pl.when(i < used)
    def _():
        _row_gather_wait(h_hbm, xbuf.at[slot], sem.at[slot], MOE_TM)
        y_ref[...] = _swiglu_tile(xbuf[slot].astype(BF16), wg_ref, wu_ref, wd_ref)

    @pl.when(i >= used)
    def _():
        y_ref[...] = jnp.zeros_like(y_ref)


def _experts(h, block_expert, used, row_tok, wg, wu, wd):
    n, d = h.shape
    n_blocks = row_tok.shape[0]
    f = wg.shape[-1]
    one = pl.Buffered(1)
    grid_spec = pltpu.PrefetchScalarGridSpec(
        num_scalar_prefetch=3,
        grid=(n_blocks,),
        in_specs=[pl.BlockSpec(memory_space=pl.ANY),
                  pl.BlockSpec((None, d, f), lambda i, be, us, tk: (be[i], 0, 0), pipeline_mode=one),
                  pl.BlockSpec((None, d, f), lambda i, be, us, tk: (be[i], 0, 0), pipeline_mode=one),
                  pl.BlockSpec((None, f, d), lambda i, be, us, tk: (be[i], 0, 0), pipeline_mode=one)],
        out_specs=pl.BlockSpec((MOE_TM, d), lambda i, be, us, tk: (i, 0)),
        scratch_shapes=[pltpu.VMEM((2, MOE_TM, d), F32), pltpu.SemaphoreType.DMA((2,))],
    )
    return pl.pallas_call(
        _expert_kernel,
        grid_spec=grid_spec,
        out_shape=jax.ShapeDtypeStruct((n_blocks * MOE_TM, d), F32),
        compiler_params=_cparams("arbitrary"),
        name="expert_swiglu",
    )(block_expert, used, row_tok, h, wg, wu, wd)


def _combine_kernel(p0_ref, p1_ref, ys_hbm, x_ref, route_ref, o_ref, buf, sem):
    i = pl.program_id(0)
    nsteps = pl.num_programs(0)
    slot = i % 2
    tt = COMBINE_TT

    def start(step, s):
        _row_gather_start(p0_ref, step, ys_hbm, buf.at[s, 0], sem.at[s, 0], tt)
        _row_gather_start(p1_ref, step, ys_hbm, buf.at[s, 1], sem.at[s, 1], tt)

    @pl.when(i == 0)
    def _():
        start(0, 0)

    @pl.when(i + 1 < nsteps)
    def _():
        start(i + 1, 1 - slot)

    _row_gather_wait(ys_hbm, buf.at[slot, 0], sem.at[slot, 0], tt)
    _row_gather_wait(ys_hbm, buf.at[slot, 1], sem.at[slot, 1], tt)
    route = route_ref[...]
    o_ref[...] = x_ref[...] + route[:, 2:3] * buf[slot, 0] + route[:, 3:4] * buf[slot, 1]


def _combine(x, ys, route, pos0, pos1):
    n, d = x.shape
    tt = COMBINE_TT
    grid_spec = pltpu.PrefetchScalarGridSpec(
        num_scalar_prefetch=2,
        grid=(n // tt,),
        in_specs=[pl.BlockSpec(memory_space=pl.ANY),
                  pl.BlockSpec((tt, d), lambda i, a, b: (i, 0)),
                  pl.BlockSpec((tt, LANES), lambda i, a, b: (i, 0))],
        out_specs=pl.BlockSpec((tt, d), lambda i, a, b: (i, 0)),
        scratch_shapes=[pltpu.VMEM((2, 2, tt, d), F32), pltpu.SemaphoreType.DMA((2, 2))],
    )
    return pl.pallas_call(
        _combine_kernel,
        grid_spec=grid_spec,
        out_shape=jax.ShapeDtypeStruct((n, d), F32),
        compiler_params=_cparams("arbitrary"),
        name="expert_combine",
    )(pos0.reshape(n // tt, tt), pos1.reshape(n // tt, tt), ys, x, route)


def _routed_ffn(x, g, w_router, b_router, wg, wu, wd, *, tm):
    n, d = x.shape
    h, route = _router(x, g, w_router, b_router, tm=tm)
    e_flat = route[:, :TOP_K].astype(jnp.int32).reshape(-1)
    onehot = (e_flat[:, None] == jnp.arange(N_EXPERTS)[None, :]).astype(jnp.int32)
    csum = jnp.cumsum(onehot, axis=0)
    counts = csum[-1]
    rank = jnp.sum((csum - onehot) * onehot, axis=1)
    padded = (counts + MOE_TM - 1) // MOE_TM * MOE_TM
    pend = jnp.cumsum(padded)
    pstart = pend - padded
    dest = (pstart[e_flat] + rank).astype(jnp.int32)
    n_blocks = (n * TOP_K) // MOE_TM + N_EXPERTS
    tok = (jnp.arange(n * TOP_K, dtype=jnp.int32) // TOP_K)
    row_tok = jnp.zeros((n_blocks * MOE_TM,), jnp.int32).at[dest].set(tok).reshape(n_blocks, MOE_TM)
    block_expert = jnp.clip(jnp.searchsorted(pend, jnp.arange(n_blocks) * MOE_TM, side='right'),
                            0, N_EXPERTS - 1).astype(jnp.int32)
    used = (pend[-1:] // MOE_TM).astype(jnp.int32)
    ys = _experts(h, block_expert, used, row_tok, wg, wu, wd)
    pos = dest.reshape(n, TOP_K)
    return _combine(x, ys, route, pos[:, 0], pos[:, 1])


def kernel(x, attn_norm_g, w_in, q_norm_g, k_norm_g, conv_w, conv_b, conv_norm_g, conv_norm_b,
           w_branch_a, w_branch_b, w_branch_c, w_out, ffn_norm_g, w_ffn_gate, w_ffn_up,
           w_ffn_down, w_router, b_router, w_exp_gate, w_exp_up, w_exp_down):
    bsz, seq, d = x.shape
    depth = attn_norm_g.shape[0]
    n = bsz * seq
    tm = 512
    xf = x.reshape(n, d)
    s_dil, s_sb, s_glu = 3 * DIL_WIDTH, 3 * DIL_WIDTH + 3 * SB_WIDTH, 3 * DIL_WIDTH + 3 * SB_WIDTH + 2 * CONV_CH
    for layer in range(depth):
        wl = w_in[layer]
        w_perm = jnp.concatenate([wl[:, s_glu:], wl[:, s_sb:s_glu], wl[:, s_dil:s_sb], wl[:, :s_dil]],
                                 axis=1).astype(BF16)
        proj = _norm_matmul(xf, attn_norm_g[layer], w_perm, tm=tm, tn=IN_COLS // 2)
        o_groups, lse_groups = [], []
        for gi, (window, dilation) in enumerate(DIL_GROUPS):
            o_g, lse_g = _dil_attention(proj, q_norm_g[layer], k_norm_g[layer], gi, window, dilation,
                                        bsz, seq)
            o_groups.append(o_g)
            lse_groups.append(lse_g)
        o_b = _sb_attention(proj, bsz, seq)
        o_c = _conformer_conv(proj, conv_w[layer], conv_b[layer], conv_norm_g[layer],
                              conv_norm_b[layer], bsz, seq)
        xf = _merge(o_groups, lse_groups, o_b, o_c, proj, xf,
                    w_branch_a[layer].astype(BF16), w_branch_b[layer].astype(BF16),
                    w_branch_c[layer].astype(BF16), w_out[layer].astype(BF16), tm=tm)
        i = layer // 2
        if layer % 2 == 0:
            xf = _dense_ffn(xf, ffn_norm_g[layer], w_ffn_gate[i].astype(BF16), w_ffn_up[i].astype(BF16),
                            w_ffn_down[i].astype(BF16), tm=tm)
        else:
            xf = _routed_ffn(xf, ffn_norm_g[layer], w_router[i], b_router[i],
                             w_exp_gate[i].astype(BF16), w_exp_up[i].astype(BF16),
                             w_exp_down[i].astype(BF16), tm=tm)
    return xf.reshape(bsz, seq, d)
```

```python
import functools
import math

import jax
import jax.numpy as jnp
from jax import lax
from jax.experimental import pallas as pl
from jax.experimental.pallas import tpu as pltpu

F32 = jnp.float32
BF16 = jnp.bfloat16

D_MODEL = 1024
HEAD_DIM = 64
DIL_GROUPS = ((128, 1), (512, 4), (2048, 16))
DIL_HEADS_PER_GROUP = 4
DIL_HEADS = len(DIL_GROUPS) * DIL_HEADS_PER_GROUP
DIL_WIDTH = DIL_HEADS * HEAD_DIM
DIL_OUT = DIL_HEADS_PER_GROUP * HEAD_DIM
DIL_BLOCK = 128
SB_HEADS = 8
SB_WIDTH = SB_HEADS * HEAD_DIM
CONV_CH = D_MODEL // 2
CONV_WIDTH = 31
N_BRANCH = 3
IN_COLS = 3 * DIL_WIDTH + 3 * SB_WIDTH + 2 * CONV_CH + N_BRANCH * D_MODEL
D_FF = 2816
N_EXPERTS = 8
TOP_K = 2
EPS = 1e-6
ALIBI_MAX_BIAS = 8.0
NEG_BIG = -1e30

COL_GATES = 0
COL_GLU = COL_GATES + N_BRANCH * D_MODEL
COL_SB = COL_GLU + 2 * CONV_CH
COL_DIL = COL_SB + 3 * SB_WIDTH

LANES = 128
SUBLANES = 8
VMEM_LIMIT = 56 * 1024 * 1024

DIL_UNITS = 4
SB_TQ = 512
SB_TK = 128
CONV_ROWS = 64
CONV_PAD = 32
FF_CHUNK = 1408
MOE_TM = 512
COMBINE_TT = 256


def _cparams(*sem):
    return pltpu.CompilerParams(dimension_semantics=sem, vmem_limit_bytes=VMEM_LIMIT)


def _sigmoid(x):
    return 1.0 / (1.0 + jnp.exp(-x))


def _norm_matmul_kernel(x_ref, g_ref, w_ref, o_ref):
    x = x_ref[...]
    ms = jnp.mean(x * x, axis=-1, keepdims=True)
    h = (x * lax.rsqrt(ms + EPS)) * g_ref[...]
    o_ref[...] = jnp.dot(h.astype(BF16), w_ref[...], preferred_element_type=F32).astype(o_ref.dtype)


def _norm_matmul(x, g, w, *, tm, tn):
    n, d = x.shape
    e = w.shape[1]
    return pl.pallas_call(
        _norm_matmul_kernel,
        grid=(e // tn, n // tm),
        in_specs=[pl.BlockSpec((tm, d), lambda j, i: (i, 0)),
                  pl.BlockSpec((1, d), lambda j, i: (0, 0)),
                  pl.BlockSpec((d, tn), lambda j, i: (0, j))],
        out_specs=pl.BlockSpec((tm, tn), lambda j, i: (i, j)),
        out_shape=jax.ShapeDtypeStruct((n, e), BF16),
        compiler_params=_cparams("arbitrary", "arbitrary"),
        name="norm_in_proj",
    )(x, g.reshape(1, d), w)


def _same_head_matrix(w):
    r = lax.broadcasted_iota(jnp.int32, (w, w), 0) // HEAD_DIM
    c = lax.broadcasted_iota(jnp.int32, (w, w), 1) // HEAD_DIM
    return (r == c).astype(BF16)


def _head_rms_scale(t, same_head):
    sq = t * t
    hi = sq.astype(BF16)
    lo = (sq - hi.astype(F32)).astype(BF16)
    ssq = (jnp.dot(hi, same_head, preferred_element_type=F32)
           + jnp.dot(lo, same_head, preferred_element_type=F32))
    return lax.rsqrt(ssq * (1.0 / HEAD_DIM) + EPS)


def _dil_attn_kernel(q_ref, k_ref, v_ref, qg_ref, kg_ref, bias_ref, o_ref, lse_ref,
                     qn_scr, kn_scr, v_scr, o_scr, *, seq, dilation):
    blk, d = DIL_BLOCK, dilation
    span = blk * d
    log2d = d.bit_length() - 1
    same_head = _same_head_matrix(DIL_OUT)
    qg = qg_ref[...] * (1.0 / math.sqrt(HEAD_DIM))
    kg = kg_ref[...]
    chunk = 256

    pairs = DIL_OUT // LANES
    pair_lanes = [slice(p * LANES, (p + 1) * LANES) for p in range(pairs)]
    for p in range(pairs):
        kn_scr[p, pl.ds(0, span), :] = jnp.zeros((span, LANES), F32)
        v_scr[p, pl.ds(0, span), :] = jnp.zeros((span, LANES), F32)

    def norm(i, c):
        rows = pl.ds(pl.multiple_of(i * chunk, chunk), chunk)
        prows = pl.ds(pl.multiple_of(span + i * chunk, blk), chunk)
        q = q_ref[0, rows, :].astype(F32)
        qn = q * _head_rms_scale(q, same_head) * qg
        k = k_ref[0, rows, :].astype(F32)
        kn = k * _head_rms_scale(k, same_head) * kg
        v = v_ref[0, rows, :].astype(F32)
        for p in range(pairs):
            qn_scr[p, rows, :] = qn[:, pair_lanes[p]]
            kn_scr[p, prows, :] = kn[:, pair_lanes[p]]
            v_scr[p, prows, :] = v[:, pair_lanes[p]]
        return c

    lax.fori_loop(0, seq // chunk, norm, 0)

    low_half = lax.broadcasted_iota(jnp.int32, (blk, LANES), 1) < HEAD_DIM

    def sub_rows(base, count):
        return pl.ds(pl.multiple_of(base, blk), count) if d == 1 else pl.ds(base, count, stride=d)

    def load(u):
        if d == 1:
            n, base = u, u * span
        else:
            n = u >> log2d
            base = (u & (d - 1)) + n * span
        q = [qn_scr[p, sub_rows(base, blk), :] for p in range(pairs)]
        kk = [kn_scr[p, sub_rows(base, 2 * blk), :].astype(BF16) for p in range(pairs)]
        vv = [v_scr[p, sub_rows(base, 2 * blk), :].astype(BF16) for p in range(pairs)]
        first = jnp.where(n == 0, 1, 0)
        return base, q, kk, vv, first

    def scores(unit):
        _, q, kk, _, first = unit
        out = []
        for h in range(DIL_HEADS_PER_GROUP):
            keep = low_half if h % 2 == 0 else jnp.logical_not(low_half)
            qh = jnp.where(keep, q[h // 2], 0.0).astype(BF16)
            s = lax.dot_general(qh, kk[h // 2], (((1,), (1,)), ((), ())), preferred_element_type=F32)
            out.append(s + bias_ref[h, first])
        return out

    def finish(unit, s_list):
        base, _, _, vv, _ = unit
        rows = sub_rows(base, blk)
        for p in range(pairs):
            oh, lh = [], []
            for hh in range(2):
                s = s_list[2 * p + hh]
                m = jnp.max(s, axis=-1, keepdims=True)
                e = jnp.exp(s - m)
                den = jnp.sum(e, axis=-1, keepdims=True)
                oh.append(jnp.dot(e.astype(BF16), vv[p], preferred_element_type=F32) * (1.0 / den))
                lh.append(m + jnp.log(den))
            o_scr[p, rows, :] = jnp.where(low_half, oh[0], oh[1])
            lse_ref[0, p, rows, :] = jnp.where(low_half, lh[0], lh[1])

    def body(it, c):
        units = [load(it * DIL_UNITS + i) for i in range(DIL_UNITS)]
        s = [None] * DIL_UNITS
        for step in range(DIL_UNITS + 1):
            if step < DIL_UNITS:
                s[step] = scores(units[step])
            if step >= 1:
                finish(units[step - 1], s[step - 1])
        return c

    lax.fori_loop(0, seq // blk // DIL_UNITS, body, 0)

    def emit(i, c):
        rows = pl.ds(pl.multiple_of(i * chunk, chunk), chunk)
        o_ref[0, rows, :] = jnp.concatenate([o_scr[p, rows, :] for p in range(pairs)],
                                            axis=-1).astype(o_ref.dtype)
        return c

    lax.fori_loop(0, seq // chunk, emit, 0)


def _dil_bias_table(group, window, dilation):
    reach = window // dilation
    assert reach <= DIL_BLOCK
    slopes = 2.0 ** (-ALIBI_MAX_BIAS * jnp.arange(1, DIL_HEADS + 1, dtype=F32) / DIL_HEADS)
    slopes = slopes[group * DIL_HEADS_PER_GROUP:(group + 1) * DIL_HEADS_PER_GROUP]
    qi = jnp.arange(DIL_BLOCK)[:, None] + DIL_BLOCK
    ki = jnp.arange(2 * DIL_BLOCK)[None, :]
    dist = qi - ki
    valid = (dist >= 0) & (dist <= reach)
    bias = -slopes[:, None, None] * (dist * dilation).astype(F32)[None]
    general = jnp.where(valid[None], bias, NEG_BIG)
    first = jnp.where((valid & (ki >= DIL_BLOCK))[None], bias, NEG_BIG)
    return jnp.stack([general, first], axis=1)


def _dil_attention(proj, q_gain, k_gain, group, window, dilation, bsz, seq):
    e = proj.shape[-1]
    assert (seq // DIL_BLOCK) % DIL_UNITS == 0
    assert seq % (DIL_BLOCK * dilation) == 0 and dilation & (dilation - 1) == 0
    view = proj.reshape(bsz, seq, e)
    w = DIL_OUT
    pairs = w // LANES
    pad = DIL_BLOCK * dilation

    def col(base):
        off = (base + group * w) // w
        return lambda b: (b, 0, off)

    bias = _dil_bias_table(group, window, dilation)
    gain = lambda g: jnp.tile(g, DIL_HEADS_PER_GROUP).reshape(1, w)
    o, lse = pl.pallas_call(
        functools.partial(_dil_attn_kernel, seq=seq, dilation=dilation),
        grid=(bsz,),
        in_specs=[pl.BlockSpec((1, seq, w), col(COL_DIL)),
                  pl.BlockSpec((1, seq, w), col(COL_DIL + DIL_WIDTH)),
                  pl.BlockSpec((1, seq, w), col(COL_DIL + 2 * DIL_WIDTH)),
                  pl.BlockSpec((1, w), lambda b: (0, 0)),
                  pl.BlockSpec((1, w), lambda b: (0, 0)),
                  pl.BlockSpec(bias.shape, lambda b: (0, 0, 0, 0))],
        out_specs=[pl.BlockSpec((1, seq, w), lambda b: (b, 0, 0)),
                   pl.BlockSpec((1, pairs, seq, LANES), lambda b: (b, 0, 0, 0))],
        out_shape=[jax.ShapeDtypeStruct((bsz, seq, w), BF16),
                   jax.ShapeDtypeStruct((bsz, pairs, seq, LANES), F32)],
        scratch_shapes=[pltpu.VMEM((pairs, seq, LANES), F32), pltpu.VMEM((pairs, pad + seq, LANES), F32),
                        pltpu.VMEM((pairs, pad + seq, LANES), F32), pltpu.VMEM((pairs, seq, LANES), F32)],
        compiler_params=_cparams("arbitrary"),
        name=f"dilated_attn_g{group}",
    )(view, view, view, gain(q_gain), gain(k_gain), bias)
    return o.reshape(bsz * seq, w), lse


def _sb_attn_kernel(q_ref, k_ref, v_ref, o_ref, vcat_scr, acc_scr, *, seq):
    tq, tk, nblk = SB_TQ, SB_TK, SB_TQ // SB_TK
    qi = pl.program_id(2)

    @pl.when(qi == 0)
    def _():
        chan = lax.broadcasted_iota(jnp.int32, (LANES, tk), 0)

        def build(kb, c):
            vt = v_ref[0, pl.ds(pl.multiple_of(kb * tk, tk), tk), :].astype(F32).T
            vcat_scr[kb] = jnp.concatenate(
                [jnp.where(chan < HEAD_DIM, vt, 0.0), jnp.where(chan >= HEAD_DIM, vt, 0.0)],
                axis=1).astype(BF16)
            return c

        lax.fori_loop(0, seq // tk, build, 0)

    lane = lax.broadcasted_iota(jnp.int32, (tq, LANES), 1)
    q = q_ref[0] * (1.0 / math.sqrt(HEAD_DIM))
    zero = jnp.zeros_like(q)
    qcat = jnp.concatenate([jnp.where(lane < HEAD_DIM, q, zero), jnp.where(lane >= HEAD_DIM, q, zero)],
                           axis=0)
    qcat_t = qcat.astype(F32).T.astype(BF16)
    neg_tri = jnp.where(lax.broadcasted_iota(jnp.int32, (tk, tk), 1)
                        >= lax.broadcasted_iota(jnp.int32, (tk, tk), 0), -1.0, 0.0).astype(BF16)
    acc_scr[...] = jnp.zeros_like(acc_scr)

    def scores(kb, rel):
        kk = k_ref[0, pl.ds(pl.multiple_of(kb * tk, tk), tk), :]
        zt = lax.dot_general(kk, qcat, (((1,), (1,)), ((), ())), preferred_element_type=F32)
        neg_abs = lax.bitcast_convert_type(
            lax.bitcast_convert_type(zt, jnp.uint32) | jnp.uint32(0x80000000), F32)
        sp = jnp.maximum(zt, 0.0) + jnp.log(1.0 + jnp.exp(neg_abs))
        before = None
        if rel is not None:
            kpos = lax.broadcasted_iota(jnp.int32, (tk, 2 * tq), 0) + rel * tk
            qpos = lax.broadcasted_iota(jnp.int32, (tk, 2 * tq), 1) & (tq - 1)
            before = kpos < qpos
            sp = jnp.where(before, sp, 0.0)
        return kk, zt[0:1, :], sp.astype(BF16), before

    def weights(state, run):
        kk, zt0, sp, before = state
        arg = jnp.dot(jnp.concatenate([neg_tri, kk], axis=1),
                      jnp.concatenate([sp, qcat_t], axis=0), preferred_element_type=F32)
        a = jnp.exp(arg - run)
        if before is not None:
            a = jnp.where(before, a, 0.0)
        a = a.astype(BF16)
        acat = jnp.concatenate([a[:, :tq], a[:, tq:]], axis=0)
        return acat, run + (zt0 - arg[0:1, :])

    def values(kb, acat):
        acc_scr[...] += jnp.dot(vcat_scr[kb], acat, preferred_element_type=F32)

    def run_blocks(blocks, run):
        n = len(blocks)
        st, ac = [None] * n, [None] * n
        for step in range(n + 2):
            if step < n:
                st[step] = scores(*blocks[step])
            if 0 <= step - 1 < n:
                ac[step - 1], run = weights(st[step - 1], run)
            if 0 <= step - 2 < n:
                values(blocks[step - 2][0], ac[step - 2])
        return run

    run = jnp.zeros((1, 2 * tq), F32)
    run = run_blocks([(qi * nblk + rel, rel) for rel in reversed(range(nblk))], run)

    def chunk(it, run):
        base = (qi - 1 - it) * nblk
        return run_blocks([(base + j, None) for j in reversed(range(nblk))], run)

    lax.fori_loop(0, qi, chunk, run)
    o_ref[0] = acc_scr[...].T.astype(o_ref.dtype)


def _sb_attention(proj, bsz, seq):
    e = proj.shape[-1]
    view = proj.reshape(bsz, seq, e)
    pairs = SB_WIDTH // LANES
    qo, ko, vo = COL_SB // LANES, (COL_SB + SB_WIDTH) // LANES, (COL_SB + 2 * SB_WIDTH) // LANES
    out = pl.pallas_call(
        functools.partial(_sb_attn_kernel, seq=seq),
        grid=(bsz, pairs, seq // SB_TQ),
        in_specs=[pl.BlockSpec((1, SB_TQ, LANES), lambda b, p, i: (b, i, qo + p)),
                  pl.BlockSpec((1, seq, LANES), lambda b, p, i: (b, 0, ko + p)),
                  pl.BlockSpec((1, seq, LANES), lambda b, p, i: (b, 0, vo + p))],
        out_specs=pl.BlockSpec((1, SB_TQ, LANES), lambda b, p, i: (b, i, p)),
        out_shape=jax.ShapeDtypeStruct((bsz, seq, SB_WIDTH), BF16),
        scratch_shapes=[pltpu.VMEM((seq // SB_TK, LANES, 2 * SB_TK), BF16),
                        pltpu.VMEM((LANES, SB_TQ), F32)],
        compiler_params=_cparams("arbitrary", "arbitrary", "arbitrary"),
        name="stick_breaking_attn",
    )(view, view, view)
    return out.reshape(bsz * seq, SB_WIDTH)


def _conv_kernel(val_ref, gate_ref, w_ref, b_ref, g_ref, beta_ref, o_ref, u_scr, *, seq):
    tr, pad, half = CONV_ROWS, CONV_PAD, CONV_CH // 2
    u_scr[pl.ds(0, pad), :] = jnp.zeros((pad, CONV_CH), F32)

    def glu(i, c):
        rows = pl.ds(pl.multiple_of(i * 256, 256), 256)
        val = val_ref[0, rows, :].astype(F32)
        gate = gate_ref[0, rows, :].astype(F32)
        u_scr[pl.ds(pl.multiple_of(pad + i * 256, SUBLANES), 256), :] = val * _sigmoid(gate)
        return c

    lax.fori_loop(0, seq // 256, glu, 0)

    def tile(i, c):
        t0 = pl.multiple_of(i * tr, tr)
        parts = []
        for ch in range(2):
            cs = slice(ch * half, (ch + 1) * half)
            win = u_scr[pl.ds(t0, tr + pad), cs]
            acc = jnp.zeros((tr, half), F32)
            first = pad - (CONV_WIDTH - 1)
            for r in range(SUBLANES):
                offs = [o for o in range(first, first + CONV_WIDTH) if o % SUBLANES == r]
                shifted = win[r:offs[-1] + tr, :]
                for o in offs:
                    w = o - first
                    acc = acc + shifted[o - r:o - r + tr, :] * w_ref[w:w + 1, cs]
            parts.append(acc)
        y = jnp.concatenate(parts, axis=-1) + b_ref[...]
        mu = jnp.mean(y, axis=-1, keepdims=True)
        yc = y - mu
        var = jnp.mean(yc * yc, axis=-1, keepdims=True)
        yn = yc * lax.rsqrt(var + EPS) * g_ref[...] + beta_ref[...]
        o_ref[0, pl.ds(t0, tr), :] = (yn * _sigmoid(yn)).astype(o_ref.dtype)
        return c

    lax.fori_loop(0, seq // tr, tile, 0)


def _conformer_conv(proj, conv_w, conv_b, norm_g, norm_b, bsz, seq):
    e = proj.shape[-1]
    view = proj.reshape(bsz, seq, e)
    c = CONV_CH
    voff, goff = COL_GLU // c, (COL_GLU + c) // c
    const = lambda b: (0, 0)
    out = pl.pallas_call(
        functools.partial(_conv_kernel, seq=seq),
        grid=(bsz,),
        in_specs=[pl.BlockSpec((1, seq, c), lambda b: (b, 0, voff)),
                  pl.BlockSpec((1, seq, c), lambda b: (b, 0, goff)),
                  pl.BlockSpec((CONV_WIDTH, c), const),
                  pl.BlockSpec((1, c), const), pl.BlockSpec((1, c), const), pl.BlockSpec((1, c), const)],
        out_specs=pl.BlockSpec((1, seq, c), lambda b: (b, 0, 0)),
        out_shape=jax.ShapeDtypeStruct((bsz, seq, c), BF16),
        scratch_shapes=[pltpu.VMEM((seq + CONV_PAD, c), F32)],
        compiler_params=_cparams("arbitrary"),
        name="conformer_conv",
    )(view, view, conv_w, conv_b.reshape(1, c), norm_g.reshape(1, c), norm_b.reshape(1, c))
    return out.reshape(bsz * seq, c)


def _merge_kernel(o1_ref, o2_ref, o3_ref, l1_ref, l2_ref, l3_ref, ob_ref, oc_ref,
                  ga_ref, gb_ref, gc_ref, x_ref, wa_ref, wb_ref, wc_ref, wo_ref, out_ref):
    by_lanes = lambda ref: jnp.concatenate([ref[p] for p in range(ref.shape[0])], axis=-1)
    l1, l2, l3 = by_lanes(l1_ref), by_lanes(l2_ref), by_lanes(l3_ref)
    m = jnp.maximum(jnp.maximum(l1, l2), l3)
    e1, e2, e3 = jnp.exp(l1 - m), jnp.exp(l2 - m), jnp.exp(l3 - m)
    o_a = (e1 * o1_ref[...].astype(F32) + e2 * o2_ref[...].astype(F32)
           + e3 * o3_ref[...].astype(F32)) / (e1 + e2 + e3)
    ya = jnp.dot(o_a.astype(BF16), wa_ref[...], preferred_element_type=F32)
    yb = jnp.dot(ob_ref[...], wb_ref[...], preferred_element_type=F32)
    yc = jnp.dot(oc_ref[...], wc_ref[...], preferred_element_type=F32)
    merged = (_sigmoid(ga_ref[...].astype(F32)) * ya + _sigmoid(gb_ref[...].astype(F32)) * yb
              + _sigmoid(gc_ref[...].astype(F32)) * yc)
    out_ref[...] = x_ref[...] + jnp.dot(merged.astype(BF16), wo_ref[...], preferred_element_type=F32)


def _merge(o_groups, lse_groups, o_b, o_c, proj, x, wa, wb, wc, wo, *, tm):
    n, d = x.shape
    row = lambda w, j=0: pl.BlockSpec((tm, w), lambda i: (i, j))
    full = lambda a: pl.BlockSpec(a.shape, lambda i: (0, 0), pipeline_mode=pl.Buffered(1))
    g0 = COL_GATES // d
    _, pairs, seq, _ = lse_groups[0].shape
    per_batch = seq // tm
    lse = pl.BlockSpec((None, pairs, tm, LANES), lambda i: (i // per_batch, 0, i % per_batch, 0))
    return pl.pallas_call(
        _merge_kernel,
        grid=(n // tm,),
        in_specs=[row(DIL_OUT), row(DIL_OUT), row(DIL_OUT), lse, lse, lse,
                  row(SB_WIDTH), row(CONV_CH), row(d, g0), row(d, g0 + 1), row(d, g0 + 2), row(d),
                  full(wa), full(wb), full(wc), full(wo)],
        out_specs=row(d),
        out_shape=jax.ShapeDtypeStruct((n, d), F32),
        compiler_params=_cparams("arbitrary"),
        name="branch_merge_out_proj",
    )(*o_groups, *lse_groups, o_b, o_c, proj, proj, proj, x, wa, wb, wc, wo)


def _swiglu_tile(h, wg_ref, wu_ref, wd_ref):
    y = None
    for c in range(D_FF // FF_CHUNK):
        cs = slice(c * FF_CHUNK, (c + 1) * FF_CHUNK)
        a = jnp.dot(h, wg_ref[:, cs], preferred_element_type=F32)
        u = jnp.dot(h, wu_ref[:, cs], preferred_element_type=F32)
        act = (a * _sigmoid(a) * u).astype(BF16)
        part = jnp.dot(act, wd_ref[cs, :], preferred_element_type=F32)
        y = part if y is None else y + part
    return y


def _dense_ffn_kernel(x_ref, g_ref, wg_ref, wu_ref, wd_ref, o_ref):
    x = x_ref[...]
    ms = jnp.mean(x * x, axis=-1, keepdims=True)
    h = ((x * lax.rsqrt(ms + EPS)) * g_ref[...]).astype(BF16)
    o_ref[...] = x + _swiglu_tile(h, wg_ref, wu_ref, wd_ref)


def _dense_ffn(x, g, wg, wu, wd, *, tm):
    n, d = x.shape
    full = lambda a: pl.BlockSpec(a.shape, lambda i: (0, 0), pipeline_mode=pl.Buffered(1))
    return pl.pallas_call(
        _dense_ffn_kernel,
        grid=(n // tm,),
        in_specs=[pl.BlockSpec((tm, d), lambda i: (i, 0)), pl.BlockSpec((1, d), lambda i: (0, 0)),
                  full(wg), full(wu), full(wd)],
        out_specs=pl.BlockSpec((tm, d), lambda i: (i, 0)),
        out_shape=jax.ShapeDtypeStruct((n, d), F32),
        compiler_params=_cparams("arbitrary"),
        name="dense_swiglu",
    )(x, g.reshape(1, d), wg, wu, wd)


def _router_kernel(x_ref, g_ref, wr_ref, br_ref, h_ref, route_ref):
    x = x_ref[...]
    ms = jnp.mean(x * x, axis=-1, keepdims=True)
    h = (x * lax.rsqrt(ms + EPS)) * g_ref[...]
    _store_token_tiles(h_ref, h)
    logits = jnp.dot(h, wr_ref[...], preferred_element_type=F32,
                     precision=lax.Precision.HIGHEST) + br_ref[...]
    lane = lax.broadcasted_iota(jnp.int32, logits.shape, 1)
    m1 = jnp.max(logits, axis=-1, keepdims=True)
    i1 = jnp.min(jnp.where(logits == m1, lane, LANES), axis=-1, keepdims=True)
    rest = jnp.where(lane == i1, -jnp.inf, logits)
    m2 = jnp.max(rest, axis=-1, keepdims=True)
    i2 = jnp.min(jnp.where(rest == m2, lane, LANES), axis=-1, keepdims=True)
    e2 = jnp.exp(m2 - m1)
    g1 = 1.0 / (1.0 + e2)
    g2 = e2 / (1.0 + e2)
    route = jnp.where(lane == 0, i1.astype(F32),
                      jnp.where(lane == 1, i2.astype(F32),
                                jnp.where(lane == 2, g1, jnp.where(lane == 3, g2, 0.0))))
    route_ref[...] = route


def _router(x, g, w_router, b_router, *, tm):
    n, d = x.shape
    wr = jnp.zeros((d, LANES), F32).at[:, :N_EXPERTS].set(w_router)
    br = jnp.full((1, LANES), NEG_BIG, F32).at[0, :N_EXPERTS].set(b_router)
    return pl.pallas_call(
        _router_kernel,
        grid=(n // tm,),
        in_specs=[pl.BlockSpec((tm, d), lambda i: (i, 0)), pl.BlockSpec((1, d), lambda i: (0, 0)),
                  pl.BlockSpec((d, LANES), lambda i: (0, 0)), pl.BlockSpec((1, LANES), lambda i: (0, 0))],
        out_specs=[pl.BlockSpec((tm * ROW_TILE, LANES), lambda i: (i, 0)),
                   pl.BlockSpec((tm, LANES), lambda i: (i, 0))],
        out_shape=[jax.ShapeDtypeStruct((n * ROW_TILE, LANES), F32), jax.ShapeDtypeStruct((n, LANES), F32)],
        compiler_params=_cparams("arbitrary"),
        name="router_top2",
    )(x, g.reshape(1, d), wr, br)


ROW_TILE = D_MODEL // LANES


def _store_token_tiles(ref, x):
    rows = x.shape[0]
    for c in range(ROW_TILE):
        ref[pl.ds(c, rows, stride=ROW_TILE), :] = x[:, c * LANES:(c + 1) * LANES]


def _load_token_tiles(ref, rows):
    return jnp.concatenate([ref[pl.ds(c, rows, stride=ROW_TILE), :] for c in range(ROW_TILE)], axis=-1)


def _row_gather_start(idx_ref, blk, src_hbm, dst, sem, rows):
    def body(g, c):
        for j in range(SUBLANES):
            r = g * SUBLANES + j
            tok = idx_ref[blk, r]
            pltpu.make_async_copy(src_hbm.at[pl.ds(pl.multiple_of(tok * ROW_TILE, ROW_TILE), ROW_TILE), :],
                                  dst.at[pl.ds(pl.multiple_of(r * ROW_TILE, ROW_TILE), ROW_TILE), :],
                                  sem).start()
        return c

    lax.fori_loop(0, rows // SUBLANES, body, 0)


def _row_gather_wait(src_hbm, dst, sem, rows):
    pltpu.make_async_copy(src_hbm.at[pl.ds(0, rows * ROW_TILE), :], dst, sem).wait()


def _expert_kernel(be_ref, used_ref, tok_ref, h_hbm, wg_ref, wu_ref, wd_ref, y_ref, xbuf, sem):
    i = pl.program_id(0)
    used = used_ref[0]
    slot = i % 2

    @pl.when(i == 0)
    def _():
        _row_gather_start(tok_ref, 0, h_hbm, xbuf.at[0], sem.at[0], MOE_TM)

    @pl.when(i + 1 < used)
    def _():
        _row_gather_start(tok_ref, i + 1, h_hbm, xbuf.at[1 - slot], sem.at[1 - slot], MOE_TM)

    @pl.when(i < used)
    def _():
        _row_gather_wait(h_hbm, xbuf.at[slot], sem.at[slot], MOE_TM)
        x = _load_token_tiles(xbuf.at[slot], MOE_TM).astype(BF16)
        _store_token_tiles(y_ref, _swiglu_tile(x, wg_ref, wu_ref, wd_ref))

    @pl.when(i >= used)
    def _():
        y_ref[...] = jnp.zeros_like(y_ref)


def _experts(h, block_expert, used, row_tok, wg, wu, wd):
    d = D_MODEL
    n_blocks = row_tok.shape[0]
    f = wg.shape[-1]
    one = pl.Buffered(1)
    grid_spec = pltpu.PrefetchScalarGridSpec(
        num_scalar_prefetch=3,
        grid=(n_blocks,),
        in_specs=[pl.BlockSpec(memory_space=pl.ANY),
                  pl.BlockSpec((None, d, f), lambda i, be, us, tk: (be[i], 0, 0), pipeline_mode=one),
                  pl.BlockSpec((None, d, f), lambda i, be, us, tk: (be[i], 0, 0), pipeline_mode=one),
                  pl.BlockSpec((None, f, d), lambda i, be, us, tk: (be[i], 0, 0), pipeline_mode=one)],
        out_specs=pl.BlockSpec((MOE_TM * ROW_TILE, LANES), lambda i, be, us, tk: (i, 0)),
        scratch_shapes=[pltpu.VMEM((2, MOE_TM * ROW_TILE, LANES), F32), pltpu.SemaphoreType.DMA((2,))],
    )
    return pl.pallas_call(
        _expert_kernel,
        grid_spec=grid_spec,
        out_shape=jax.ShapeDtypeStruct((n_blocks * MOE_TM * ROW_TILE, LANES), F32),
        compiler_params=_cparams("arbitrary"),
        name="expert_swiglu",
    )(block_expert, used, row_tok, h, wg, wu, wd)


def _combine_kernel(p0_ref, p1_ref, ys_hbm, x_ref, route_ref, o_ref, buf, sem):
    i = pl.program_id(0)
    nsteps = pl.num_programs(0)
    slot = i % 2
    tt = COMBINE_TT

    def start(step, s):
        _row_gather_start(p0_ref, step, ys_hbm, buf.at[s, 0], sem.at[s, 0], tt)
        _row_gather_start(p1_ref, step, ys_hbm, buf.at[s, 1], sem.at[s, 1], tt)

    @pl.when(i == 0)
    def _():
        start(0, 0)

    @pl.when(i + 1 < nsteps)
    def _():
        start(i + 1, 1 - slot)

    _row_gather_wait(ys_hbm, buf.at[slot, 0], sem.at[slot, 0], tt)
    _row_gather_wait(ys_hbm, buf.at[slot, 1], sem.at[slot, 1], tt)
    route = route_ref[...]
    y0 = _load_token_tiles(buf.at[slot, 0], tt)
    y1 = _load_token_tiles(buf.at[slot, 1], tt)
    o_ref[...] = x_ref[...] + route[:, 2:3] * y0 + route[:, 3:4] * y1


def _combine(x, ys, route, pos0, pos1):
    n, d = x.shape
    tt = COMBINE_TT
    grid_spec = pltpu.PrefetchScalarGridSpec(
        num_scalar_prefetch=2,
        grid=(n // tt,),
        in_specs=[pl.BlockSpec(memory_space=pl.ANY),
                  pl.BlockSpec((tt, d), lambda i, a, b: (i, 0)),
                  pl.BlockSpec((tt, LANES), lambda i, a, b: (i, 0))],
        out_specs=pl.BlockSpec((tt, d), lambda i, a, b: (i, 0)),
        scratch_shapes=[pltpu.VMEM((2, 2, tt * ROW_TILE, LANES), F32), pltpu.SemaphoreType.DMA((2, 2))],
    )
    return pl.pallas_call(
        _combine_kernel,
        grid_spec=grid_spec,
        out_shape=jax.ShapeDtypeStruct((n, d), F32),
        compiler_params=_cparams("arbitrary"),
        name="expert_combine",
    )(pos0.reshape(n // tt, tt), pos1.reshape(n // tt, tt), ys, x, route)


def _routed_ffn(x, g, w_router, b_router, wg, wu, wd, *, tm):
    n, d = x.shape
    h, route = _router(x, g, w_router, b_router, tm=tm)
    e_flat = route[:, :TOP_K].astype(jnp.int32).reshape(-1)
    onehot = (e_flat[:, None] == jnp.arange(N_EXPERTS)[None, :]).astype(jnp.int32)
    csum = jnp.cumsum(onehot, axis=0)
    counts = csum[-1]
    rank = jnp.sum((csum - onehot) * onehot, axis=1)
    padded = (counts + MOE_TM - 1) // MOE_TM * MOE_TM
    pend = jnp.cumsum(padded)
    pstart = pend - padded
    dest = (pstart[e_flat] + rank).astype(jnp.int32)
    n_blocks = (n * TOP_K) // MOE_TM + N_EXPERTS
    tok = (jnp.arange(n * TOP_K, dtype=jnp.int32) // TOP_K)
    row_tok = jnp.zeros((n_blocks * MOE_TM,), jnp.int32).at[dest].set(tok).reshape(n_blocks, MOE_TM)
    block_expert = jnp.clip(jnp.searchsorted(pend, jnp.arange(n_blocks) * MOE_TM, side='right'),
                            0, N_EXPERTS - 1).astype(jnp.int32)
    used = (pend[-1:] // MOE_TM).astype(jnp.int32)
    ys = _experts(h, block_expert, used, row_tok, wg, wu, wd)
    pos = dest.reshape(n, TOP_K)
    return _combine(x, ys, route, pos[:, 0], pos[:, 1])


def kernel(x, attn_norm_g, w_in, q_norm_g, k_norm_g, conv_w, conv_b, conv_norm_g, conv_norm_b,
           w_branch_a, w_branch_b, w_branch_c, w_out, ffn_norm_g, w_ffn_gate, w_ffn_up,
           w_ffn_down, w_router, b_router, w_exp_gate, w_exp_up, w_exp_down):
    bsz, seq, d = x.shape
    depth = attn_norm_g.shape[0]
    n = bsz * seq
    tm = 512
    xf = x.reshape(n, d)
    s_dil, s_sb, s_glu = 3 * DIL_WIDTH, 3 * DIL_WIDTH + 3 * SB_WIDTH, 3 * DIL_WIDTH + 3 * SB_WIDTH + 2 * CONV_CH
    for layer in range(depth):
        wl = w_in[layer]
        w_perm = jnp.concatenate([wl[:, s_glu:], wl[:, s_sb:s_glu], wl[:, s_dil:s_sb], wl[:, :s_dil]],
                                 axis=1).astype(BF16)
        proj = _norm_matmul(xf, attn_norm_g[layer], w_perm, tm=tm, tn=IN_COLS // 2)
        o_groups, lse_groups = [], []
        for gi, (window, dilation) in enumerate(DIL_GROUPS):
            o_g, lse_g = _dil_attention(proj, q_norm_g[layer], k_norm_g[layer], gi, window, dilation,
                                        bsz, seq)
            o_groups.append(o_g)
            lse_groups.append(lse_g)
        o_b = _sb_attention(proj, bsz, seq)
        o_c = _conformer_conv(proj, conv_w[layer], conv_b[layer], conv_norm_g[layer],
                              conv_norm_b[layer], bsz, seq)
        xf = _merge(o_groups, lse_groups, o_b, o_c, proj, xf,
                    w_branch_a[layer].astype(BF16), w_branch_b[layer].astype(BF16),
                    w_branch_c[layer].astype(BF16), w_out[layer].astype(BF16), tm=tm)
        i = layer // 2
        if layer % 2 == 0:
            xf = _dense_ffn(xf, ffn_norm_g[layer], w_ffn_gate[i].astype(BF16), w_ffn_up[i].astype(BF16),
                            w_ffn_down[i].astype(BF16), tm=tm)
        else:
            xf = _routed_ffn(xf, ffn_norm_g[layer], w_router[i], b_router[i],
                             w_exp_gate[i].astype(BF16), w_exp_up[i].astype(BF16),
                             w_exp_down[i].astype(BF16), tm=tm)
    return xf.reshape(bsz, seq, d)
```

```python
import functools
import math

import jax
import jax.numpy as jnp
from jax import lax
from jax.experimental import pallas as pl
from jax.experimental.pallas import tpu as pltpu

F32 = jnp.float32
BF16 = jnp.bfloat16

D_MODEL = 1024
HEAD_DIM = 64
DIL_GROUPS = ((128, 1), (512, 4), (2048, 16))
DIL_HEADS_PER_GROUP = 4
DIL_HEADS = len(DIL_GROUPS) * DIL_HEADS_PER_GROUP
DIL_WIDTH = DIL_HEADS * HEAD_DIM
DIL_OUT = DIL_HEADS_PER_GROUP * HEAD_DIM
DIL_BLOCK = 128
SB_HEADS = 8
SB_WIDTH = SB_HEADS * HEAD_DIM
CONV_CH = D_MODEL // 2
CONV_WIDTH = 31
N_BRANCH = 3
IN_COLS = 3 * DIL_WIDTH + 3 * SB_WIDTH + 2 * CONV_CH + N_BRANCH * D_MODEL
D_FF = 2816
N_EXPERTS = 8
TOP_K = 2
EPS = 1e-6
ALIBI_MAX_BIAS = 8.0
NEG_BIG = -1e30

COL_GATES = 0
COL_GLU = COL_GATES + N_BRANCH * D_MODEL
COL_SB = COL_GLU + 2 * CONV_CH
COL_DIL = COL_SB + 3 * SB_WIDTH

LANES = 128
SUBLANES = 8
VMEM_LIMIT = 56 * 1024 * 1024

DIL_UNITS = 4
SB_TQ = 512
SB_TK = 128
SB_DEAD_RUN = 128.0
CONV_ROWS = 64
CONV_PAD = 32
FF_CHUNK = 1408
MOE_TM = 512
COMBINE_TT = 256


def _cparams(*sem):
    return pltpu.CompilerParams(dimension_semantics=sem, vmem_limit_bytes=VMEM_LIMIT)


def _sigmoid(x):
    return 1.0 / (1.0 + jnp.exp(-x))


def _norm_matmul_kernel(x_ref, g_ref, w_ref, o_ref):
    x = x_ref[...]
    ms = jnp.mean(x * x, axis=-1, keepdims=True)
    h = (x * lax.rsqrt(ms + EPS)) * g_ref[...]
    o_ref[...] = jnp.dot(h.astype(BF16), w_ref[...], preferred_element_type=F32).astype(o_ref.dtype)


def _norm_matmul(x, g, w, *, tm, tn):
    n, d = x.shape
    e = w.shape[1]
    return pl.pallas_call(
        _norm_matmul_kernel,
        grid=(e // tn, n // tm),
        in_specs=[pl.BlockSpec((tm, d), lambda j, i: (i, 0)),
                  pl.BlockSpec((1, d), lambda j, i: (0, 0)),
                  pl.BlockSpec((d, tn), lambda j, i: (0, j))],
        out_specs=pl.BlockSpec((tm, tn), lambda j, i: (i, j)),
        out_shape=jax.ShapeDtypeStruct((n, e), BF16),
        compiler_params=_cparams("arbitrary", "arbitrary"),
        name="norm_in_proj",
    )(x, g.reshape(1, d), w)


def _same_head_matrix(w):
    r = lax.broadcasted_iota(jnp.int32, (w, w), 0) // HEAD_DIM
    c = lax.broadcasted_iota(jnp.int32, (w, w), 1) // HEAD_DIM
    return (r == c).astype(BF16)


def _head_rms_scale(t, same_head):
    sq = t * t
    hi = sq.astype(BF16)
    lo = (sq - hi.astype(F32)).astype(BF16)
    ssq = (jnp.dot(hi, same_head, preferred_element_type=F32)
           + jnp.dot(lo, same_head, preferred_element_type=F32))
    return lax.rsqrt(ssq * (1.0 / HEAD_DIM) + EPS)


def _dil_attn_kernel(q_ref, k_ref, v_ref, qg_ref, kg_ref, bias_ref, o_ref, lse_ref,
                     qn_scr, kn_scr, v_scr, o_scr, *, seq, dilation):
    blk, d = DIL_BLOCK, dilation
    span = blk * d
    log2d = d.bit_length() - 1
    same_head = _same_head_matrix(DIL_OUT)
    qg = qg_ref[...] * (1.0 / math.sqrt(HEAD_DIM))
    kg = kg_ref[...]
    chunk = 256

    pairs = DIL_OUT // LANES
    pair_lanes = [slice(p * LANES, (p + 1) * LANES) for p in range(pairs)]
    for p in range(pairs):
        kn_scr[p, pl.ds(0, span), :] = jnp.zeros((span, LANES), F32)
        v_scr[p, pl.ds(0, span), :] = jnp.zeros((span, LANES), F32)

    def norm(i, c):
        rows = pl.ds(pl.multiple_of(i * chunk, chunk), chunk)
        prows = pl.ds(pl.multiple_of(span + i * chunk, blk), chunk)
        q = q_ref[0, rows, :].astype(F32)
        qn = q * _head_rms_scale(q, same_head) * qg
        k = k_ref[0, rows, :].astype(F32)
        kn = k * _head_rms_scale(k, same_head) * kg
        v = v_ref[0, rows, :].astype(F32)
        for p in range(pairs):
            qn_scr[p, rows, :] = qn[:, pair_lanes[p]]
            kn_scr[p, prows, :] = kn[:, pair_lanes[p]]
            v_scr[p, prows, :] = v[:, pair_lanes[p]]
        return c

    lax.fori_loop(0, seq // chunk, norm, 0)

    low_half = lax.broadcasted_iota(jnp.int32, (blk, LANES), 1) < HEAD_DIM

    def sub_rows(base, count):
        return pl.ds(pl.multiple_of(base, blk), count) if d == 1 else pl.ds(base, count, stride=d)

    def load(u):
        if d == 1:
            n, base = u, u * span
        else:
            n = u >> log2d
            base = (u & (d - 1)) + n * span
        q = [qn_scr[p, sub_rows(base, blk), :] for p in range(pairs)]
        kk = [kn_scr[p, sub_rows(base, 2 * blk), :].astype(BF16) for p in range(pairs)]
        vv = [v_scr[p, sub_rows(base, 2 * blk), :].astype(BF16) for p in range(pairs)]
        first = jnp.where(n == 0, 1, 0)
        return base, q, kk, vv, first

    def scores(unit):
        _, q, kk, _, first = unit
        out = []
        for h in range(DIL_HEADS_PER_GROUP):
            keep = low_half if h % 2 == 0 else jnp.logical_not(low_half)
            qh = jnp.where(keep, q[h // 2], 0.0).astype(BF16)
            s = lax.dot_general(qh, kk[h // 2], (((1,), (1,)), ((), ())), preferred_element_type=F32)
            out.append(s + bias_ref[h, first])
        return out

    def finish(unit, s_list):
        base, _, _, vv, _ = unit
        rows = sub_rows(base, blk)
        for p in range(pairs):
            oh, lh = [], []
            for hh in range(2):
                s = s_list[2 * p + hh]
                m = jnp.max(s, axis=-1, keepdims=True)
                e = jnp.exp(s - m)
                den = jnp.sum(e, axis=-1, keepdims=True)
                oh.append(jnp.dot(e.astype(BF16), vv[p], preferred_element_type=F32) * (1.0 / den))
                lh.append(m + jnp.log(den))
            o_scr[p, rows, :] = jnp.where(low_half, oh[0], oh[1])
            lse_ref[0, p, rows, :] = jnp.where(low_half, lh[0], lh[1])

    def body(it, c):
        units = [load(it * DIL_UNITS + i) for i in range(DIL_UNITS)]
        s = [None] * DIL_UNITS
        for step in range(DIL_UNITS + 1):
            if step < DIL_UNITS:
                s[step] = scores(units[step])
            if step >= 1:
                finish(units[step - 1], s[step - 1])
        return c

    lax.fori_loop(0, seq // blk // DIL_UNITS, body, 0)

    def emit(i, c):
        rows = pl.ds(pl.multiple_of(i * chunk, chunk), chunk)
        o_ref[0, rows, :] = jnp.concatenate([o_scr[p, rows, :] for p in range(pairs)],
                                            axis=-1).astype(o_ref.dtype)
        return c

    lax.fori_loop(0, seq // chunk, emit, 0)


def _dil_bias_table(group, window, dilation):
    reach = window // dilation
    assert reach <= DIL_BLOCK
    slopes = 2.0 ** (-ALIBI_MAX_BIAS * jnp.arange(1, DIL_HEADS + 1, dtype=F32) / DIL_HEADS)
    slopes = slopes[group * DIL_HEADS_PER_GROUP:(group + 1) * DIL_HEADS_PER_GROUP]
    qi = jnp.arange(DIL_BLOCK)[:, None] + DIL_BLOCK
    ki = jnp.arange(2 * DIL_BLOCK)[None, :]
    dist = qi - ki
    valid = (dist >= 0) & (dist <= reach)
    bias = -slopes[:, None, None] * (dist * dilation).astype(F32)[None]
    general = jnp.where(valid[None], bias, NEG_BIG)
    first = jnp.where((valid & (ki >= DIL_BLOCK))[None], bias, NEG_BIG)
    return jnp.stack([general, first], axis=1)


def _dil_attention(proj, q_gain, k_gain, group, window, dilation, bsz, seq):
    e = proj.shape[-1]
    assert (seq // DIL_BLOCK) % DIL_UNITS == 0
    assert seq % (DIL_BLOCK * dilation) == 0 and dilation & (dilation - 1) == 0
    view = proj.reshape(bsz, seq, e)
    w = DIL_OUT
    pairs = w // LANES
    pad = DIL_BLOCK * dilation

    def col(base):
        off = (base + group * w) // w
        return lambda b: (b, 0, off)

    bias = _dil_bias_table(group, window, dilation)
    gain = lambda g: jnp.tile(g, DIL_HEADS_PER_GROUP).reshape(1, w)
    o, lse = pl.pallas_call(
        functools.partial(_dil_attn_kernel, seq=seq, dilation=dilation),
        grid=(bsz,),
        in_specs=[pl.BlockSpec((1, seq, w), col(COL_DIL)),
                  pl.BlockSpec((1, seq, w), col(COL_DIL + DIL_WIDTH)),
                  pl.BlockSpec((1, seq, w), col(COL_DIL + 2 * DIL_WIDTH)),
                  pl.BlockSpec((1, w), lambda b: (0, 0)),
                  pl.BlockSpec((1, w), lambda b: (0, 0)),
                  pl.BlockSpec(bias.shape, lambda b: (0, 0, 0, 0))],
        out_specs=[pl.BlockSpec((1, seq, w), lambda b: (b, 0, 0)),
                   pl.BlockSpec((1, pairs, seq, LANES), lambda b: (b, 0, 0, 0))],
        out_shape=[jax.ShapeDtypeStruct((bsz, seq, w), BF16),
                   jax.ShapeDtypeStruct((bsz, pairs, seq, LANES), F32)],
        scratch_shapes=[pltpu.VMEM((pairs, seq, LANES), F32), pltpu.VMEM((pairs, pad + seq, LANES), F32),
                        pltpu.VMEM((pairs, pad + seq, LANES), F32), pltpu.VMEM((pairs, seq, LANES), F32)],
        compiler_params=_cparams("arbitrary"),
        name=f"dilated_attn_g{group}",
    )(view, view, view, gain(q_gain), gain(k_gain), bias)
    return o.reshape(bsz * seq, w), lse


def _sb_attn_kernel(q_ref, k_ref, v_ref, o_ref, vcat_scr, acc_scr, *, seq):
    tq, tk, nblk = SB_TQ, SB_TK, SB_TQ // SB_TK
    qi = pl.program_id(2)

    @pl.when(qi == 0)
    def _():
        chan = lax.broadcasted_iota(jnp.int32, (LANES, tk), 0)

        def build(kb, c):
            vt = v_ref[0, pl.ds(pl.multiple_of(kb * tk, tk), tk), :].astype(F32).T
            vcat_scr[kb] = jnp.concatenate(
                [jnp.where(chan < HEAD_DIM, vt, 0.0), jnp.where(chan >= HEAD_DIM, vt, 0.0)],
                axis=1).astype(BF16)
            return c

        lax.fori_loop(0, seq // tk, build, 0)

    lane = lax.broadcasted_iota(jnp.int32, (tq, LANES), 1)
    q = q_ref[0] * (1.0 / math.sqrt(HEAD_DIM))
    zero = jnp.zeros_like(q)
    qcat = jnp.concatenate([jnp.where(lane < HEAD_DIM, q, zero), jnp.where(lane >= HEAD_DIM, q, zero)],
                           axis=0)
    qcat_t = qcat.astype(F32).T.astype(BF16)
    neg_tri = jnp.where(lax.broadcasted_iota(jnp.int32, (tk, tk), 1)
                        >= lax.broadcasted_iota(jnp.int32, (tk, tk), 0), -1.0, 0.0).astype(BF16)
    acc_scr[...] = jnp.zeros_like(acc_scr)

    def scores(kb, rel):
        kk = k_ref[0, pl.ds(pl.multiple_of(kb * tk, tk), tk), :]
        zt = lax.dot_general(kk, qcat, (((1,), (1,)), ((), ())), preferred_element_type=F32)
        neg_abs = lax.bitcast_convert_type(
            lax.bitcast_convert_type(zt, jnp.uint32) | jnp.uint32(0x80000000), F32)
        sp = jnp.maximum(zt, 0.0) + jnp.log(1.0 + jnp.exp(neg_abs))
        before = None
        if rel is not None:
            kpos = lax.broadcasted_iota(jnp.int32, (tk, 2 * tq), 0) + rel * tk
            qpos = lax.broadcasted_iota(jnp.int32, (tk, 2 * tq), 1) & (tq - 1)
            before = kpos < qpos
            sp = jnp.where(before, sp, 0.0)
        return kk, zt[0:1, :], sp.astype(BF16), before

    def weights(state, run):
        kk, zt0, sp, before = state
        arg = jnp.dot(jnp.concatenate([neg_tri, kk], axis=1),
                      jnp.concatenate([sp, qcat_t], axis=0), preferred_element_type=F32)
        a = jnp.exp(arg - run)
        if before is not None:
            a = jnp.where(before, a, 0.0)
        a = a.astype(BF16)
        acat = jnp.concatenate([a[:, :tq], a[:, tq:]], axis=0)
        return acat, run + (zt0 - arg[0:1, :])

    def values(kb, acat):
        acc_scr[...] += jnp.dot(vcat_scr[kb], acat, preferred_element_type=F32)

    def run_blocks(blocks, run):
        n = len(blocks)
        st, ac = [None] * n, [None] * n
        for step in range(n + 2):
            if step < n:
                st[step] = scores(*blocks[step])
            if 0 <= step - 1 < n:
                ac[step - 1], run = weights(st[step - 1], run)
            if 0 <= step - 2 < n:
                values(blocks[step - 2][0], ac[step - 2])
        return run

    run = jnp.zeros((1, 2 * tq), F32)
    run = run_blocks([(qi * nblk + rel, rel) for rel in reversed(range(nblk))], run)

    def alive(run):
        return (jnp.min(run) < SB_DEAD_RUN).astype(jnp.int32)

    def more(carry):
        it, _, go = carry
        return jnp.logical_and(it < qi, go > 0)

    def chunk(carry):
        it, run, _ = carry
        base = (qi - 1 - it) * nblk
        run = run_blocks([(base + j, None) for j in reversed(range(nblk))], run)
        return it + 1, run, alive(run)

    lax.while_loop(more, chunk, (jnp.int32(0), run, alive(run)))
    o_ref[0] = acc_scr[...].T.astype(o_ref.dtype)


def _sb_attention(proj, bsz, seq):
    e = proj.shape[-1]
    view = proj.reshape(bsz, seq, e)
    pairs = SB_WIDTH // LANES
    qo, ko, vo = COL_SB // LANES, (COL_SB + SB_WIDTH) // LANES, (COL_SB + 2 * SB_WIDTH) // LANES
    out = pl.pallas_call(
        functools.partial(_sb_attn_kernel, seq=seq),
        grid=(bsz, pairs, seq // SB_TQ),
        in_specs=[pl.BlockSpec((1, SB_TQ, LANES), lambda b, p, i: (b, i, qo + p)),
                  pl.BlockSpec((1, seq, LANES), lambda b, p, i: (b, 0, ko + p)),
                  pl.BlockSpec((1, seq, LANES), lambda b, p, i: (b, 0, vo + p))],
        out_specs=pl.BlockSpec((1, SB_TQ, LANES), lambda b, p, i: (b, i, p)),
        out_shape=jax.ShapeDtypeStruct((bsz, seq, SB_WIDTH), BF16),
        scratch_shapes=[pltpu.VMEM((seq // SB_TK, LANES, 2 * SB_TK), BF16),
                        pltpu.VMEM((LANES, SB_TQ), F32)],
        compiler_params=_cparams("arbitrary", "arbitrary", "arbitrary"),
        name="stick_breaking_attn",
    )(view, view, view)
    return out.reshape(bsz * seq, SB_WIDTH)


def _conv_kernel(val_ref, gate_ref, w_ref, b_ref, g_ref, beta_ref, o_ref, u_scr, *, seq):
    tr, pad, half = CONV_ROWS, CONV_PAD, CONV_CH // 2
    u_scr[pl.ds(0, pad), :] = jnp.zeros((pad, CONV_CH), F32)

    def glu(i, c):
        rows = pl.ds(pl.multiple_of(i * 256, 256), 256)
        val = val_ref[0, rows, :].astype(F32)
        gate = gate_ref[0, rows, :].astype(F32)
        u_scr[pl.ds(pl.multiple_of(pad + i * 256, SUBLANES), 256), :] = val * _sigmoid(gate)
        return c

    lax.fori_loop(0, seq // 256, glu, 0)

    def tile(i, c):
        t0 = pl.multiple_of(i * tr, tr)
        parts = []
        for ch in range(2):
            cs = slice(ch * half, (ch + 1) * half)
            win = u_scr[pl.ds(t0, tr + pad), cs]
            acc = jnp.zeros((tr, half), F32)
            first = pad - (CONV_WIDTH - 1)
            for r in range(SUBLANES):
                offs = [o for o in range(first, first + CONV_WIDTH) if o % SUBLANES == r]
                shifted = win[r:offs[-1] + tr, :]
                for o in offs:
                    w = o - first
                    acc = acc + shifted[o - r:o - r + tr, :] * w_ref[w:w + 1, cs]
            parts.append(acc)
        y = jnp.concatenate(parts, axis=-1) + b_ref[...]
        mu = jnp.mean(y, axis=-1, keepdims=True)
        yc = y - mu
        var = jnp.mean(yc * yc, axis=-1, keepdims=True)
        yn = yc * lax.rsqrt(var + EPS) * g_ref[...] + beta_ref[...]
        o_ref[0, pl.ds(t0, tr), :] = (yn * _sigmoid(yn)).astype(o_ref.dtype)
        return c

    lax.fori_loop(0, seq // tr, tile, 0)


def _conformer_conv(proj, conv_w, conv_b, norm_g, norm_b, bsz, seq):
    e = proj.shape[-1]
    view = proj.reshape(bsz, seq, e)
    c = CONV_CH
    voff, goff = COL_GLU // c, (COL_GLU + c) // c
    const = lambda b: (0, 0)
    out = pl.pallas_call(
        functools.partial(_conv_kernel, seq=seq),
        grid=(bsz,),
        in_specs=[pl.BlockSpec((1, seq, c), lambda b: (b, 0, voff)),
                  pl.BlockSpec((1, seq, c), lambda b: (b, 0, goff)),
                  pl.BlockSpec((CONV_WIDTH, c), const),
                  pl.BlockSpec((1, c), const), pl.BlockSpec((1, c), const), pl.BlockSpec((1, c), const)],
        out_specs=pl.BlockSpec((1, seq, c), lambda b: (b, 0, 0)),
        out_shape=jax.ShapeDtypeStruct((bsz, seq, c), BF16),
        scratch_shapes=[pltpu.VMEM((seq + CONV_PAD, c), F32)],
        compiler_params=_cparams("arbitrary"),
        name="conformer_conv",
    )(view, view, conv_w, conv_b.reshape(1, c), norm_g.reshape(1, c), norm_b.reshape(1, c))
    return out.reshape(bsz * seq, c)


def _merge_kernel(o1_ref, o2_ref, o3_ref, l1_ref, l2_ref, l3_ref, ob_ref, oc_ref,
                  ga_ref, gb_ref, gc_ref, x_ref, wa_ref, wb_ref, wc_ref, wo_ref, out_ref):
    by_lanes = lambda ref: jnp.concatenate([ref[p] for p in range(ref.shape[0])], axis=-1)
    l1, l2, l3 = by_lanes(l1_ref), by_lanes(l2_ref), by_lanes(l3_ref)
    m = jnp.maximum(jnp.maximum(l1, l2), l3)
    e1, e2, e3 = jnp.exp(l1 - m), jnp.exp(l2 - m), jnp.exp(l3 - m)
    o_a = (e1 * o1_ref[...].astype(F32) + e2 * o2_ref[...].astype(F32)
           + e3 * o3_ref[...].astype(F32)) / (e1 + e2 + e3)
    ya = jnp.dot(o_a.astype(BF16), wa_ref[...], preferred_element_type=F32)
    yb = jnp.dot(ob_ref[...], wb_ref[...], preferred_element_type=F32)
    yc = jnp.dot(oc_ref[...], wc_ref[...], preferred_element_type=F32)
    merged = (_sigmoid(ga_ref[...].astype(F32)) * ya + _sigmoid(gb_ref[...].astype(F32)) * yb
              + _sigmoid(gc_ref[...].astype(F32)) * yc)
    out_ref[...] = x_ref[...] + jnp.dot(merged.astype(BF16), wo_ref[...], preferred_element_type=F32)


def _merge(o_groups, lse_groups, o_b, o_c, proj, x, wa, wb, wc, wo, *, tm):
    n, d = x.shape
    row = lambda w, j=0: pl.BlockSpec((tm, w), lambda i: (i, j))
    full = lambda a: pl.BlockSpec(a.shape, lambda i: (0, 0), pipeline_mode=pl.Buffered(1))
    g0 = COL_GATES // d
    _, pairs, seq, _ = lse_groups[0].shape
    per_batch = seq // tm
    lse = pl.BlockSpec((None, pairs, tm, LANES), lambda i: (i // per_batch, 0, i % per_batch, 0))
    return pl.pallas_call(
        _merge_kernel,
        grid=(n // tm,),
        in_specs=[row(DIL_OUT), row(DIL_OUT), row(DIL_OUT), lse, lse, lse,
                  row(SB_WIDTH), row(CONV_CH), row(d, g0), row(d, g0 + 1), row(d, g0 + 2), row(d),
                  full(wa), full(wb), full(wc), full(wo)],
        out_specs=row(d),
        out_shape=jax.ShapeDtypeStruct((n, d), F32),
        compiler_params=_cparams("arbitrary"),
        name="branch_merge_out_proj",
    )(*o_groups, *lse_groups, o_b, o_c, proj, proj, proj, x, wa, wb, wc, wo)


def _swiglu_tile(h, wg_ref, wu_ref, wd_ref):
    y = None
    for c in range(D_FF // FF_CHUNK):
        cs = slice(c * FF_CHUNK, (c + 1) * FF_CHUNK)
        a = jnp.dot(h, wg_ref[:, cs], preferred_element_type=F32)
        u = jnp.dot(h, wu_ref[:, cs], preferred_element_type=F32)
        act = (a * _sigmoid(a) * u).astype(BF16)
        part = jnp.dot(act, wd_ref[cs, :], preferred_element_type=F32)
        y = part if y is None else y + part
    return y


def _dense_ffn_kernel(x_ref, g_ref, wg_ref, wu_ref, wd_ref, o_ref):
    x = x_ref[...]
    ms = jnp.mean(x * x, axis=-1, keepdims=True)
    h = ((x * lax.rsqrt(ms + EPS)) * g_ref[...]).astype(BF16)
    o_ref[...] = x + _swiglu_tile(h, wg_ref, wu_ref, wd_ref)


def _dense_ffn(x, g, wg, wu, wd, *, tm):
    n, d = x.shape
    full = lambda a: pl.BlockSpec(a.shape, lambda i: (0, 0), pipeline_mode=pl.Buffered(1))
    return pl.pallas_call(
        _dense_ffn_kernel,
        grid=(n // tm,),
        in_specs=[pl.BlockSpec((tm, d), lambda i: (i, 0)), pl.BlockSpec((1, d), lambda i: (0, 0)),
                  full(wg), full(wu), full(wd)],
        out_specs=pl.BlockSpec((tm, d), lambda i: (i, 0)),
        out_shape=jax.ShapeDtypeStruct((n, d), F32),
        compiler_params=_cparams("arbitrary"),
        name="dense_swiglu",
    )(x, g.reshape(1, d), wg, wu, wd)


def _router_kernel(x_ref, g_ref, wr_ref, br_ref, h_ref, route_ref):
    x = x_ref[...]
    ms = jnp.mean(x * x, axis=-1, keepdims=True)
    h = (x * lax.rsqrt(ms + EPS)) * g_ref[...]
    _store_token_tiles(h_ref, h)
    logits = jnp.dot(h, wr_ref[...], preferred_element_type=F32,
                     precision=lax.Precision.HIGHEST) + br_ref[...]
    lane = lax.broadcasted_iota(jnp.int32, logits.shape, 1)
    m1 = jnp.max(logits, axis=-1, keepdims=True)
    i1 = jnp.min(jnp.where(logits == m1, lane, LANES), axis=-1, keepdims=True)
    rest = jnp.where(lane == i1, -jnp.inf, logits)
    m2 = jnp.max(rest, axis=-1, keepdims=True)
    i2 = jnp.min(jnp.where(rest == m2, lane, LANES), axis=-1, keepdims=True)
    e2 = jnp.exp(m2 - m1)
    g1 = 1.0 / (1.0 + e2)
    g2 = e2 / (1.0 + e2)
    route = jnp.where(lane == 0, i1.astype(F32),
                      jnp.where(lane == 1, i2.astype(F32),
                                jnp.where(lane == 2, g1, jnp.where(lane == 3, g2, 0.0))))
    route_ref[...] = route


def _router(x, g, w_router, b_router, *, tm):
    n, d = x.shape
    wr = jnp.zeros((d, LANES), F32).at[:, :N_EXPERTS].set(w_router)
    br = jnp.full((1, LANES), NEG_BIG, F32).at[0, :N_EXPERTS].set(b_router)
    return pl.pallas_call(
        _router_kernel,
        grid=(n // tm,),
        in_specs=[pl.BlockSpec((tm, d), lambda i: (i, 0)), pl.BlockSpec((1, d), lambda i: (0, 0)),
                  pl.BlockSpec((d, LANES), lambda i: (0, 0)), pl.BlockSpec((1, LANES), lambda i: (0, 0))],
        out_specs=[pl.BlockSpec((tm * ROW_TILE, LANES), lambda i: (i, 0)),
                   pl.BlockSpec((tm, LANES), lambda i: (i, 0))],
        out_shape=[jax.ShapeDtypeStruct((n * ROW_TILE, LANES), F32), jax.ShapeDtypeStruct((n, LANES), F32)],
        compiler_params=_cparams("arbitrary"),
        name="router_top2",
    )(x, g.reshape(1, d), wr, br)


ROW_TILE = D_MODEL // LANES


def _store_token_tiles(ref, x):
    rows = x.shape[0]
    for c in range(ROW_TILE):
        ref[pl.ds(c, rows, stride=ROW_TILE), :] = x[:, c * LANES:(c + 1) * LANES]


def _load_token_tiles(ref, rows):
    return jnp.concatenate([ref[pl.ds(c, rows, stride=ROW_TILE), :] for c in range(ROW_TILE)], axis=-1)


def _row_gather_start(idx_ref, blk, src_hbm, dst, sem, rows):
    def body(g, c):
        for j in range(SUBLANES):
            r = g * SUBLANES + j
            tok = idx_ref[blk, r]
            pltpu.make_async_copy(src_hbm.at[pl.ds(pl.multiple_of(tok * ROW_TILE, ROW_TILE), ROW_TILE), :],
                                  dst.at[pl.ds(pl.multiple_of(r * ROW_TILE, ROW_TILE), ROW_TILE), :],
                                  sem).start()
        return c

    lax.fori_loop(0, rows // SUBLANES, body, 0)


def _row_gather_wait(src_hbm, dst, sem, rows):
    pltpu.make_async_copy(src_hbm.at[pl.ds(0, rows * ROW_TILE), :], dst, sem).wait()


def _expert_kernel(be_ref, used_ref, tok_ref, h_hbm, wg_ref, wu_ref, wd_ref, y_ref, xbuf, sem):
    i = pl.program_id(0)
    used = used_ref[0]
    slot = i % 2

    @pl.when(i == 0)
    def _():
        _row_gather_start(tok_ref, 0, h_hbm, xbuf.at[0], sem.at[0], MOE_TM)

    @pl.when(i + 1 < used)
    def _():
        _row_gather_start(tok_ref, i + 1, h_hbm, xbuf.at[1 - slot], sem.at[1 - slot], MOE_TM)

    @pl.when(i < used)
    def _():
        _row_gather_wait(h_hbm, xbuf.at[slot], sem.at[slot], MOE_TM)
        x = _load_token_tiles(xbuf.at[slot], MOE_TM).astype(BF16)
        _store_token_tiles(y_ref, _swiglu_tile(x, wg_ref, wu_ref, wd_ref))

    @pl.when(i >= used)
    def _():
        y_ref[...] = jnp.zeros_like(y_ref)


def _experts(h, block_expert, used, row_tok, wg, wu, wd):
    d = D_MODEL
    n_blocks = row_tok.shape[0]
    f = wg.shape[-1]
    one = pl.Buffered(1)
    grid_spec = pltpu.PrefetchScalarGridSpec(
        num_scalar_prefetch=3,
        grid=(n_blocks,),
        in_specs=[pl.BlockSpec(memory_space=pl.ANY),
                  pl.BlockSpec((None, d, f), lambda i, be, us, tk: (be[i], 0, 0), pipeline_mode=one),
                  pl.BlockSpec((None, d, f), lambda i, be, us, tk: (be[i], 0, 0), pipeline_mode=one),
                  pl.BlockSpec((None, f, d), lambda i, be, us, tk: (be[i], 0, 0), pipeline_mode=one)],
        out_specs=pl.BlockSpec((MOE_TM * ROW_TILE, LANES), lambda i, be, us, tk: (i, 0)),
        scratch_shapes=[pltpu.VMEM((2, MOE_TM * ROW_TILE, LANES), F32), pltpu.SemaphoreType.DMA((2,))],
    )
    return pl.pallas_call(
        _expert_kernel,
        grid_spec=grid_spec,
        out_shape=jax.ShapeDtypeStruct((n_blocks * MOE_TM * ROW_TILE, LANES), F32),
        compiler_params=_cparams("arbitrary"),
        name="expert_swiglu",
    )(block_expert, used, row_tok, h, wg, wu, wd)


def _combine_kernel(p0_ref, p1_ref, ys_hbm, x_ref, route_ref, o_ref, buf, sem):
    i = pl.program_id(0)
    nsteps = pl.num_programs(0)
    slot = i % 2
    tt = COMBINE_TT

    def start(step, s):
        _row_gather_start(p0_ref, step, ys_hbm, buf.at[s, 0], sem.at[s, 0], tt)
        _row_gather_start(p1_ref, step, ys_hbm, buf.at[s, 1], sem.at[s, 1], tt)

    @pl.when(i == 0)
    def _():
        start(0, 0)

    @pl.when(i + 1 < nsteps)
    def _():
        start(i + 1, 1 - slot)

    _row_gather_wait(ys_hbm, buf.at[slot, 0], sem.at[slot, 0], tt)
    _row_gather_wait(ys_hbm, buf.at[slot, 1], sem.at[slot, 1], tt)
    route = route_ref[...]
    y0 = _load_token_tiles(buf.at[slot, 0], tt)
    y1 = _load_token_tiles(buf.at[slot, 1], tt)
    o_ref[...] = x_ref[...] + route[:, 2:3] * y0 + route[:, 3:4] * y1


def _combine(x, ys, route, pos0, pos1):
    n, d = x.shape
    tt = COMBINE_TT
    grid_spec = pltpu.PrefetchScalarGridSpec(
        num_scalar_prefetch=2,
        grid=(n // tt,),
        in_specs=[pl.BlockSpec(memory_space=pl.ANY),
                  pl.BlockSpec((tt, d), lambda i, a, b: (i, 0)),
                  pl.BlockSpec((tt, LANES), lambda i, a, b: (i, 0))],
        out_specs=pl.BlockSpec((tt, d), lambda i, a, b: (i, 0)),
        scratch_shapes=[pltpu.VMEM((2, 2, tt * ROW_TILE, LANES), F32), pltpu.SemaphoreType.DMA((2, 2))],
    )
    return pl.pallas_call(
        _combine_kernel,
        grid_spec=grid_spec,
        out_shape=jax.ShapeDtypeStruct((n, d), F32),
        compiler_params=_cparams("arbitrary"),
        name="expert_combine",
    )(pos0.reshape(n // tt, tt), pos1.reshape(n // tt, tt), ys, x, route)


def _routed_ffn(x, g, w_router, b_router, wg, wu, wd, *, tm):
    n, d = x.shape
    h, route = _router(x, g, w_router, b_router, tm=tm)
    e_flat = route[:, :TOP_K].astype(jnp.int32).reshape(-1)
    onehot = (e_flat[:, None] == jnp.arange(N_EXPERTS)[None, :]).astype(jnp.int32)
    csum = jnp.cumsum(onehot, axis=0)
    counts = csum[-1]
    rank = jnp.sum((csum - onehot) * onehot, axis=1)
    padded = (counts + MOE_TM - 1) // MOE_TM * MOE_TM
    pend = jnp.cumsum(padded)
    pstart = pend - padded
    dest = (pstart[e_flat] + rank).astype(jnp.int32)
    n_blocks = (n * TOP_K) // MOE_TM + N_EXPERTS
    tok = (jnp.arange(n * TOP_K, dtype=jnp.int32) // TOP_K)
    row_tok = jnp.zeros((n_blocks * MOE_TM,), jnp.int32).at[dest].set(tok).reshape(n_blocks, MOE_TM)
    block_expert = jnp.clip(jnp.searchsorted(pend, jnp.arange(n_blocks) * MOE_TM, side='right'),
                            0, N_EXPERTS - 1).astype(jnp.int32)
    used = (pend[-1:] // MOE_TM).astype(jnp.int32)
    ys = _experts(h, block_expert, used, row_tok, wg, wu, wd)
    pos = dest.reshape(n, TOP_K)
    return _combine(x, ys, route, pos[:, 0], pos[:, 1])


def kernel(x, attn_norm_g, w_in, q_norm_g, k_norm_g, conv_w, conv_b, conv_norm_g, conv_norm_b,
           w_branch_a, w_branch_b, w_branch_c, w_out, ffn_norm_g, w_ffn_gate, w_ffn_up,
           w_ffn_down, w_router, b_router, w_exp_gate, w_exp_up, w_exp_down):
    bsz, seq, d = x.shape
    depth = attn_norm_g.shape[0]
    n = bsz * seq
    tm = 512
    xf = x.reshape(n, d)
    s_dil, s_sb, s_glu = 3 * DIL_WIDTH, 3 * DIL_WIDTH + 3 * SB_WIDTH, 3 * DIL_WIDTH + 3 * SB_WIDTH + 2 * CONV_CH
    for layer in range(depth):
        wl = w_in[layer]
        w_perm = jnp.concatenate([wl[:, s_glu:], wl[:, s_sb:s_glu], wl[:, s_dil:s_sb], wl[:, :s_dil]],
                                 axis=1).astype(BF16)
        proj = _norm_matmul(xf, attn_norm_g[layer], w_perm, tm=tm, tn=IN_COLS // 2)
        o_groups, lse_groups = [], []
        for gi, (window, dilation) in enumerate(DIL_GROUPS):
            o_g, lse_g = _dil_attention(proj, q_norm_g[layer], k_norm_g[layer], gi, window, dilation,
                                        bsz, seq)
            o_groups.append(o_g)
            lse_groups.append(lse_g)
        o_b = _sb_attention(proj, bsz, seq)
        o_c = _conformer_conv(proj, conv_w[layer], conv_b[layer], conv_norm_g[layer],
                              conv_norm_b[layer], bsz, seq)
        xf = _merge(o_groups, lse_groups, o_b, o_c, proj, xf,
                    w_branch_a[layer].astype(BF16), w_branch_b[layer].astype(BF16),
                    w_branch_c[layer].astype(BF16), w_out[layer].astype(BF16), tm=tm)
        i = layer // 2
        if layer % 2 == 0:
            xf = _dense_ffn(xf, ffn_norm_g[layer], w_ffn_gate[i].astype(BF16), w_ffn_up[i].astype(BF16),
                            w_ffn_down[i].astype(BF16), tm=tm)
        else:
            xf = _routed_ffn(xf, ffn_norm_g[layer], w_router[i], b_router[i],
                             w_exp_gate[i].astype(BF16), w_exp_up[i].astype(BF16),
                             w_exp_down[i].astype(BF16), tm=tm)
    return xf.reshape(bsz, seq, d)
```

```python
import functools
import math

import jax
import jax.numpy as jnp
from jax import lax
from jax.experimental import pallas as pl
from jax.experimental.pallas import tpu as pltpu

F32 = jnp.float32
BF16 = jnp.bfloat16

D_MODEL = 1024
HEAD_DIM = 64
DIL_GROUPS = ((128, 1), (512, 4), (2048, 16))
DIL_HEADS_PER_GROUP = 4
DIL_HEADS = len(DIL_GROUPS) * DIL_HEADS_PER_GROUP
DIL_WIDTH = DIL_HEADS * HEAD_DIM
DIL_OUT = DIL_HEADS_PER_GROUP * HEAD_DIM
DIL_BLOCK = 128
SB_HEADS = 8
SB_WIDTH = SB_HEADS * HEAD_DIM
CONV_CH = D_MODEL // 2
CONV_WIDTH = 31
N_BRANCH = 3
IN_COLS = 3 * DIL_WIDTH + 3 * SB_WIDTH + 2 * CONV_CH + N_BRANCH * D_MODEL
D_FF = 2816
N_EXPERTS = 8
TOP_K = 2
EPS = 1e-6
ALIBI_MAX_BIAS = 8.0
NEG_BIG = -1e30

COL_GATES = 0
COL_GLU = COL_GATES + N_BRANCH * D_MODEL
COL_SB = COL_GLU + 2 * CONV_CH
COL_DIL = COL_SB + 3 * SB_WIDTH

LANES = 128
SUBLANES = 8
VMEM_LIMIT = 56 * 1024 * 1024

DIL_UNITS = 4
SB_TQ = 512
SB_TK = 128
SB_DEAD_RUN = 128.0
CONV_ROWS = 64
CONV_PAD = 32
FF_CHUNK = 1408
MOE_TM = 512
COMBINE_TT = 256


def _cparams(*sem):
    return pltpu.CompilerParams(dimension_semantics=sem, vmem_limit_bytes=VMEM_LIMIT)


def _sigmoid(x):
    return 1.0 / (1.0 + jnp.exp(-x))


def _norm_matmul_kernel(x_ref, g_ref, w_ref, o_ref):
    x = x_ref[...]
    ms = jnp.mean(x * x, axis=-1, keepdims=True)
    h = (x * lax.rsqrt(ms + EPS)) * g_ref[...]
    o_ref[...] = jnp.dot(h.astype(BF16), w_ref[...], preferred_element_type=F32).astype(o_ref.dtype)


def _norm_matmul(x, g, w, *, tm, tn):
    n, d = x.shape
    e = w.shape[1]
    return pl.pallas_call(
        _norm_matmul_kernel,
        grid=(e // tn, n // tm),
        in_specs=[pl.BlockSpec((tm, d), lambda j, i: (i, 0)),
                  pl.BlockSpec((1, d), lambda j, i: (0, 0)),
                  pl.BlockSpec((d, tn), lambda j, i: (0, j))],
        out_specs=pl.BlockSpec((tm, tn), lambda j, i: (i, j)),
        out_shape=jax.ShapeDtypeStruct((n, e), BF16),
        compiler_params=_cparams("arbitrary", "arbitrary"),
        name="norm_in_proj",
    )(x, g.reshape(1, d), w)


def _same_head_matrix(w):
    r = lax.broadcasted_iota(jnp.int32, (w, w), 0) // HEAD_DIM
    c = lax.broadcasted_iota(jnp.int32, (w, w), 1) // HEAD_DIM
    return (r == c).astype(BF16)


def _head_rms_scale(t, same_head):
    sq = t * t
    hi = sq.astype(BF16)
    lo = (sq - hi.astype(F32)).astype(BF16)
    ssq = (jnp.dot(hi, same_head, preferred_element_type=F32)
           + jnp.dot(lo, same_head, preferred_element_type=F32))
    return lax.rsqrt(ssq * (1.0 / HEAD_DIM) + EPS)


def _dil_attn_kernel(q_ref, k_ref, v_ref, qg_ref, kg_ref, bias_ref, o_ref, lse_ref,
                     qn_scr, kn_scr, v_scr, o_scr, *, seq, dilation):
    blk, d = DIL_BLOCK, dilation
    span = blk * d
    log2d = d.bit_length() - 1
    same_head = _same_head_matrix(DIL_OUT)
    qg = qg_ref[...] * (1.0 / math.sqrt(HEAD_DIM))
    kg = kg_ref[...]
    chunk = 256

    pairs = DIL_OUT // LANES
    pair_lanes = [slice(p * LANES, (p + 1) * LANES) for p in range(pairs)]
    for p in range(pairs):
        kn_scr[p, pl.ds(0, span), :] = jnp.zeros((span, LANES), F32)
        v_scr[p, pl.ds(0, span), :] = jnp.zeros((span, LANES), F32)

    def norm(i, c):
        rows = pl.ds(pl.multiple_of(i * chunk, chunk), chunk)
        prows = pl.ds(pl.multiple_of(span + i * chunk, blk), chunk)
        q = q_ref[0, rows, :].astype(F32)
        qn = q * _head_rms_scale(q, same_head) * qg
        k = k_ref[0, rows, :].astype(F32)
        kn = k * _head_rms_scale(k, same_head) * kg
        v = v_ref[0, rows, :].astype(F32)
        for p in range(pairs):
            qn_scr[p, rows, :] = qn[:, pair_lanes[p]]
            kn_scr[p, prows, :] = kn[:, pair_lanes[p]]
            v_scr[p, prows, :] = v[:, pair_lanes[p]]
        return c

    lax.fori_loop(0, seq // chunk, norm, 0)

    low_half = lax.broadcasted_iota(jnp.int32, (blk, LANES), 1) < HEAD_DIM

    def sub_rows(base, count):
        return pl.ds(pl.multiple_of(base, blk), count) if d == 1 else pl.ds(base, count, stride=d)

    def load(u):
        if d == 1:
            n, base = u, u * span
        else:
            n = u >> log2d
            base = (u & (d - 1)) + n * span
        q = [qn_scr[p, sub_rows(base, blk), :] for p in range(pairs)]
        kk = [kn_scr[p, sub_rows(base, 2 * blk), :].astype(BF16) for p in range(pairs)]
        vv = [v_scr[p, sub_rows(base, 2 * blk), :].astype(BF16) for p in range(pairs)]
        first = jnp.where(n == 0, 1, 0)
        return base, q, kk, vv, first

    def scores(unit):
        _, q, kk, _, first = unit
        out = []
        for h in range(DIL_HEADS_PER_GROUP):
            keep = low_half if h % 2 == 0 else jnp.logical_not(low_half)
            qh = jnp.where(keep, q[h // 2], 0.0).astype(BF16)
            s = lax.dot_general(qh, kk[h // 2], (((1,), (1,)), ((), ())), preferred_element_type=F32)
            out.append(s + bias_ref[h, first])
        return out

    def finish(unit, s_list):
        base, _, _, vv, _ = unit
        rows = sub_rows(base, blk)
        for p in range(pairs):
            oh, lh = [], []
            for hh in range(2):
                s = s_list[2 * p + hh]
                m = jnp.max(s, axis=-1, keepdims=True)
                e = jnp.exp(s - m)
                den = jnp.sum(e, axis=-1, keepdims=True)
                oh.append(jnp.dot(e.astype(BF16), vv[p], preferred_element_type=F32) * (1.0 / den))
                lh.append(m + jnp.log(den))
            o_scr[p, rows, :] = jnp.where(low_half, oh[0], oh[1])
            lse_ref[0, p, rows, :] = jnp.where(low_half, lh[0], lh[1])

    def body(it, c):
        units = [load(it * DIL_UNITS + i) for i in range(DIL_UNITS)]
        s = [None] * DIL_UNITS
        for step in range(DIL_UNITS + 1):
            if step < DIL_UNITS:
                s[step] = scores(units[step])
            if step >= 1:
                finish(units[step - 1], s[step - 1])
        return c

    lax.fori_loop(0, seq // blk // DIL_UNITS, body, 0)

    def emit(i, c):
        rows = pl.ds(pl.multiple_of(i * chunk, chunk), chunk)
        o_ref[0, rows, :] = jnp.concatenate([o_scr[p, rows, :] for p in range(pairs)],
                                            axis=-1).astype(o_ref.dtype)
        return c

    lax.fori_loop(0, seq // chunk, emit, 0)


def _dil_bias_table(group, window, dilation):
    reach = window // dilation
    assert reach <= DIL_BLOCK
    slopes = 2.0 ** (-ALIBI_MAX_BIAS * jnp.arange(1, DIL_HEADS + 1, dtype=F32) / DIL_HEADS)
    slopes = slopes[group * DIL_HEADS_PER_GROUP:(group + 1) * DIL_HEADS_PER_GROUP]
    qi = jnp.arange(DIL_BLOCK)[:, None] + DIL_BLOCK
    ki = jnp.arange(2 * DIL_BLOCK)[None, :]
    dist = qi - ki
    valid = (dist >= 0) & (dist <= reach)
    bias = -slopes[:, None, None] * (dist * dilation).astype(F32)[None]
    general = jnp.where(valid[None], bias, NEG_BIG)
    first = jnp.where((valid & (ki >= DIL_BLOCK))[None], bias, NEG_BIG)
    return jnp.stack([general, first], axis=1)


def _dil_attention(proj, q_gain, k_gain, group, window, dilation, bsz, seq):
    e = proj.shape[-1]
    assert (seq // DIL_BLOCK) % DIL_UNITS == 0
    assert seq % (DIL_BLOCK * dilation) == 0 and dilation & (dilation - 1) == 0
    view = proj.reshape(bsz, seq, e)
    w = DIL_OUT
    pairs = w // LANES
    pad = DIL_BLOCK * dilation

    def col(base):
        off = (base + group * w) // w
        return lambda b: (b, 0, off)

    bias = _dil_bias_table(group, window, dilation)
    gain = lambda g: jnp.tile(g, DIL_HEADS_PER_GROUP).reshape(1, w)
    o, lse = pl.pallas_call(
        functools.partial(_dil_attn_kernel, seq=seq, dilation=dilation),
        grid=(bsz,),
        in_specs=[pl.BlockSpec((1, seq, w), col(COL_DIL)),
                  pl.BlockSpec((1, seq, w), col(COL_DIL + DIL_WIDTH)),
                  pl.BlockSpec((1, seq, w), col(COL_DIL + 2 * DIL_WIDTH)),
                  pl.BlockSpec((1, w), lambda b: (0, 0)),
                  pl.BlockSpec((1, w), lambda b: (0, 0)),
                  pl.BlockSpec(bias.shape, lambda b: (0, 0, 0, 0))],
        out_specs=[pl.BlockSpec((1, seq, w), lambda b: (b, 0, 0)),
                   pl.BlockSpec((1, pairs, seq, LANES), lambda b: (b, 0, 0, 0))],
        out_shape=[jax.ShapeDtypeStruct((bsz, seq, w), BF16),
                   jax.ShapeDtypeStruct((bsz, pairs, seq, LANES), F32)],
        scratch_shapes=[pltpu.VMEM((pairs, seq, LANES), F32), pltpu.VMEM((pairs, pad + seq, LANES), F32),
                        pltpu.VMEM((pairs, pad + seq, LANES), F32), pltpu.VMEM((pairs, seq, LANES), F32)],
        compiler_params=_cparams("arbitrary"),
        name=f"dilated_attn_g{group}",
    )(view, view, view, gain(q_gain), gain(k_gain), bias)
    return o.reshape(bsz * seq, w), lse


def _sb_attn_kernel(q_ref, k_ref, v_ref, o_ref, vcat_scr, acc_scr, *, seq):
    tq, tk, nblk = SB_TQ, SB_TK, SB_TQ // SB_TK
    qi = pl.program_id(2)

    @pl.when(qi == 0)
    def _():
        chan = lax.broadcasted_iota(jnp.int32, (LANES, tk), 0)

        def build(kb, c):
            vt = v_ref[0, pl.ds(pl.multiple_of(kb * tk, tk), tk), :].astype(F32).T
            vcat_scr[kb] = jnp.concatenate(
                [jnp.where(chan < HEAD_DIM, vt, 0.0), jnp.where(chan >= HEAD_DIM, vt, 0.0)],
                axis=1).astype(BF16)
            return c

        lax.fori_loop(0, seq // tk, build, 0)

    lane = lax.broadcasted_iota(jnp.int32, (tq, LANES), 1)
    q = q_ref[0] * (1.0 / math.sqrt(HEAD_DIM))
    zero = jnp.zeros_like(q)
    qcat = jnp.concatenate([jnp.where(lane < HEAD_DIM, q, zero), jnp.where(lane >= HEAD_DIM, q, zero)],
                           axis=0)
    qcat_t = qcat.astype(F32).T.astype(BF16)
    neg_tri = jnp.where(lax.broadcasted_iota(jnp.int32, (tk, tk), 1)
                        >= lax.broadcasted_iota(jnp.int32, (tk, tk), 0), -1.0, 0.0).astype(BF16)
    acc_scr[...] = jnp.zeros_like(acc_scr)

    def scores(kb, rel):
        kk = k_ref[0, pl.ds(pl.multiple_of(kb * tk, tk), tk), :]
        zt = lax.dot_general(kk, qcat, (((1,), (1,)), ((), ())), preferred_element_type=F32)
        neg_abs = lax.bitcast_convert_type(
            lax.bitcast_convert_type(zt, jnp.uint32) | jnp.uint32(0x80000000), F32)
        sp = jnp.maximum(zt, 0.0) + jnp.log(1.0 + jnp.exp(neg_abs))
        before = None
        if rel is not None:
            kpos = lax.broadcasted_iota(jnp.int32, (tk, 2 * tq), 0) + rel * tk
            qpos = lax.broadcasted_iota(jnp.int32, (tk, 2 * tq), 1) & (tq - 1)
            before = kpos < qpos
            sp = jnp.where(before, sp, 0.0)
        return kk, zt[0:1, :], sp.astype(BF16), before

    def weights(state, run):
        kk, zt0, sp, before = state
        arg = jnp.dot(jnp.concatenate([neg_tri, kk], axis=1),
                      jnp.concatenate([sp, qcat_t], axis=0), preferred_element_type=F32)
        a = jnp.exp(arg - run)
        if before is not None:
            a = jnp.where(before, a, 0.0)
        a = a.astype(BF16)
        acat = jnp.concatenate([a[:, :tq], a[:, tq:]], axis=0)
        return acat, run + (zt0 - arg[0:1, :])

    def values(kb, acat):
        acc_scr[...] += jnp.dot(vcat_scr[kb], acat, preferred_element_type=F32)

    def run_blocks(blocks, run):
        n = len(blocks)
        st, ac = [None] * n, [None] * n
        for step in range(n + 2):
            if step < n:
                st[step] = scores(*blocks[step])
            if 0 <= step - 1 < n:
                ac[step - 1], run = weights(st[step - 1], run)
            if 0 <= step - 2 < n:
                values(blocks[step - 2][0], ac[step - 2])
        return run

    run = jnp.zeros((1, 2 * tq), F32)
    run = run_blocks([(qi * nblk + rel, rel) for rel in reversed(range(nblk))], run)

    def alive(run):
        return (jnp.min(run) < SB_DEAD_RUN).astype(jnp.int32)

    def more(carry):
        it, _, go = carry
        return jnp.logical_and(it < qi, go > 0)

    def chunk(carry):
        it, run, _ = carry
        base = (qi - 1 - it) * nblk
        run = run_blocks([(base + j, None) for j in reversed(range(nblk))], run)
        return it + 1, run, alive(run)

    lax.while_loop(more, chunk, (jnp.int32(0), run, alive(run)))
    o_ref[0] = acc_scr[...].T.astype(o_ref.dtype)


def _sb_attention(proj, bsz, seq):
    e = proj.shape[-1]
    view = proj.reshape(bsz, seq, e)
    pairs = SB_WIDTH // LANES
    qo, ko, vo = COL_SB // LANES, (COL_SB + SB_WIDTH) // LANES, (COL_SB + 2 * SB_WIDTH) // LANES
    out = pl.pallas_call(
        functools.partial(_sb_attn_kernel, seq=seq),
        grid=(bsz, pairs, seq // SB_TQ),
        in_specs=[pl.BlockSpec((1, SB_TQ, LANES), lambda b, p, i: (b, i, qo + p)),
                  pl.BlockSpec((1, seq, LANES), lambda b, p, i: (b, 0, ko + p)),
                  pl.BlockSpec((1, seq, LANES), lambda b, p, i: (b, 0, vo + p))],
        out_specs=pl.BlockSpec((1, SB_TQ, LANES), lambda b, p, i: (b, i, p)),
        out_shape=jax.ShapeDtypeStruct((bsz, seq, SB_WIDTH), BF16),
        scratch_shapes=[pltpu.VMEM((seq // SB_TK, LANES, 2 * SB_TK), BF16),
                        pltpu.VMEM((LANES, SB_TQ), F32)],
        compiler_params=_cparams("arbitrary", "arbitrary", "arbitrary"),
        name="stick_breaking_attn",
    )(view, view, view)
    return out.reshape(bsz * seq, SB_WIDTH)


def _conv_kernel(val_ref, gate_ref, w_ref, b_ref, g_ref, beta_ref, o_ref, u_scr, *, seq):
    tr, pad, half = CONV_ROWS, CONV_PAD, CONV_CH // 2
    u_scr[pl.ds(0, pad), :] = jnp.zeros((pad, CONV_CH), F32)

    def glu(i, c):
        rows = pl.ds(pl.multiple_of(i * 256, 256), 256)
        val = val_ref[0, rows, :].astype(F32)
        gate = gate_ref[0, rows, :].astype(F32)
        u_scr[pl.ds(pl.multiple_of(pad + i * 256, SUBLANES), 256), :] = val * _sigmoid(gate)
        return c

    lax.fori_loop(0, seq // 256, glu, 0)

    def tile(i, c):
        t0 = pl.multiple_of(i * tr, tr)
        parts = []
        for ch in range(2):
            cs = slice(ch * half, (ch + 1) * half)
            win = u_scr[pl.ds(t0, tr + pad), cs]
            acc = jnp.zeros((tr, half), F32)
            first = pad - (CONV_WIDTH - 1)
            for r in range(SUBLANES):
                offs = [o for o in range(first, first + CONV_WIDTH) if o % SUBLANES == r]
                shifted = win[r:offs[-1] + tr, :]
                for o in offs:
                    w = o - first
                    acc = acc + shifted[o - r:o - r + tr, :] * w_ref[w:w + 1, cs]
            parts.append(acc)
        y = jnp.concatenate(parts, axis=-1) + b_ref[...]
        mu = jnp.mean(y, axis=-1, keepdims=True)
        yc = y - mu
        var = jnp.mean(yc * yc, axis=-1, keepdims=True)
        yn = yc * lax.rsqrt(var + EPS) * g_ref[...] + beta_ref[...]
        o_ref[0, pl.ds(t0, tr), :] = (yn * _sigmoid(yn)).astype(o_ref.dtype)
        return c

    lax.fori_loop(0, seq // tr, tile, 0)


def _conformer_conv(proj, conv_w, conv_b, norm_g, norm_b, bsz, seq):
    e = proj.shape[-1]
    view = proj.reshape(bsz, seq, e)
    c = CONV_CH
    voff, goff = COL_GLU // c, (COL_GLU + c) // c
    const = lambda b: (0, 0)
    out = pl.pallas_call(
        functools.partial(_conv_kernel, seq=seq),
        grid=(bsz,),
        in_specs=[pl.BlockSpec((1, seq, c), lambda b: (b, 0, voff)),
                  pl.BlockSpec((1, seq, c), lambda b: (b, 0, goff)),
                  pl.BlockSpec((CONV_WIDTH, c), const),
                  pl.BlockSpec((1, c), const), pl.BlockSpec((1, c), const), pl.BlockSpec((1, c), const)],
        out_specs=pl.BlockSpec((1, seq, c), lambda b: (b, 0, 0)),
        out_shape=jax.ShapeDtypeStruct((bsz, seq, c), BF16),
        scratch_shapes=[pltpu.VMEM((seq + CONV_PAD, c), F32)],
        compiler_params=_cparams("arbitrary"),
        name="conformer_conv",
    )(view, view, conv_w, conv_b.reshape(1, c), norm_g.reshape(1, c), norm_b.reshape(1, c))
    return out.reshape(bsz * seq, c)


def _merge_kernel(o1_ref, o2_ref, o3_ref, l1_ref, l2_ref, l3_ref, ob_ref, oc_ref,
                  ga_ref, gb_ref, gc_ref, x_ref, wa_ref, wb_ref, wc_ref, wo_ref, out_ref):
    by_lanes = lambda ref: jnp.concatenate([ref[p] for p in range(ref.shape[0])], axis=-1)
    l1, l2, l3 = by_lanes(l1_ref), by_lanes(l2_ref), by_lanes(l3_ref)
    m = jnp.maximum(jnp.maximum(l1, l2), l3)
    e1, e2, e3 = jnp.exp(l1 - m), jnp.exp(l2 - m), jnp.exp(l3 - m)
    o_a = (e1 * o1_ref[...].astype(F32) + e2 * o2_ref[...].astype(F32)
           + e3 * o3_ref[...].astype(F32)) / (e1 + e2 + e3)
    ya = jnp.dot(o_a.astype(BF16), wa_ref[...], preferred_element_type=F32)
    yb = jnp.dot(ob_ref[...], wb_ref[...], preferred_element_type=F32)
    yc = jnp.dot(oc_ref[...], wc_ref[...], preferred_element_type=F32)
    merged = (_sigmoid(ga_ref[...].astype(F32)) * ya + _sigmoid(gb_ref[...].astype(F32)) * yb
              + _sigmoid(gc_ref[...].astype(F32)) * yc)
    out_ref[...] = x_ref[...] + jnp.dot(merged.astype(BF16), wo_ref[...], preferred_element_type=F32)


def _merge(o_groups, lse_groups, o_b, o_c, proj, x, wa, wb, wc, wo, *, tm):
    n, d = x.shape
    row = lambda w, j=0: pl.BlockSpec((tm, w), lambda i: (i, j))
    full = lambda a: pl.BlockSpec(a.shape, lambda i: (0, 0), pipeline_mode=pl.Buffered(1))
    g0 = COL_GATES // d
    _, pairs, seq, _ = lse_groups[0].shape
    per_batch = seq // tm
    lse = pl.BlockSpec((None, pairs, tm, LANES), lambda i: (i // per_batch, 0, i % per_batch, 0))
    return pl.pallas_call(
        _merge_kernel,
        grid=(n // tm,),
        in_specs=[row(DIL_OUT), row(DIL_OUT), row(DIL_OUT), lse, lse, lse,
                  row(SB_WIDTH), row(CONV_CH), row(d, g0), row(d, g0 + 1), row(d, g0 + 2), row(d),
                  full(wa), full(wb), full(wc), full(wo)],
        out_specs=row(d),
        out_shape=jax.ShapeDtypeStruct((n, d), F32),
        compiler_params=_cparams("arbitrary"),
        name="branch_merge_out_proj",
    )(*o_groups, *lse_groups, o_b, o_c, proj, proj, proj, x, wa, wb, wc, wo)


SWIGLU_DOTS = 3 * (D_FF // FF_CHUNK)


def _swiglu_tile(h, wg_ref, wu_ref, wd_ref, after_dot=None):
    done = lambda k: after_dot(k) if after_dot is not None else None
    y = None
    for c in range(D_FF // FF_CHUNK):
        cs = slice(c * FF_CHUNK, (c + 1) * FF_CHUNK)
        a = jnp.dot(h, wg_ref[:, cs], preferred_element_type=F32)
        done(3 * c)
        u = jnp.dot(h, wu_ref[:, cs], preferred_element_type=F32)
        done(3 * c + 1)
        act = (a * _sigmoid(a) * u).astype(BF16)
        part = jnp.dot(act, wd_ref[cs, :], preferred_element_type=F32)
        done(3 * c + 2)
        y = part if y is None else y + part
    return y


def _dense_ffn_kernel(x_ref, g_ref, wg_ref, wu_ref, wd_ref, o_ref):
    x = x_ref[...]
    ms = jnp.mean(x * x, axis=-1, keepdims=True)
    h = ((x * lax.rsqrt(ms + EPS)) * g_ref[...]).astype(BF16)
    o_ref[...] = x + _swiglu_tile(h, wg_ref, wu_ref, wd_ref)


def _dense_ffn(x, g, wg, wu, wd, *, tm):
    n, d = x.shape
    full = lambda a: pl.BlockSpec(a.shape, lambda i: (0, 0), pipeline_mode=pl.Buffered(1))
    return pl.pallas_call(
        _dense_ffn_kernel,
        grid=(n // tm,),
        in_specs=[pl.BlockSpec((tm, d), lambda i: (i, 0)), pl.BlockSpec((1, d), lambda i: (0, 0)),
                  full(wg), full(wu), full(wd)],
        out_specs=pl.BlockSpec((tm, d), lambda i: (i, 0)),
        out_shape=jax.ShapeDtypeStruct((n, d), F32),
        compiler_params=_cparams("arbitrary"),
        name="dense_swiglu",
    )(x, g.reshape(1, d), wg, wu, wd)


def _router_kernel(x_ref, g_ref, wr_ref, br_ref, h_ref, route_ref):
    x = x_ref[...]
    ms = jnp.mean(x * x, axis=-1, keepdims=True)
    h = (x * lax.rsqrt(ms + EPS)) * g_ref[...]
    _store_token_tiles(h_ref, h)
    logits = jnp.dot(h, wr_ref[...], preferred_element_type=F32,
                     precision=lax.Precision.HIGHEST) + br_ref[...]
    lane = lax.broadcasted_iota(jnp.int32, logits.shape, 1)
    m1 = jnp.max(logits, axis=-1, keepdims=True)
    i1 = jnp.min(jnp.where(logits == m1, lane, LANES), axis=-1, keepdims=True)
    rest = jnp.where(lane == i1, -jnp.inf, logits)
    m2 = jnp.max(rest, axis=-1, keepdims=True)
    i2 = jnp.min(jnp.where(rest == m2, lane, LANES), axis=-1, keepdims=True)
    e2 = jnp.exp(m2 - m1)
    g1 = 1.0 / (1.0 + e2)
    g2 = e2 / (1.0 + e2)
    route = jnp.where(lane == 0, i1.astype(F32),
                      jnp.where(lane == 1, i2.astype(F32),
                                jnp.where(lane == 2, g1, jnp.where(lane == 3, g2, 0.0))))
    route_ref[...] = route


def _router(x, g, w_router, b_router, *, tm):
    n, d = x.shape
    wr = jnp.zeros((d, LANES), F32).at[:, :N_EXPERTS].set(w_router)
    br = jnp.full((1, LANES), NEG_BIG, F32).at[0, :N_EXPERTS].set(b_router)
    return pl.pallas_call(
        _router_kernel,
        grid=(n // tm,),
        in_specs=[pl.BlockSpec((tm, d), lambda i: (i, 0)), pl.BlockSpec((1, d), lambda i: (0, 0)),
                  pl.BlockSpec((d, LANES), lambda i: (0, 0)), pl.BlockSpec((1, LANES), lambda i: (0, 0))],
        out_specs=[pl.BlockSpec((tm * ROW_TILE, LANES), lambda i: (i, 0)),
                   pl.BlockSpec((tm, LANES), lambda i: (i, 0))],
        out_shape=[jax.ShapeDtypeStruct((n * ROW_TILE, LANES), F32), jax.ShapeDtypeStruct((n, LANES), F32)],
        compiler_params=_cparams("arbitrary"),
        name="router_top2",
    )(x, g.reshape(1, d), wr, br)


ROW_TILE = D_MODEL // LANES


def _store_token_tiles(ref, x):
    rows = x.shape[0]
    for c in range(ROW_TILE):
        ref[pl.ds(c, rows, stride=ROW_TILE), :] = x[:, c * LANES:(c + 1) * LANES]


def _load_token_tiles(ref, rows):
    return jnp.concatenate([ref[pl.ds(c, rows, stride=ROW_TILE), :] for c in range(ROW_TILE)], axis=-1)


def _tile_rows(t):
    start = t * ROW_TILE
    return pl.ds(start if isinstance(start, int) else pl.multiple_of(start, ROW_TILE), ROW_TILE)


class _BlockRows:
    def __init__(self, dst_ref, n_tokens):
        self.dst_ref = dst_ref
        self.last_token = n_tokens - 1

    def gather_row(self, blk, r, h_hbm, xbuf, sem):
        tok = lax.shift_right_logical(self.dst_ref[blk, r], TOP_K.bit_length() - 1)
        tok = jnp.minimum(tok, self.last_token)
        pltpu.make_async_copy(h_hbm.at[_tile_rows(tok), :], xbuf.at[_tile_rows(r), :], sem).start()

    def scatter_row(self, blk, r, ybuf, y_hbm, sem):
        pltpu.make_async_copy(ybuf.at[_tile_rows(r), :], y_hbm.at[_tile_rows(self.dst_ref[blk, r]), :],
                              sem).start()

    def all_rows(self, row_fn):
        def body(g, c):
            for j in range(SUBLANES):
                row_fn(g * SUBLANES + j)
            return c

        lax.fori_loop(0, MOE_TM // SUBLANES, body, 0)

    @staticmethod
    def wait_gather(h_hbm, xbuf, sem):
        pltpu.make_async_copy(h_hbm.at[pl.ds(0, MOE_TM * ROW_TILE), :], xbuf, sem).wait()

    @staticmethod
    def wait_scatter(ybuf, y_hbm, sem):
        pltpu.make_async_copy(ybuf, y_hbm.at[pl.ds(0, MOE_TM * ROW_TILE), :], sem).wait()


def _expert_kernel(be_ref, used_ref, dst_ref, h_hbm, wg_ref, wu_ref, wd_ref, y_hbm,
                   xbuf, ybuf, gsem, ssem, *, n_tokens):
    i = pl.program_id(0)
    used = used_ref[0]
    slot = i % 2
    rows = _BlockRows(dst_ref, n_tokens)
    pieces = SWIGLU_DOTS - 2
    per_piece = MOE_TM // pieces

    def compute(scatter_previous):
        _BlockRows.wait_gather(h_hbm, xbuf.at[slot], gsem.at[slot])
        x = _load_token_tiles(xbuf.at[slot], MOE_TM).astype(BF16)
        nxt = jnp.minimum(i + 1, used - 1)

        def start_some(k):
            if k < pieces:
                for r in range(k * per_piece, (k + 1) * per_piece):
                    rows.gather_row(nxt, r, h_hbm, xbuf.at[1 - slot], gsem.at[1 - slot])
                    if scatter_previous:
                        rows.scatter_row(i - 1, r, ybuf.at[1 - slot], y_hbm, ssem.at[0])

        _store_token_tiles(ybuf.at[slot], _swiglu_tile(x, wg_ref, wu_ref, wd_ref, after_dot=start_some))

    @pl.when(i == 0)
    def _():
        rows.all_rows(lambda r: rows.gather_row(0, r, h_hbm, xbuf.at[0], gsem.at[0]))
        ybuf[1] = jnp.zeros((MOE_TM * ROW_TILE, LANES), F32)
        clear = pltpu.make_async_copy(
            ybuf.at[1], y_hbm.at[pl.ds(n_tokens * TOP_K * ROW_TILE, MOE_TM * ROW_TILE), :], ssem.at[0])
        clear.start()
        clear.wait()
        compute(False)

    @pl.when(jnp.logical_and(i >= 2, i <= used))
    def _():
        _BlockRows.wait_scatter(ybuf.at[slot], y_hbm, ssem.at[0])

    @pl.when(jnp.logical_and(i >= 1, i < used))
    def _():
        compute(True)

    @pl.when(i == used)
    def _():
        _BlockRows.wait_gather(h_hbm, xbuf.at[slot], gsem.at[slot])
        rows.all_rows(lambda r: rows.scatter_row(i - 1, r, ybuf.at[1 - slot], y_hbm, ssem.at[0]))
        _BlockRows.wait_scatter(ybuf.at[1 - slot], y_hbm, ssem.at[0])


def _experts(h, block_expert, used, row_dst, wg, wu, wd, n_tokens):
    d = D_MODEL
    n_blocks = row_dst.shape[0]
    f = wg.shape[-1]
    assert n_tokens * TOP_K >= 2 * MOE_TM
    one = pl.Buffered(1)
    grid_spec = pltpu.PrefetchScalarGridSpec(
        num_scalar_prefetch=3,
        grid=(n_blocks,),
        in_specs=[pl.BlockSpec(memory_space=pl.ANY),
                  pl.BlockSpec((None, d, f), lambda i, be, us, ds: (be[i], 0, 0), pipeline_mode=one),
                  pl.BlockSpec((None, d, f), lambda i, be, us, ds: (be[i], 0, 0), pipeline_mode=one),
                  pl.BlockSpec((None, f, d), lambda i, be, us, ds: (be[i], 0, 0), pipeline_mode=one)],
        out_specs=pl.BlockSpec(memory_space=pl.ANY),
        scratch_shapes=[pltpu.VMEM((2, MOE_TM * ROW_TILE, LANES), F32),
                        pltpu.VMEM((2, MOE_TM * ROW_TILE, LANES), F32),
                        pltpu.SemaphoreType.DMA((2,)), pltpu.SemaphoreType.DMA((1,))],
    )
    return pl.pallas_call(
        functools.partial(_expert_kernel, n_tokens=n_tokens),
        grid_spec=grid_spec,
        out_shape=jax.ShapeDtypeStruct(((n_tokens * TOP_K + MOE_TM) * ROW_TILE, LANES), F32),
        compiler_params=_cparams("arbitrary"),
        name="expert_swiglu",
    )(block_expert, used, row_dst, h, wg, wu, wd)


def _combine_kernel(y_ref, x_ref, route_ref, o_ref):
    tt = x_ref.shape[0]
    route = route_ref[...]
    out = x_ref[...]
    for k in range(TOP_K):
        yk = jnp.concatenate([y_ref[pl.ds(k * ROW_TILE + c, tt, stride=TOP_K * ROW_TILE), :]
                              for c in range(ROW_TILE)], axis=-1)
        out = out + route[:, TOP_K + k:TOP_K + k + 1] * yk
    o_ref[...] = out


def _combine(x, ys, route):
    n, d = x.shape
    tt = COMBINE_TT
    return pl.pallas_call(
        _combine_kernel,
        grid=(n // tt,),
        in_specs=[pl.BlockSpec((tt * TOP_K * ROW_TILE, LANES), lambda i: (i, 0)),
                  pl.BlockSpec((tt, d), lambda i: (i, 0)),
                  pl.BlockSpec((tt, LANES), lambda i: (i, 0))],
        out_specs=pl.BlockSpec((tt, d), lambda i: (i, 0)),
        out_shape=jax.ShapeDtypeStruct((n, d), F32),
        compiler_params=_cparams("arbitrary"),
        name="expert_combine",
    )(ys, x, route)


def _routed_ffn(x, g, w_router, b_router, wg, wu, wd, *, tm):
    n, d = x.shape
    h, route = _router(x, g, w_router, b_router, tm=tm)
    e_flat = route[:, :TOP_K].astype(jnp.int32).reshape(-1)
    onehot = (e_flat[:, None] == jnp.arange(N_EXPERTS)[None, :]).astype(jnp.int32)
    csum = jnp.cumsum(onehot, axis=0)
    counts = csum[-1]
    rank = jnp.sum((csum - onehot) * onehot, axis=1)
    padded = (counts + MOE_TM - 1) // MOE_TM * MOE_TM
    pend = jnp.cumsum(padded)
    pstart = pend - padded
    dest = (pstart[e_flat] + rank).astype(jnp.int32)
    n_blocks = (n * TOP_K) // MOE_TM + N_EXPERTS
    scratch_rows = n * TOP_K + jnp.arange(n_blocks * MOE_TM, dtype=jnp.int32) % MOE_TM
    row_dst = scratch_rows.at[dest].set(jnp.arange(n * TOP_K, dtype=jnp.int32)).reshape(n_blocks, MOE_TM)
    block_expert = jnp.clip(jnp.searchsorted(pend, jnp.arange(n_blocks) * MOE_TM, side='right'),
                            0, N_EXPERTS - 1).astype(jnp.int32)
    used = (pend[-1:] // MOE_TM).astype(jnp.int32)
    ys = _experts(h, block_expert, used, row_dst, wg, wu, wd, n)
    return _combine(x, ys, route)


def kernel(x, attn_norm_g, w_in, q_norm_g, k_norm_g, conv_w, conv_b, conv_norm_g, conv_norm_b,
           w_branch_a, w_branch_b, w_branch_c, w_out, ffn_norm_g, w_ffn_gate, w_ffn_up,
           w_ffn_down, w_router, b_router, w_exp_gate, w_exp_up, w_exp_down):
    bsz, seq, d = x.shape
    depth = attn_norm_g.shape[0]
    n = bsz * seq
    tm = 512
    xf = x.reshape(n, d)
    s_dil, s_sb, s_glu = 3 * DIL_WIDTH, 3 * DIL_WIDTH + 3 * SB_WIDTH, 3 * DIL_WIDTH + 3 * SB_WIDTH + 2 * CONV_CH
    for layer in range(depth):
        wl = w_in[layer]
        w_perm = jnp.concatenate([wl[:, s_glu:], wl[:, s_sb:s_glu], wl[:, s_dil:s_sb], wl[:, :s_dil]],
                                 axis=1).astype(BF16)
        proj = _norm_matmul(xf, attn_norm_g[layer], w_perm, tm=tm, tn=IN_COLS // 2)
        o_groups, lse_groups = [], []
        for gi, (window, dilation) in enumerate(DIL_GROUPS):
            o_g, lse_g = _dil_attention(proj, q_norm_g[layer], k_norm_g[layer], gi, window, dilation,
                                        bsz, seq)
            o_groups.append(o_g)
            lse_groups.append(lse_g)
        o_b = _sb_attention(proj, bsz, seq)
        o_c = _conformer_conv(proj, conv_w[layer], conv_b[layer], conv_norm_g[layer],
                              conv_norm_b[layer], bsz, seq)
        xf = _merge(o_groups, lse_groups, o_b, o_c, proj, xf,
                    w_branch_a[layer].astype(BF16), w_branch_b[layer].astype(BF16),
                    w_branch_c[layer].astype(BF16), w_out[layer].astype(BF16), tm=tm)
        i = layer // 2
        if layer % 2 == 0:
            xf = _dense_ffn(xf, ffn_norm_g[layer], w_ffn_gate[i].astype(BF16), w_ffn_up[i].astype(BF16),
                            w_ffn_down[i].astype(BF16), tm=tm)
        else:
            xf = _routed_ffn(xf, ffn_norm_g[layer], w_router[i], b_router[i],
                             w_exp_gate[i].astype(BF16), w_exp_up[i].astype(BF16),
                             w_exp_down[i].astype(BF16), tm=tm)
    return xf.reshape(bsz, seq, d)
```

```python
import functools
import math

import jax
import jax.numpy as jnp
from jax import lax
from jax.experimental import pallas as pl
from jax.experimental.pallas import tpu as pltpu

F32 = jnp.float32
BF16 = jnp.bfloat16

D_MODEL = 1024
HEAD_DIM = 64
DIL_GROUPS = ((128, 1), (512, 4), (2048, 16))
DIL_HEADS_PER_GROUP = 4
DIL_HEADS = len(DIL_GROUPS) * DIL_HEADS_PER_GROUP
DIL_WIDTH = DIL_HEADS * HEAD_DIM
DIL_OUT = DIL_HEADS_PER_GROUP * HEAD_DIM
DIL_BLOCK = 128
SB_HEADS = 8
SB_WIDTH = SB_HEADS * HEAD_DIM
CONV_CH = D_MODEL // 2
CONV_WIDTH = 31
N_BRANCH = 3
IN_COLS = 3 * DIL_WIDTH + 3 * SB_WIDTH + 2 * CONV_CH + N_BRANCH * D_MODEL
D_FF = 2816
N_EXPERTS = 8
TOP_K = 2
EPS = 1e-6
ALIBI_MAX_BIAS = 8.0
NEG_BIG = -1e30

COL_GATES = 0
COL_GLU = COL_GATES + N_BRANCH * D_MODEL
COL_SB = COL_GLU + 2 * CONV_CH
COL_DIL = COL_SB + 3 * SB_WIDTH

LANES = 128
SUBLANES = 8
VMEM_LIMIT = 56 * 1024 * 1024

DIL_UNITS = 4
SB_TQ = 512
SB_TK = 128
SB_DEAD_RUN = 128.0
CONV_ROWS = 64
CONV_PAD = 32
FF_CHUNK = 1408
MOE_TM = 512
COMBINE_TT = 256


def _cparams(*sem):
    return pltpu.CompilerParams(dimension_semantics=sem, vmem_limit_bytes=VMEM_LIMIT)


def _sigmoid(x):
    return 1.0 / (1.0 + jnp.exp(-x))


def _norm_matmul_kernel(x_ref, g_ref, w_ref, o_ref):
    x = x_ref[...]
    ms = jnp.mean(x * x, axis=-1, keepdims=True)
    h = (x * lax.rsqrt(ms + EPS)) * g_ref[...]
    o_ref[...] = jnp.dot(h.astype(BF16), w_ref[...], preferred_element_type=F32).astype(o_ref.dtype)


def _norm_matmul(x, g, w, *, tm, tn):
    n, d = x.shape
    e = w.shape[1]
    return pl.pallas_call(
        _norm_matmul_kernel,
        grid=(e // tn, n // tm),
        in_specs=[pl.BlockSpec((tm, d), lambda j, i: (i, 0)),
                  pl.BlockSpec((1, d), lambda j, i: (0, 0)),
                  pl.BlockSpec((d, tn), lambda j, i: (0, j))],
        out_specs=pl.BlockSpec((tm, tn), lambda j, i: (i, j)),
        out_shape=jax.ShapeDtypeStruct((n, e), BF16),
        compiler_params=_cparams("arbitrary", "arbitrary"),
        name="norm_in_proj",
    )(x, g.reshape(1, d), w)


def _same_head_matrix(w):
    r = lax.broadcasted_iota(jnp.int32, (w, w), 0) // HEAD_DIM
    c = lax.broadcasted_iota(jnp.int32, (w, w), 1) // HEAD_DIM
    return (r == c).astype(BF16)


def _head_rms_scale(t, same_head):
    sq = t * t
    hi = sq.astype(BF16)
    lo = (sq - hi.astype(F32)).astype(BF16)
    ssq = (jnp.dot(hi, same_head, preferred_element_type=F32)
           + jnp.dot(lo, same_head, preferred_element_type=F32))
    return lax.rsqrt(ssq * (1.0 / HEAD_DIM) + EPS)


def _dil_attn_kernel(q_ref, k_ref, v_ref, qg_ref, kg_ref, bias_ref, o_ref, lse_ref,
                     qn_scr, kn_scr, v_scr, o_scr, *, seq, dilation):
    blk, d = DIL_BLOCK, dilation
    span = blk * d
    log2d = d.bit_length() - 1
    same_head = _same_head_matrix(DIL_OUT)
    qg = qg_ref[...] * (1.0 / math.sqrt(HEAD_DIM))
    kg = kg_ref[...]
    chunk = 256

    pairs = DIL_OUT // LANES
    pair_lanes = [slice(p * LANES, (p + 1) * LANES) for p in range(pairs)]
    for p in range(pairs):
        kn_scr[p, pl.ds(0, span), :] = jnp.zeros((span, LANES), F32)
        v_scr[p, pl.ds(0, span), :] = jnp.zeros((span, LANES), F32)

    def norm(i, c):
        rows = pl.ds(pl.multiple_of(i * chunk, chunk), chunk)
        prows = pl.ds(pl.multiple_of(span + i * chunk, blk), chunk)
        q = q_ref[0, rows, :].astype(F32)
        qn = q * _head_rms_scale(q, same_head) * qg
        k = k_ref[0, rows, :].astype(F32)
        kn = k * _head_rms_scale(k, same_head) * kg
        v = v_ref[0, rows, :].astype(F32)
        for p in range(pairs):
            qn_scr[p, rows, :] = qn[:, pair_lanes[p]]
            kn_scr[p, prows, :] = kn[:, pair_lanes[p]]
            v_scr[p, prows, :] = v[:, pair_lanes[p]]
        return c

    lax.fori_loop(0, seq // chunk, norm, 0)

    low_half = lax.broadcasted_iota(jnp.int32, (blk, LANES), 1) < HEAD_DIM

    def sub_rows(base, count):
        return pl.ds(pl.multiple_of(base, blk), count) if d == 1 else pl.ds(base, count, stride=d)

    def load(u):
        if d == 1:
            n, base = u, u * span
        else:
            n = u >> log2d
            base = (u & (d - 1)) + n * span
        q = [qn_scr[p, sub_rows(base, blk), :] for p in range(pairs)]
        kk = [kn_scr[p, sub_rows(base, 2 * blk), :].astype(BF16) for p in range(pairs)]
        vv = [v_scr[p, sub_rows(base, 2 * blk), :].astype(BF16) for p in range(pairs)]
        first = jnp.where(n == 0, 1, 0)
        return base, q, kk, vv, first

    def scores(unit):
        _, q, kk, _, first = unit
        out = []
        for h in range(DIL_HEADS_PER_GROUP):
            keep = low_half if h % 2 == 0 else jnp.logical_not(low_half)
            qh = jnp.where(keep, q[h // 2], 0.0).astype(BF16)
            s = lax.dot_general(qh, kk[h // 2], (((1,), (1,)), ((), ())), preferred_element_type=F32)
            out.append(s + bias_ref[h, first])
        return out

    def finish(unit, s_list):
        base, _, _, vv, _ = unit
        rows = sub_rows(base, blk)
        for p in range(pairs):
            oh, lh = [], []
            for hh in range(2):
                s = s_list[2 * p + hh]
                m = jnp.max(s, axis=-1, keepdims=True)
                e = jnp.exp(s - m)
                den = jnp.sum(e, axis=-1, keepdims=True)
                oh.append(jnp.dot(e.astype(BF16), vv[p], preferred_element_type=F32) * (1.0 / den))
                lh.append(m + jnp.log(den))
            o_scr[p, rows, :] = jnp.where(low_half, oh[0], oh[1])
            lse_ref[0, p, rows, :] = jnp.where(low_half, lh[0], lh[1])

    def body(it, c):
        units = [load(it * DIL_UNITS + i) for i in range(DIL_UNITS)]
        s = [None] * DIL_UNITS
        for step in range(DIL_UNITS + 1):
            if step < DIL_UNITS:
                s[step] = scores(units[step])
            if step >= 1:
                finish(units[step - 1], s[step - 1])
        return c

    lax.fori_loop(0, seq // blk // DIL_UNITS, body, 0)

    def emit(i, c):
        rows = pl.ds(pl.multiple_of(i * chunk, chunk), chunk)
        o_ref[0, rows, :] = jnp.concatenate([o_scr[p, rows, :] for p in range(pairs)],
                                            axis=-1).astype(o_ref.dtype)
        return c

    lax.fori_loop(0, seq // chunk, emit, 0)


def _dil_bias_table(group, window, dilation):
    reach = window // dilation
    assert reach <= DIL_BLOCK
    slopes = 2.0 ** (-ALIBI_MAX_BIAS * jnp.arange(1, DIL_HEADS + 1, dtype=F32) / DIL_HEADS)
    slopes = slopes[group * DIL_HEADS_PER_GROUP:(group + 1) * DIL_HEADS_PER_GROUP]
    qi = jnp.arange(DIL_BLOCK)[:, None] + DIL_BLOCK
    ki = jnp.arange(2 * DIL_BLOCK)[None, :]
    dist = qi - ki
    valid = (dist >= 0) & (dist <= reach)
    bias = -slopes[:, None, None] * (dist * dilation).astype(F32)[None]
    general = jnp.where(valid[None], bias, NEG_BIG)
    first = jnp.where((valid & (ki >= DIL_BLOCK))[None], bias, NEG_BIG)
    return jnp.stack([general, first], axis=1)


def _dil_attention(proj, q_gain, k_gain, group, window, dilation, bsz, seq):
    e = proj.shape[-1]
    assert (seq // DIL_BLOCK) % DIL_UNITS == 0
    assert seq % (DIL_BLOCK * dilation) == 0 and dilation & (dilation - 1) == 0
    view = proj.reshape(bsz, seq, e)
    w = DIL_OUT
    pairs = w // LANES
    pad = DIL_BLOCK * dilation

    def col(base):
        off = (base + group * w) // w
        return lambda b: (b, 0, off)

    bias = _dil_bias_table(group, window, dilation)
    gain = lambda g: jnp.tile(g, DIL_HEADS_PER_GROUP).reshape(1, w)
    o, lse = pl.pallas_call(
        functools.partial(_dil_attn_kernel, seq=seq, dilation=dilation),
        grid=(bsz,),
        in_specs=[pl.BlockSpec((1, seq, w), col(COL_DIL)),
                  pl.BlockSpec((1, seq, w), col(COL_DIL + DIL_WIDTH)),
                  pl.BlockSpec((1, seq, w), col(COL_DIL + 2 * DIL_WIDTH)),
                  pl.BlockSpec((1, w), lambda b: (0, 0)),
                  pl.BlockSpec((1, w), lambda b: (0, 0)),
                  pl.BlockSpec(bias.shape, lambda b: (0, 0, 0, 0))],
        out_specs=[pl.BlockSpec((1, seq, w), lambda b: (b, 0, 0)),
                   pl.BlockSpec((1, pairs, seq, LANES), lambda b: (b, 0, 0, 0))],
        out_shape=[jax.ShapeDtypeStruct((bsz, seq, w), BF16),
                   jax.ShapeDtypeStruct((bsz, pairs, seq, LANES), F32)],
        scratch_shapes=[pltpu.VMEM((pairs, seq, LANES), F32), pltpu.VMEM((pairs, pad + seq, LANES), F32),
                        pltpu.VMEM((pairs, pad + seq, LANES), F32), pltpu.VMEM((pairs, seq, LANES), F32)],
        compiler_params=_cparams("arbitrary"),
        name=f"dilated_attn_g{group}",
    )(view, view, view, gain(q_gain), gain(k_gain), bias)
    return o.reshape(bsz * seq, w), lse


def _sb_attn_kernel(q_ref, k_ref, v_ref, o_ref, vcat_scr, acc_scr, *, seq):
    tq, tk, nblk = SB_TQ, SB_TK, SB_TQ // SB_TK
    qi = pl.program_id(2)

    @pl.when(qi == 0)
    def _():
        chan = lax.broadcasted_iota(jnp.int32, (LANES, tk), 0)

        def build(kb, c):
            vt = v_ref[0, pl.ds(pl.multiple_of(kb * tk, tk), tk), :].astype(F32).T
            vcat_scr[kb] = jnp.concatenate(
                [jnp.where(chan < HEAD_DIM, vt, 0.0), jnp.where(chan >= HEAD_DIM, vt, 0.0)],
                axis=1).astype(BF16)
            return c

        lax.fori_loop(0, seq // tk, build, 0)

    lane = lax.broadcasted_iota(jnp.int32, (tq, LANES), 1)
    q = q_ref[0] * (1.0 / math.sqrt(HEAD_DIM))
    zero = jnp.zeros_like(q)
    qcat = jnp.concatenate([jnp.where(lane < HEAD_DIM, q, zero), jnp.where(lane >= HEAD_DIM, q, zero)],
                           axis=0)
    qcat_t = qcat.astype(F32).T.astype(BF16)
    neg_tri = jnp.where(lax.broadcasted_iota(jnp.int32, (tk, tk), 1)
                        >= lax.broadcasted_iota(jnp.int32, (tk, tk), 0), -1.0, 0.0).astype(BF16)
    acc_scr[...] = jnp.zeros_like(acc_scr)

    def scores(kb, rel):
        kk = k_ref[0, pl.ds(pl.multiple_of(kb * tk, tk), tk), :]
        zt = lax.dot_general(kk, qcat, (((1,), (1,)), ((), ())), preferred_element_type=F32)
        neg_abs = lax.bitcast_convert_type(
            lax.bitcast_convert_type(zt, jnp.uint32) | jnp.uint32(0x80000000), F32)
        sp = jnp.maximum(zt, 0.0) + jnp.log(1.0 + jnp.exp(neg_abs))
        before = None
        if rel is not None:
            kpos = lax.broadcasted_iota(jnp.int32, (tk, 2 * tq), 0) + rel * tk
            qpos = lax.broadcasted_iota(jnp.int32, (tk, 2 * tq), 1) & (tq - 1)
            before = kpos < qpos
            sp = jnp.where(before, sp, 0.0)
        return kk, zt[0:1, :], sp.astype(BF16), before

    def weights(state, run):
        kk, zt0, sp, before = state
        arg = jnp.dot(jnp.concatenate([neg_tri, kk], axis=1),
                      jnp.concatenate([sp, qcat_t], axis=0), preferred_element_type=F32)
        a = jnp.exp(arg - run)
        if before is not None:
            a = jnp.where(before, a, 0.0)
        a = a.astype(BF16)
        acat = jnp.concatenate([a[:, :tq], a[:, tq:]], axis=0)
        return acat, run + (zt0 - arg[0:1, :])

    def values(kb, acat):
        acc_scr[...] += jnp.dot(vcat_scr[kb], acat, preferred_element_type=F32)

    def run_blocks(blocks, run):
        n = len(blocks)
        st, ac = [None] * n, [None] * n
        for step in range(n + 2):
            if step < n:
                st[step] = scores(*blocks[step])
            if 0 <= step - 1 < n:
                ac[step - 1], run = weights(st[step - 1], run)
            if 0 <= step - 2 < n:
                values(blocks[step - 2][0], ac[step - 2])
        return run

    run = jnp.zeros((1, 2 * tq), F32)
    run = run_blocks([(qi * nblk + rel, rel) for rel in reversed(range(nblk))], run)

    def alive(run):
        return (jnp.min(run) < SB_DEAD_RUN).astype(jnp.int32)

    def more(carry):
        it, _, go = carry
        return jnp.logical_and(it < qi, go > 0)

    def chunk(carry):
        it, run, _ = carry
        base = (qi - 1 - it) * nblk
        run = run_blocks([(base + j, None) for j in reversed(range(nblk))], run)
        return it + 1, run, alive(run)

    lax.while_loop(more, chunk, (jnp.int32(0), run, alive(run)))
    o_ref[0] = acc_scr[...].T.astype(o_ref.dtype)


def _sb_attention(proj, bsz, seq):
    e = proj.shape[-1]
    view = proj.reshape(bsz, seq, e)
    pairs = SB_WIDTH // LANES
    qo, ko, vo = COL_SB // LANES, (COL_SB + SB_WIDTH) // LANES, (COL_SB + 2 * SB_WIDTH) // LANES
    out = pl.pallas_call(
        functools.partial(_sb_attn_kernel, seq=seq),
        grid=(bsz, pairs, seq // SB_TQ),
        in_specs=[pl.BlockSpec((1, SB_TQ, LANES), lambda b, p, i: (b, i, qo + p)),
                  pl.BlockSpec((1, seq, LANES), lambda b, p, i: (b, 0, ko + p)),
                  pl.BlockSpec((1, seq, LANES), lambda b, p, i: (b, 0, vo + p))],
        out_specs=pl.BlockSpec((1, SB_TQ, LANES), lambda b, p, i: (b, i, p)),
        out_shape=jax.ShapeDtypeStruct((bsz, seq, SB_WIDTH), BF16),
        scratch_shapes=[pltpu.VMEM((seq // SB_TK, LANES, 2 * SB_TK), BF16),
                        pltpu.VMEM((LANES, SB_TQ), F32)],
        compiler_params=_cparams("arbitrary", "arbitrary", "arbitrary"),
        name="stick_breaking_attn",
    )(view, view, view)
    return out.reshape(bsz * seq, SB_WIDTH)


def _conv_kernel(val_ref, gate_ref, w_ref, b_ref, g_ref, beta_ref, o_ref, u_scr, sh_scr, *, seq):
    tr, pad, half = CONV_ROWS, CONV_PAD, CONV_CH // 2
    u_scr[pl.ds(0, pad), :] = jnp.zeros((pad, CONV_CH), F32)

    def glu(i, c):
        rows = pl.ds(pl.multiple_of(i * 256, 256), 256)
        val = val_ref[0, rows, :].astype(F32)
        gate = gate_ref[0, rows, :].astype(F32)
        u_scr[pl.ds(pl.multiple_of(pad + i * 256, SUBLANES), 256), :] = val * _sigmoid(gate)
        return c

    lax.fori_loop(0, seq // 256, glu, 0)

    def tile(i, c):
        t0 = pl.multiple_of(i * tr, tr)
        parts = []
        for ch in range(2):
            cs = slice(ch * half, (ch + 1) * half)
            win = u_scr[pl.ds(t0, tr + pad), cs]
            acc = jnp.zeros((tr, half), F32)
            first = pad - (CONV_WIDTH - 1)
            for r in range(SUBLANES):
                offs = [o for o in range(first, first + CONV_WIDTH) if o % SUBLANES == r]
                n_rows = offs[-1] - r + tr
                sh_scr[r, pl.ds(0, n_rows), :] = win[r:r + n_rows, :]
                for o in offs:
                    w = o - first
                    acc = acc + sh_scr[r, pl.ds(o - r, tr), :] * w_ref[w:w + 1, cs]
            parts.append(acc)
        y = jnp.concatenate(parts, axis=-1) + b_ref[...]
        mu = jnp.mean(y, axis=-1, keepdims=True)
        yc = y - mu
        var = jnp.mean(yc * yc, axis=-1, keepdims=True)
        yn = yc * lax.rsqrt(var + EPS) * g_ref[...] + beta_ref[...]
        o_ref[0, pl.ds(t0, tr), :] = (yn * _sigmoid(yn)).astype(o_ref.dtype)
        return c

    lax.fori_loop(0, seq // tr, tile, 0)


def _conformer_conv(proj, conv_w, conv_b, norm_g, norm_b, bsz, seq):
    e = proj.shape[-1]
    view = proj.reshape(bsz, seq, e)
    c = CONV_CH
    voff, goff = COL_GLU // c, (COL_GLU + c) // c
    const = lambda b: (0, 0)
    out = pl.pallas_call(
        functools.partial(_conv_kernel, seq=seq),
        grid=(bsz,),
        in_specs=[pl.BlockSpec((1, seq, c), lambda b: (b, 0, voff)),
                  pl.BlockSpec((1, seq, c), lambda b: (b, 0, goff)),
                  pl.BlockSpec((CONV_WIDTH, c), const),
                  pl.BlockSpec((1, c), const), pl.BlockSpec((1, c), const), pl.BlockSpec((1, c), const)],
        out_specs=pl.BlockSpec((1, seq, c), lambda b: (b, 0, 0)),
        out_shape=jax.ShapeDtypeStruct((bsz, seq, c), BF16),
        scratch_shapes=[pltpu.VMEM((seq + CONV_PAD, c), F32),
                        pltpu.VMEM((SUBLANES, CONV_ROWS + CONV_PAD, c // 2), F32)],
        compiler_params=_cparams("arbitrary"),
        name="conformer_conv",
    )(view, view, conv_w, conv_b.reshape(1, c), norm_g.reshape(1, c), norm_b.reshape(1, c))
    return out.reshape(bsz * seq, c)


def _merge_kernel(o1_ref, o2_ref, o3_ref, l1_ref, l2_ref, l3_ref, ob_ref, oc_ref,
                  ga_ref, gb_ref, gc_ref, x_ref, wa_ref, wb_ref, wc_ref, wo_ref, out_ref):
    by_lanes = lambda ref: jnp.concatenate([ref[p] for p in range(ref.shape[0])], axis=-1)
    l1, l2, l3 = by_lanes(l1_ref), by_lanes(l2_ref), by_lanes(l3_ref)
    m = jnp.maximum(jnp.maximum(l1, l2), l3)
    e1, e2, e3 = jnp.exp(l1 - m), jnp.exp(l2 - m), jnp.exp(l3 - m)
    o_a = (e1 * o1_ref[...].astype(F32) + e2 * o2_ref[...].astype(F32)
           + e3 * o3_ref[...].astype(F32)) / (e1 + e2 + e3)
    ya = jnp.dot(o_a.astype(BF16), wa_ref[...], preferred_element_type=F32)
    yb = jnp.dot(ob_ref[...], wb_ref[...], preferred_element_type=F32)
    yc = jnp.dot(oc_ref[...], wc_ref[...], preferred_element_type=F32)
    merged = (_sigmoid(ga_ref[...].astype(F32)) * ya + _sigmoid(gb_ref[...].astype(F32)) * yb
              + _sigmoid(gc_ref[...].astype(F32)) * yc)
    out_ref[...] = x_ref[...] + jnp.dot(merged.astype(BF16), wo_ref[...], preferred_element_type=F32)


def _merge(o_groups, lse_groups, o_b, o_c, proj, x, wa, wb, wc, wo, *, tm):
    n, d = x.shape
    row = lambda w, j=0: pl.BlockSpec((tm, w), lambda i: (i, j))
    full = lambda a: pl.BlockSpec(a.shape, lambda i: (0, 0), pipeline_mode=pl.Buffered(1))
    g0 = COL_GATES // d
    _, pairs, seq, _ = lse_groups[0].shape
    per_batch = seq // tm
    lse = pl.BlockSpec((None, pairs, tm, LANES), lambda i: (i // per_batch, 0, i % per_batch, 0))
    return pl.pallas_call(
        _merge_kernel,
        grid=(n // tm,),
        in_specs=[row(DIL_OUT), row(DIL_OUT), row(DIL_OUT), lse, lse, lse,
                  row(SB_WIDTH), row(CONV_CH), row(d, g0), row(d, g0 + 1), row(d, g0 + 2), row(d),
                  full(wa), full(wb), full(wc), full(wo)],
        out_specs=row(d),
        out_shape=jax.ShapeDtypeStruct((n, d), F32),
        compiler_params=_cparams("arbitrary"),
        name="branch_merge_out_proj",
    )(*o_groups, *lse_groups, o_b, o_c, proj, proj, proj, x, wa, wb, wc, wo)


MXU_COLS = 256


def _col_pieces(lo, hi):
    return [(p, min(p + MXU_COLS, hi)) for p in range(lo, hi, MXU_COLS)]


SWIGLU_HOOKS = (D_FF // FF_CHUNK) * (2 * len(_col_pieces(0, FF_CHUNK)) + len(_col_pieces(0, D_MODEL)))


def _swiglu_tile(h, wg_ref, wu_ref, wd_ref, hook=None):
    def matmul(x, w_ref, rows, lo, hi):
        if hook is None:
            return jnp.dot(x, w_ref[rows, lo:hi], preferred_element_type=F32)
        pieces = []
        for p, q in _col_pieces(lo, hi):
            pieces.append(jnp.dot(x, w_ref[rows, p:q], preferred_element_type=F32))
            hook()
        return jnp.concatenate(pieces, axis=-1)

    y = None
    for c in range(D_FF // FF_CHUNK):
        lo, hi = c * FF_CHUNK, (c + 1) * FF_CHUNK
        a = matmul(h, wg_ref, slice(None), lo, hi)
        u = matmul(h, wu_ref, slice(None), lo, hi)
        act = (a * _sigmoid(a) * u).astype(BF16)
        part = matmul(act, wd_ref, slice(lo, hi), 0, D_MODEL)
        y = part if y is None else y + part
    return y


def _dense_ffn_kernel(x_ref, g_ref, wg_ref, wu_ref, wd_ref, o_ref):
    x = x_ref[...]
    ms = jnp.mean(x * x, axis=-1, keepdims=True)
    h = ((x * lax.rsqrt(ms + EPS)) * g_ref[...]).astype(BF16)
    o_ref[...] = x + _swiglu_tile(h, wg_ref, wu_ref, wd_ref)


def _dense_ffn(x, g, wg, wu, wd, *, tm):
    n, d = x.shape
    full = lambda a: pl.BlockSpec(a.shape, lambda i: (0, 0), pipeline_mode=pl.Buffered(1))
    return pl.pallas_call(
        _dense_ffn_kernel,
        grid=(n // tm,),
        in_specs=[pl.BlockSpec((tm, d), lambda i: (i, 0)), pl.BlockSpec((1, d), lambda i: (0, 0)),
                  full(wg), full(wu), full(wd)],
        out_specs=pl.BlockSpec((tm, d), lambda i: (i, 0)),
        out_shape=jax.ShapeDtypeStruct((n, d), F32),
        compiler_params=_cparams("arbitrary"),
        name="dense_swiglu",
    )(x, g.reshape(1, d), wg, wu, wd)


def _router_kernel(x_ref, g_ref, wr_ref, br_ref, h_ref, route_ref):
    x = x_ref[...]
    ms = jnp.mean(x * x, axis=-1, keepdims=True)
    h = (x * lax.rsqrt(ms + EPS)) * g_ref[...]
    _store_token_tiles(h_ref, h)
    logits = jnp.dot(h, wr_ref[...], preferred_element_type=F32,
                     precision=lax.Precision.HIGHEST) + br_ref[...]
    lane = lax.broadcasted_iota(jnp.int32, logits.shape, 1)
    m1 = jnp.max(logits, axis=-1, keepdims=True)
    i1 = jnp.min(jnp.where(logits == m1, lane, LANES), axis=-1, keepdims=True)
    rest = jnp.where(lane == i1, -jnp.inf, logits)
    m2 = jnp.max(rest, axis=-1, keepdims=True)
    i2 = jnp.min(jnp.where(rest == m2, lane, LANES), axis=-1, keepdims=True)
    e2 = jnp.exp(m2 - m1)
    g1 = 1.0 / (1.0 + e2)
    g2 = e2 / (1.0 + e2)
    route = jnp.where(lane == 0, i1.astype(F32),
                      jnp.where(lane == 1, i2.astype(F32),
                                jnp.where(lane == 2, g1, jnp.where(lane == 3, g2, 0.0))))
    route_ref[...] = route


def _router(x, g, w_router, b_router, *, tm):
    n, d = x.shape
    wr = jnp.zeros((d, LANES), F32).at[:, :N_EXPERTS].set(w_router)
    br = jnp.full((1, LANES), NEG_BIG, F32).at[0, :N_EXPERTS].set(b_router)
    return pl.pallas_call(
        _router_kernel,
        grid=(n // tm,),
        in_specs=[pl.BlockSpec((tm, d), lambda i: (i, 0)), pl.BlockSpec((1, d), lambda i: (0, 0)),
                  pl.BlockSpec((d, LANES), lambda i: (0, 0)), pl.BlockSpec((1, LANES), lambda i: (0, 0))],
        out_specs=[pl.BlockSpec((tm * ROW_TILE, LANES), lambda i: (i, 0)),
                   pl.BlockSpec((tm, LANES), lambda i: (i, 0))],
        out_shape=[jax.ShapeDtypeStruct((n * ROW_TILE, LANES), F32), jax.ShapeDtypeStruct((n, LANES), F32)],
        compiler_params=_cparams("arbitrary"),
        name="router_top2",
    )(x, g.reshape(1, d), wr, br)


ROW_TILE = D_MODEL // LANES


def _store_token_tiles(ref, x):
    rows = x.shape[0]
    for c in range(ROW_TILE):
        ref[pl.ds(c, rows, stride=ROW_TILE), :] = x[:, c * LANES:(c + 1) * LANES]


def _load_token_tiles(ref, rows):
    return jnp.concatenate([ref[pl.ds(c, rows, stride=ROW_TILE), :] for c in range(ROW_TILE)], axis=-1)


def _tile_rows(t):
    start = t * ROW_TILE
    return pl.ds(start if isinstance(start, int) else pl.multiple_of(start, ROW_TILE), ROW_TILE)


class _BlockRows:
    def __init__(self, dst_ref, n_tokens):
        self.dst_ref = dst_ref
        self.last_token = n_tokens - 1

    def gather_row(self, blk, r, h_hbm, xbuf, sem):
        tok = lax.shift_right_logical(self.dst_ref[blk, r], TOP_K.bit_length() - 1)
        tok = jnp.minimum(tok, self.last_token)
        pltpu.make_async_copy(h_hbm.at[_tile_rows(tok), :], xbuf.at[_tile_rows(r), :],
                              sem).start(priority=self._queue(r))

    def scatter_row(self, blk, r, ybuf, y_hbm, sem):
        pltpu.make_async_copy(ybuf.at[_tile_rows(r), :], y_hbm.at[_tile_rows(self.dst_ref[blk, r]), :],
                              sem).start(priority=self._queue(r))

    @staticmethod
    def _queue(r):
        return r % 2 if isinstance(r, int) else 0

    def all_rows(self, row_fn):
        def body(g, c):
            for j in range(SUBLANES):
                row_fn(g * SUBLANES + j)
            return c

        lax.fori_loop(0, MOE_TM // SUBLANES, body, 0)

    @staticmethod
    def wait_gather(h_hbm, xbuf, sem):
        pltpu.make_async_copy(h_hbm.at[pl.ds(0, MOE_TM * ROW_TILE), :], xbuf, sem).wait()

    @staticmethod
    def wait_scatter(ybuf, y_hbm, sem):
        pltpu.make_async_copy(ybuf, y_hbm.at[pl.ds(0, MOE_TM * ROW_TILE), :], sem).wait()


def _expert_kernel(be_ref, used_ref, dst_ref, h_hbm, wg_ref, wu_ref, wd_ref, y_hbm,
                   xbuf, ybuf, gsem, ssem, *, n_tokens):
    i = pl.program_id(0)
    used = used_ref[0]
    slot = i % 2
    rows = _BlockRows(dst_ref, n_tokens)
    per_hook = MOE_TM // SWIGLU_HOOKS

    def compute(scatter_previous):
        _BlockRows.wait_gather(h_hbm, xbuf.at[slot], gsem.at[slot])
        x = _load_token_tiles(xbuf.at[slot], MOE_TM).astype(BF16)
        nxt = jnp.minimum(i + 1, used - 1)
        done = [0]

        def start_some():
            for r in range(done[0], done[0] + per_hook):
                rows.gather_row(nxt, r, h_hbm, xbuf.at[1 - slot], gsem.at[1 - slot])
                if scatter_previous:
                    rows.scatter_row(i - 1, r, ybuf.at[1 - slot], y_hbm, ssem.at[0])
            done[0] += per_hook

        y = _swiglu_tile(x, wg_ref, wu_ref, wd_ref, hook=start_some)
        assert done[0] == MOE_TM
        _store_token_tiles(ybuf.at[slot], y)

    @pl.when(i == 0)
    def _():
        rows.all_rows(lambda r: rows.gather_row(0, r, h_hbm, xbuf.at[0], gsem.at[0]))
        ybuf[1] = jnp.zeros((MOE_TM * ROW_TILE, LANES), F32)
        clear = pltpu.make_async_copy(
            ybuf.at[1], y_hbm.at[pl.ds(n_tokens * TOP_K * ROW_TILE, MOE_TM * ROW_TILE), :], ssem.at[0])
        clear.start()
        clear.wait()
        compute(False)

    @pl.when(jnp.logical_and(i >= 2, i <= used))
    def _():
        _BlockRows.wait_scatter(ybuf.at[slot], y_hbm, ssem.at[0])

    @pl.when(jnp.logical_and(i >= 1, i < used))
    def _():
        compute(True)

    @pl.when(i == used)
    def _():
        _BlockRows.wait_gather(h_hbm, xbuf.at[slot], gsem.at[slot])
        rows.all_rows(lambda r: rows.scatter_row(i - 1, r, ybuf.at[1 - slot], y_hbm, ssem.at[0]))
        _BlockRows.wait_scatter(ybuf.at[1 - slot], y_hbm, ssem.at[0])


def _experts(h, block_expert, used, row_dst, wg, wu, wd, n_tokens):
    d = D_MODEL
    n_blocks = row_dst.shape[0]
    f = wg.shape[-1]
    assert n_tokens * TOP_K >= 2 * MOE_TM
    one = pl.Buffered(1)
    grid_spec = pltpu.PrefetchScalarGridSpec(
        num_scalar_prefetch=3,
        grid=(n_blocks,),
        in_specs=[pl.BlockSpec(memory_space=pl.ANY),
                  pl.BlockSpec((None, d, f), lambda i, be, us, ds: (be[i], 0, 0), pipeline_mode=one),
                  pl.BlockSpec((None, d, f), lambda i, be, us, ds: (be[i], 0, 0), pipeline_mode=one),
                  pl.BlockSpec((None, f, d), lambda i, be, us, ds: (be[i], 0, 0), pipeline_mode=one)],
        out_specs=pl.BlockSpec(memory_space=pl.ANY),
        scratch_shapes=[pltpu.VMEM((2, MOE_TM * ROW_TILE, LANES), F32),
                        pltpu.VMEM((2, MOE_TM * ROW_TILE, LANES), F32),
                        pltpu.SemaphoreType.DMA((2,)), pltpu.SemaphoreType.DMA((1,))],
    )
    return pl.pallas_call(
        functools.partial(_expert_kernel, n_tokens=n_tokens),
        grid_spec=grid_spec,
        out_shape=jax.ShapeDtypeStruct(((n_tokens * TOP_K + MOE_TM) * ROW_TILE, LANES), F32),
        compiler_params=_cparams("arbitrary"),
        name="expert_swiglu",
    )(block_expert, used, row_dst, h, wg, wu, wd)


def _combine_kernel(y_ref, x_ref, route_ref, o_ref):
    tt = x_ref.shape[0]
    route = route_ref[...]
    out = x_ref[...]
    for k in range(TOP_K):
        yk = jnp.concatenate([y_ref[pl.ds(k * ROW_TILE + c, tt, stride=TOP_K * ROW_TILE), :]
                              for c in range(ROW_TILE)], axis=-1)
        out = out + route[:, TOP_K + k:TOP_K + k + 1] * yk
    o_ref[...] = out


def _combine(x, ys, route):
    n, d = x.shape
    tt = COMBINE_TT
    return pl.pallas_call(
        _combine_kernel,
        grid=(n // tt,),
        in_specs=[pl.BlockSpec((tt * TOP_K * ROW_TILE, LANES), lambda i: (i, 0)),
                  pl.BlockSpec((tt, d), lambda i: (i, 0)),
                  pl.BlockSpec((tt, LANES), lambda i: (i, 0))],
        out_specs=pl.BlockSpec((tt, d), lambda i: (i, 0)),
        out_shape=jax.ShapeDtypeStruct((n, d), F32),
        compiler_params=_cparams("arbitrary"),
        name="expert_combine",
    )(ys, x, route)


def _routed_ffn(x, g, w_router, b_router, wg, wu, wd, *, tm):
    n, d = x.shape
    h, route = _router(x, g, w_router, b_router, tm=tm)
    e_flat = route[:, :TOP_K].astype(jnp.int32).reshape(-1)
    onehot = (e_flat[:, None] == jnp.arange(N_EXPERTS)[None, :]).astype(jnp.int32)
    csum = jnp.cumsum(onehot, axis=0)
    counts = csum[-1]
    rank = jnp.sum((csum - onehot) * onehot, axis=1)
    padded = (counts + MOE_TM - 1) // MOE_TM * MOE_TM
    pend = jnp.cumsum(padded)
    pstart = pend - padded
    dest = (pstart[e_flat] + rank).astype(jnp.int32)
    n_blocks = (n * TOP_K) // MOE_TM + N_EXPERTS
    scratch_rows = n * TOP_K + jnp.arange(n_blocks * MOE_TM, dtype=jnp.int32) % MOE_TM
    row_dst = scratch_rows.at[dest].set(jnp.arange(n * TOP_K, dtype=jnp.int32)).reshape(n_blocks, MOE_TM)
    block_expert = jnp.clip(jnp.searchsorted(pend, jnp.arange(n_blocks) * MOE_TM, side='right'),
                            0, N_EXPERTS - 1).astype(jnp.int32)
    used = (pend[-1:] // MOE_TM).astype(jnp.int32)
    ys = _experts(h, block_expert, used, row_dst, wg, wu, wd, n)
    return _combine(x, ys, route)


def kernel(x, attn_norm_g, w_in, q_norm_g, k_norm_g, conv_w, conv_b, conv_norm_g, conv_norm_b,
           w_branch_a, w_branch_b, w_branch_c, w_out, ffn_norm_g, w_ffn_gate, w_ffn_up,
           w_ffn_down, w_router, b_router, w_exp_gate, w_exp_up, w_exp_down):
    bsz, seq, d = x.shape
    depth = attn_norm_g.shape[0]
    n = bsz * seq
    tm = 512
    xf = x.reshape(n, d)
    s_dil, s_sb, s_glu = 3 * DIL_WIDTH, 3 * DIL_WIDTH + 3 * SB_WIDTH, 3 * DIL_WIDTH + 3 * SB_WIDTH + 2 * CONV_CH
    for layer in range(depth):
        wl = w_in[layer]
        w_perm = jnp.concatenate([wl[:, s_glu:], wl[:, s_sb:s_glu], wl[:, s_dil:s_sb], wl[:, :s_dil]],
                                 axis=1).astype(BF16)
        proj = _norm_matmul(xf, attn_norm_g[layer], w_perm, tm=tm, tn=IN_COLS // 2)
        o_groups, lse_groups = [], []
        for gi, (window, dilation) in enumerate(DIL_GROUPS):
            o_g, lse_g = _dil_attention(proj, q_norm_g[layer], k_norm_g[layer], gi, window, dilation,
                                        bsz, seq)
            o_groups.append(o_g)
            lse_groups.append(lse_g)
        o_b = _sb_attention(proj, bsz, seq)
        o_c = _conformer_conv(proj, conv_w[layer], conv_b[layer], conv_norm_g[layer],
                              conv_norm_b[layer], bsz, seq)
        xf = _merge(o_groups, lse_groups, o_b, o_c, proj, xf,
                    w_branch_a[layer].astype(BF16), w_branch_b[layer].astype(BF16),
                    w_branch_c[layer].astype(BF16), w_out[layer].astype(BF16), tm=tm)
        i = layer // 2
        if layer % 2 == 0:
            xf = _dense_ffn(xf, ffn_norm_g[layer], w_ffn_gate[i].astype(BF16), w_ffn_up[i].astype(BF16),
                            w_ffn_down[i].astype(BF16), tm=tm)
        else:
            xf = _routed_ffn(xf, ffn_norm_g[layer], w_router[i], b_router[i],
                             w_exp_gate[i].astype(BF16), w_exp_up[i].astype(BF16),
                             w_exp_down[i].astype(BF16), tm=tm)
    return xf.reshape(bsz, seq, d)
```

```python
import functools
import math

import jax
import jax.numpy as jnp
from jax import lax
from jax.experimental import pallas as pl
from jax.experimental.pallas import tpu as pltpu

F32 = jnp.float32
BF16 = jnp.bfloat16

D_MODEL = 1024
HEAD_DIM = 64
DIL_GROUPS = ((128, 1), (512, 4), (2048, 16))
DIL_HEADS_PER_GROUP = 4
DIL_HEADS = len(DIL_GROUPS) * DIL_HEADS_PER_GROUP
DIL_WIDTH = DIL_HEADS * HEAD_DIM
DIL_OUT = DIL_HEADS_PER_GROUP * HEAD_DIM
DIL_BLOCK = 128
SB_HEADS = 8
SB_WIDTH = SB_HEADS * HEAD_DIM
CONV_CH = D_MODEL // 2
CONV_WIDTH = 31
N_BRANCH = 3
IN_COLS = 3 * DIL_WIDTH + 3 * SB_WIDTH + 2 * CONV_CH + N_BRANCH * D_MODEL
D_FF = 2816
N_EXPERTS = 8
TOP_K = 2
EPS = 1e-6
ALIBI_MAX_BIAS = 8.0
NEG_BIG = -1e30

COL_GATES = 0
COL_GLU = COL_GATES + N_BRANCH * D_MODEL
COL_SB = COL_GLU + 2 * CONV_CH
COL_DIL = COL_SB + 3 * SB_WIDTH

LANES = 128
SUBLANES = 8
VMEM_LIMIT = 56 * 1024 * 1024

DIL_UNITS = 4
SB_TQ = 512
SB_TK = 128
SB_CHUNK = 2
SB_DEAD_RUN = 128.0
CONV_ROWS = 64
CONV_PAD = 32
FF_CHUNK = 1408
MOE_TM = 512
COMBINE_TT = 256


def _cparams(*sem):
    return pltpu.CompilerParams(dimension_semantics=sem, vmem_limit_bytes=VMEM_LIMIT)


def _sigmoid(x):
    return 1.0 / (1.0 + jnp.exp(-x))


def _norm_matmul_kernel(x_ref, g_ref, w_ref, o_ref):
    x = x_ref[...]
    ms = jnp.mean(x * x, axis=-1, keepdims=True)
    h = (x * lax.rsqrt(ms + EPS)) * g_ref[...]
    o_ref[...] = jnp.dot(h.astype(BF16), w_ref[...], preferred_element_type=F32).astype(o_ref.dtype)


def _norm_matmul(x, g, w, *, tm, tn):
    n, d = x.shape
    e = w.shape[1]
    return pl.pallas_call(
        _norm_matmul_kernel,
        grid=(e // tn, n // tm),
        in_specs=[pl.BlockSpec((tm, d), lambda j, i: (i, 0)),
                  pl.BlockSpec((1, d), lambda j, i: (0, 0)),
                  pl.BlockSpec((d, tn), lambda j, i: (0, j))],
        out_specs=pl.BlockSpec((tm, tn), lambda j, i: (i, j)),
        out_shape=jax.ShapeDtypeStruct((n, e), BF16),
        compiler_params=_cparams("arbitrary", "arbitrary"),
        name="norm_in_proj",
    )(x, g.reshape(1, d), w)


def _same_head_matrix(w):
    r = lax.broadcasted_iota(jnp.int32, (w, w), 0) // HEAD_DIM
    c = lax.broadcasted_iota(jnp.int32, (w, w), 1) // HEAD_DIM
    return (r == c).astype(BF16)


def _head_rms_scale(t, same_head):
    sq = t * t
    hi = sq.astype(BF16)
    lo = (sq - hi.astype(F32)).astype(BF16)
    ssq = (jnp.dot(hi, same_head, preferred_element_type=F32)
           + jnp.dot(lo, same_head, preferred_element_type=F32))
    return lax.rsqrt(ssq * (1.0 / HEAD_DIM) + EPS)


def _dil_attn_kernel(q_ref, k_ref, v_ref, qg_ref, kg_ref, bias_ref, o_ref, lse_ref,
                     qn_scr, kn_scr, v_scr, o_scr, *, seq, dilation):
    blk, d = DIL_BLOCK, dilation
    span = blk * d
    log2d = d.bit_length() - 1
    same_head = _same_head_matrix(DIL_OUT)
    qg = qg_ref[...] * (1.0 / math.sqrt(HEAD_DIM))
    kg = kg_ref[...]
    chunk = 256

    pairs = DIL_OUT // LANES
    pair_lanes = [slice(p * LANES, (p + 1) * LANES) for p in range(pairs)]
    for p in range(pairs):
        kn_scr[p, pl.ds(0, span), :] = jnp.zeros((span, LANES), F32)
        v_scr[p, pl.ds(0, span), :] = jnp.zeros((span, LANES), F32)

    def norm(i, c):
        rows = pl.ds(pl.multiple_of(i * chunk, chunk), chunk)
        prows = pl.ds(pl.multiple_of(span + i * chunk, blk), chunk)
        q = q_ref[0, rows, :].astype(F32)
        qn = q * _head_rms_scale(q, same_head) * qg
        k = k_ref[0, rows, :].astype(F32)
        kn = k * _head_rms_scale(k, same_head) * kg
        v = v_ref[0, rows, :].astype(F32)
        for p in range(pairs):
            qn_scr[p, rows, :] = qn[:, pair_lanes[p]]
            kn_scr[p, prows, :] = kn[:, pair_lanes[p]]
            v_scr[p, prows, :] = v[:, pair_lanes[p]]
        return c

    lax.fori_loop(0, seq // chunk, norm, 0)

    low_half = lax.broadcasted_iota(jnp.int32, (blk, LANES), 1) < HEAD_DIM

    def sub_rows(base, count):
        return pl.ds(pl.multiple_of(base, blk), count) if d == 1 else pl.ds(base, count, stride=d)

    def load(u):
        if d == 1:
            n, base = u, u * span
        else:
            n = u >> log2d
            base = (u & (d - 1)) + n * span
        q = [qn_scr[p, sub_rows(base, blk), :] for p in range(pairs)]
        kk = [kn_scr[p, sub_rows(base, 2 * blk), :].astype(BF16) for p in range(pairs)]
        vv = [v_scr[p, sub_rows(base, 2 * blk), :].astype(BF16) for p in range(pairs)]
        first = jnp.where(n == 0, 1, 0)
        return base, q, kk, vv, first

    def scores(unit):
        _, q, kk, _, first = unit
        out = []
        for h in range(DIL_HEADS_PER_GROUP):
            keep = low_half if h % 2 == 0 else jnp.logical_not(low_half)
            qh = jnp.where(keep, q[h // 2], 0.0).astype(BF16)
            s = lax.dot_general(qh, kk[h // 2], (((1,), (1,)), ((), ())), preferred_element_type=F32)
            out.append(s + bias_ref[h, first])
        return out

    def finish(unit, s_list):
        base, _, _, vv, _ = unit
        rows = sub_rows(base, blk)
        for p in range(pairs):
            oh, lh = [], []
            for hh in range(2):
                s = s_list[2 * p + hh]
                m = jnp.max(s, axis=-1, keepdims=True)
                e = jnp.exp(s - m)
                den = jnp.sum(e, axis=-1, keepdims=True)
                oh.append(jnp.dot(e.astype(BF16), vv[p], preferred_element_type=F32) * (1.0 / den))
                lh.append(m + jnp.log(den))
            o_scr[p, rows, :] = jnp.where(low_half, oh[0], oh[1])
            lse_ref[0, p, rows, :] = jnp.where(low_half, lh[0], lh[1])

    def body(it, c):
        units = [load(it * DIL_UNITS + i) for i in range(DIL_UNITS)]
        s = [None] * DIL_UNITS
        for step in range(DIL_UNITS + 1):
            if step < DIL_UNITS:
                s[step] = scores(units[step])
            if step >= 1:
                finish(units[step - 1], s[step - 1])
        return c

    lax.fori_loop(0, seq // blk // DIL_UNITS, body, 0)

    def emit(i, c):
        rows = pl.ds(pl.multiple_of(i * chunk, chunk), chunk)
        o_ref[0, rows, :] = jnp.concatenate([o_scr[p, rows, :] for p in range(pairs)],
                                            axis=-1).astype(o_ref.dtype)
        return c

    lax.fori_loop(0, seq // chunk, emit, 0)


def _dil_bias_table(group, window, dilation):
    reach = window // dilation
    assert reach <= DIL_BLOCK
    slopes = 2.0 ** (-ALIBI_MAX_BIAS * jnp.arange(1, DIL_HEADS + 1, dtype=F32) / DIL_HEADS)
    slopes = slopes[group * DIL_HEADS_PER_GROUP:(group + 1) * DIL_HEADS_PER_GROUP]
    qi = jnp.arange(DIL_BLOCK)[:, None] + DIL_BLOCK
    ki = jnp.arange(2 * DIL_BLOCK)[None, :]
    dist = qi - ki
    valid = (dist >= 0) & (dist <= reach)
    bias = -slopes[:, None, None] * (dist * dilation).astype(F32)[None]
    general = jnp.where(valid[None], bias, NEG_BIG)
    first = jnp.where((valid & (ki >= DIL_BLOCK))[None], bias, NEG_BIG)
    return jnp.stack([general, first], axis=1)


def _dil_attention(proj, q_gain, k_gain, group, window, dilation, bsz, seq):
    e = proj.shape[-1]
    assert (seq // DIL_BLOCK) % DIL_UNITS == 0
    assert seq % (DIL_BLOCK * dilation) == 0 and dilation & (dilation - 1) == 0
    view = proj.reshape(bsz, seq, e)
    w = DIL_OUT
    pairs = w // LANES
    pad = DIL_BLOCK * dilation

    def col(base):
        off = (base + group * w) // w
        return lambda b: (b, 0, off)

    bias = _dil_bias_table(group, window, dilation)
    gain = lambda g: jnp.tile(g, DIL_HEADS_PER_GROUP).reshape(1, w)
    o, lse = pl.pallas_call(
        functools.partial(_dil_attn_kernel, seq=seq, dilation=dilation),
        grid=(bsz,),
        in_specs=[pl.BlockSpec((1, seq, w), col(COL_DIL)),
                  pl.BlockSpec((1, seq, w), col(COL_DIL + DIL_WIDTH)),
                  pl.BlockSpec((1, seq, w), col(COL_DIL + 2 * DIL_WIDTH)),
                  pl.BlockSpec((1, w), lambda b: (0, 0)),
                  pl.BlockSpec((1, w), lambda b: (0, 0)),
                  pl.BlockSpec(bias.shape, lambda b: (0, 0, 0, 0))],
        out_specs=[pl.BlockSpec((1, seq, w), lambda b: (b, 0, 0)),
                   pl.BlockSpec((1, pairs, seq, LANES), lambda b: (b, 0, 0, 0))],
        out_shape=[jax.ShapeDtypeStruct((bsz, seq, w), BF16),
                   jax.ShapeDtypeStruct((bsz, pairs, seq, LANES), F32)],
        scratch_shapes=[pltpu.VMEM((pairs, seq, LANES), F32), pltpu.VMEM((pairs, pad + seq, LANES), F32),
                        pltpu.VMEM((pairs, pad + seq, LANES), F32), pltpu.VMEM((pairs, seq, LANES), F32)],
        compiler_params=_cparams("arbitrary"),
        name=f"dilated_attn_g{group}",
    )(view, view, view, gain(q_gain), gain(k_gain), bias)
    return o.reshape(bsz * seq, w), lse


def _sb_attn_kernel(q_ref, k_ref, v_ref, o_ref, vcat_scr, acc_scr, *, seq):
    tq, tk, nblk = SB_TQ, SB_TK, SB_TQ // SB_TK
    qi = pl.program_id(2)

    @pl.when(qi == 0)
    def _():
        chan = lax.broadcasted_iota(jnp.int32, (LANES, tk), 0)

        def build(kb, c):
            vt = v_ref[0, pl.ds(pl.multiple_of(kb * tk, tk), tk), :].astype(F32).T
            vcat_scr[kb] = jnp.concatenate(
                [jnp.where(chan < HEAD_DIM, vt, 0.0), jnp.where(chan >= HEAD_DIM, vt, 0.0)],
                axis=1).astype(BF16)
            return c

        lax.fori_loop(0, seq // tk, build, 0)

    lane = lax.broadcasted_iota(jnp.int32, (tq, LANES), 1)
    q = q_ref[0] * (1.0 / math.sqrt(HEAD_DIM))
    zero = jnp.zeros_like(q)
    qcat = jnp.concatenate([jnp.where(lane < HEAD_DIM, q, zero), jnp.where(lane >= HEAD_DIM, q, zero)],
                           axis=0)
    qcat_t = qcat.astype(F32).T.astype(BF16)
    neg_tri = jnp.where(lax.broadcasted_iota(jnp.int32, (tk, tk), 1)
                        >= lax.broadcasted_iota(jnp.int32, (tk, tk), 0), -1.0, 0.0).astype(BF16)
    acc_scr[...] = jnp.zeros_like(acc_scr)

    def scores(kb, rel):
        kk = k_ref[0, pl.ds(pl.multiple_of(kb * tk, tk), tk), :]
        zt = lax.dot_general(kk, qcat, (((1,), (1,)), ((), ())), preferred_element_type=F32)
        neg_abs = lax.bitcast_convert_type(
            lax.bitcast_convert_type(zt, jnp.uint32) | jnp.uint32(0x80000000), F32)
        sp = jnp.maximum(zt, 0.0) + jnp.log(1.0 + jnp.exp(neg_abs))
        before = None
        if rel is not None:
            kpos = lax.broadcasted_iota(jnp.int32, (tk, 2 * tq), 0) + rel * tk
            qpos = lax.broadcasted_iota(jnp.int32, (tk, 2 * tq), 1) & (tq - 1)
            before = kpos < qpos
            sp = jnp.where(before, sp, 0.0)
        return kk, zt[0:1, :], sp.astype(BF16), before

    def weights(state, run):
        kk, zt0, sp, before = state
        arg = jnp.dot(jnp.concatenate([neg_tri, kk], axis=1),
                      jnp.concatenate([sp, qcat_t], axis=0), preferred_element_type=F32)
        a = jnp.exp(arg - run)
        if before is not None:
            a = jnp.where(before, a, 0.0)
        a = a.astype(BF16)
        acat = jnp.concatenate([a[:, :tq], a[:, tq:]], axis=0)
        return acat, run + (zt0 - arg[0:1, :])

    def values(kb, acat):
        acc_scr[...] += jnp.dot(vcat_scr[kb], acat, preferred_element_type=F32)

    def run_blocks(blocks, run):
        n = len(blocks)
        st, ac = [None] * n, [None] * n
        for step in range(n + 2):
            if step < n:
                st[step] = scores(*blocks[step])
            if 0 <= step - 1 < n:
                ac[step - 1], run = weights(st[step - 1], run)
            if 0 <= step - 2 < n:
                values(blocks[step - 2][0], ac[step - 2])
        return run

    run = jnp.zeros((1, 2 * tq), F32)
    run = run_blocks([(qi * nblk + rel, rel) for rel in reversed(range(nblk))], run)

    def alive(run):
        return (jnp.min(run) < SB_DEAD_RUN).astype(jnp.int32)

    n_chunks = qi * (nblk // SB_CHUNK)

    def more(carry):
        it, _, go = carry
        return jnp.logical_and(it < n_chunks, go > 0)

    def chunk(carry):
        it, run, _ = carry
        base = (n_chunks - 1 - it) * SB_CHUNK
        run = run_blocks([(base + j, None) for j in reversed(range(SB_CHUNK))], run)
        return it + 1, run, alive(run)

    lax.while_loop(more, chunk, (jnp.int32(0), run, alive(run)))
    o_ref[0] = acc_scr[...].T.astype(o_ref.dtype)


def _sb_attention(proj, bsz, seq):
    e = proj.shape[-1]
    view = proj.reshape(bsz, seq, e)
    pairs = SB_WIDTH // LANES
    qo, ko, vo = COL_SB // LANES, (COL_SB + SB_WIDTH) // LANES, (COL_SB + 2 * SB_WIDTH) // LANES
    out = pl.pallas_call(
        functools.partial(_sb_attn_kernel, seq=seq),
        grid=(bsz, pairs, seq // SB_TQ),
        in_specs=[pl.BlockSpec((1, SB_TQ, LANES), lambda b, p, i: (b, i, qo + p)),
                  pl.BlockSpec((1, seq, LANES), lambda b, p, i: (b, 0, ko + p)),
                  pl.BlockSpec((1, seq, LANES), lambda b, p, i: (b, 0, vo + p))],
        out_specs=pl.BlockSpec((1, SB_TQ, LANES), lambda b, p, i: (b, i, p)),
        out_shape=jax.ShapeDtypeStruct((bsz, seq, SB_WIDTH), BF16),
        scratch_shapes=[pltpu.VMEM((seq // SB_TK, LANES, 2 * SB_TK), BF16),
                        pltpu.VMEM((LANES, SB_TQ), F32)],
        compiler_params=_cparams("arbitrary", "arbitrary", "arbitrary"),
        name="stick_breaking_attn",
    )(view, view, view)
    return out.reshape(bsz * seq, SB_WIDTH)


def _conv_kernel(val_ref, gate_ref, w_ref, b_ref, g_ref, beta_ref, o_ref, u_scr, sh_scr, *, seq):
    tr, pad, half = CONV_ROWS, CONV_PAD, CONV_CH // 2
    u_scr[pl.ds(0, pad), :] = jnp.zeros((pad, CONV_CH), F32)

    def glu(i, c):
        rows = pl.ds(pl.multiple_of(i * 256, 256), 256)
        val = val_ref[0, rows, :].astype(F32)
        gate = gate_ref[0, rows, :].astype(F32)
        u_scr[pl.ds(pl.multiple_of(pad + i * 256, SUBLANES), 256), :] = val * _sigmoid(gate)
        return c

    lax.fori_loop(0, seq // 256, glu, 0)

    def tile(i, c):
        t0 = pl.multiple_of(i * tr, tr)
        parts = []
        for ch in range(2):
            cs = slice(ch * half, (ch + 1) * half)
            win = u_scr[pl.ds(t0, tr + pad), cs]
            acc = jnp.zeros((tr, half), F32)
            first = pad - (CONV_WIDTH - 1)
            for r in range(SUBLANES):
                offs = [o for o in range(first, first + CONV_WIDTH) if o % SUBLANES == r]
                n_rows = offs[-1] - r + tr
                sh_scr[r, pl.ds(0, n_rows), :] = win[r:r + n_rows, :]
                for o in offs:
                    w = o - first
                    acc = acc + sh_scr[r, pl.ds(o - r, tr), :] * w_ref[w:w + 1, cs]
            parts.append(acc)
        y = jnp.concatenate(parts, axis=-1) + b_ref[...]
        mu = jnp.mean(y, axis=-1, keepdims=True)
        yc = y - mu
        var = jnp.mean(yc * yc, axis=-1, keepdims=True)
        yn = yc * lax.rsqrt(var + EPS) * g_ref[...] + beta_ref[...]
        o_ref[0, pl.ds(t0, tr), :] = (yn * _sigmoid(yn)).astype(o_ref.dtype)
        return c

    lax.fori_loop(0, seq // tr, tile, 0)


def _conformer_conv(proj, conv_w, conv_b, norm_g, norm_b, bsz, seq):
    e = proj.shape[-1]
    view = proj.reshape(bsz, seq, e)
    c = CONV_CH
    voff, goff = COL_GLU // c, (COL_GLU + c) // c
    const = lambda b: (0, 0)
    out = pl.pallas_call(
        functools.partial(_conv_kernel, seq=seq),
        grid=(bsz,),
        in_specs=[pl.BlockSpec((1, seq, c), lambda b: (b, 0, voff)),
                  pl.BlockSpec((1, seq, c), lambda b: (b, 0, goff)),
                  pl.BlockSpec((CONV_WIDTH, c), const),
                  pl.BlockSpec((1, c), const), pl.BlockSpec((1, c), const), pl.BlockSpec((1, c), const)],
        out_specs=pl.BlockSpec((1, seq, c), lambda b: (b, 0, 0)),
        out_shape=jax.ShapeDtypeStruct((bsz, seq, c), BF16),
        scratch_shapes=[pltpu.VMEM((seq + CONV_PAD, c), F32),
                        pltpu.VMEM((SUBLANES, CONV_ROWS + CONV_PAD, c // 2), F32)],
        compiler_params=_cparams("arbitrary"),
        name="conformer_conv",
    )(view, view, conv_w, conv_b.reshape(1, c), norm_g.reshape(1, c), norm_b.reshape(1, c))
    return out.reshape(bsz * seq, c)


def _merge_kernel(o1_ref, o2_ref, o3_ref, l1_ref, l2_ref, l3_ref, ob_ref, oc_ref,
                  ga_ref, gb_ref, gc_ref, x_ref, wa_ref, wb_ref, wc_ref, wo_ref, out_ref):
    by_lanes = lambda ref: jnp.concatenate([ref[p] for p in range(ref.shape[0])], axis=-1)
    l1, l2, l3 = by_lanes(l1_ref), by_lanes(l2_ref), by_lanes(l3_ref)
    m = jnp.maximum(jnp.maximum(l1, l2), l3)
    e1, e2, e3 = jnp.exp(l1 - m), jnp.exp(l2 - m), jnp.exp(l3 - m)
    o_a = (e1 * o1_ref[...].astype(F32) + e2 * o2_ref[...].astype(F32)
           + e3 * o3_ref[...].astype(F32)) / (e1 + e2 + e3)
    ya = jnp.dot(o_a.astype(BF16), wa_ref[...], preferred_element_type=F32)
    yb = jnp.dot(ob_ref[...], wb_ref[...], preferred_element_type=F32)
    yc = jnp.dot(oc_ref[...], wc_ref[...], preferred_element_type=F32)
    merged = (_sigmoid(ga_ref[...].astype(F32)) * ya + _sigmoid(gb_ref[...].astype(F32)) * yb
              + _sigmoid(gc_ref[...].astype(F32)) * yc)
    out_ref[...] = x_ref[...] + jnp.dot(merged.astype(BF16), wo_ref[...], preferred_element_type=F32)


def _merge(o_groups, lse_groups, o_b, o_c, proj, x, wa, wb, wc, wo, *, tm):
    n, d = x.shape
    row = lambda w, j=0: pl.BlockSpec((tm, w), lambda i: (i, j))
    full = lambda a: pl.BlockSpec(a.shape, lambda i: (0, 0), pipeline_mode=pl.Buffered(1))
    g0 = COL_GATES // d
    _, pairs, seq, _ = lse_groups[0].shape
    per_batch = seq // tm
    lse = pl.BlockSpec((None, pairs, tm, LANES), lambda i: (i // per_batch, 0, i % per_batch, 0))
    return pl.pallas_call(
        _merge_kernel,
        grid=(n // tm,),
        in_specs=[row(DIL_OUT), row(DIL_OUT), row(DIL_OUT), lse, lse, lse,
                  row(SB_WIDTH), row(CONV_CH), row(d, g0), row(d, g0 + 1), row(d, g0 + 2), row(d),
                  full(wa), full(wb), full(wc), full(wo)],
        out_specs=row(d),
        out_shape=jax.ShapeDtypeStruct((n, d), F32),
        compiler_params=_cparams("arbitrary"),
        name="branch_merge_out_proj",
    )(*o_groups, *lse_groups, o_b, o_c, proj, proj, proj, x, wa, wb, wc, wo)


MXU_COLS = 256


def _col_pieces(lo, hi):
    return [(p, min(p + MXU_COLS, hi)) for p in range(lo, hi, MXU_COLS)]


SWIGLU_HOOKS = (D_FF // FF_CHUNK) * (2 * len(_col_pieces(0, FF_CHUNK)) + len(_col_pieces(0, D_MODEL)))


def _swiglu_tile(h, wg_ref, wu_ref, wd_ref, hook=None):
    def matmul(x, w_ref, rows, lo, hi):
        if hook is None:
            return jnp.dot(x, w_ref[rows, lo:hi], preferred_element_type=F32)
        pieces = []
        for p, q in _col_pieces(lo, hi):
            pieces.append(jnp.dot(x, w_ref[rows, p:q], preferred_element_type=F32))
            hook()
        return jnp.concatenate(pieces, axis=-1)

    y = None
    for c in range(D_FF // FF_CHUNK):
        lo, hi = c * FF_CHUNK, (c + 1) * FF_CHUNK
        a = matmul(h, wg_ref, slice(None), lo, hi)
        u = matmul(h, wu_ref, slice(None), lo, hi)
        act = (a * _sigmoid(a) * u).astype(BF16)
        part = matmul(act, wd_ref, slice(lo, hi), 0, D_MODEL)
        y = part if y is None else y + part
    return y


def _dense_ffn_kernel(x_ref, g_ref, wg_ref, wu_ref, wd_ref, o_ref):
    x = x_ref[...]
    ms = jnp.mean(x * x, axis=-1, keepdims=True)
    h = ((x * lax.rsqrt(ms + EPS)) * g_ref[...]).astype(BF16)
    o_ref[...] = x + _swiglu_tile(h, wg_ref, wu_ref, wd_ref)


def _dense_ffn(x, g, wg, wu, wd, *, tm):
    n, d = x.shape
    full = lambda a: pl.BlockSpec(a.shape, lambda i: (0, 0), pipeline_mode=pl.Buffered(1))
    return pl.pallas_call(
        _dense_ffn_kernel,
        grid=(n // tm,),
        in_specs=[pl.BlockSpec((tm, d), lambda i: (i, 0)), pl.BlockSpec((1, d), lambda i: (0, 0)),
                  full(wg), full(wu), full(wd)],
        out_specs=pl.BlockSpec((tm, d), lambda i: (i, 0)),
        out_shape=jax.ShapeDtypeStruct((n, d), F32),
        compiler_params=_cparams("arbitrary"),
        name="dense_swiglu",
    )(x, g.reshape(1, d), wg, wu, wd)


def _router_kernel(x_ref, g_ref, wr_ref, br_ref, h_ref, route_ref):
    x = x_ref[...]
    ms = jnp.mean(x * x, axis=-1, keepdims=True)
    h = (x * lax.rsqrt(ms + EPS)) * g_ref[...]
    _store_token_tiles(h_ref, h)
    logits = jnp.dot(h, wr_ref[...], preferred_element_type=F32,
                     precision=lax.Precision.HIGHEST) + br_ref[...]
    lane = lax.broadcasted_iota(jnp.int32, logits.shape, 1)
    m1 = jnp.max(logits, axis=-1, keepdims=True)
    i1 = jnp.min(jnp.where(logits == m1, lane, LANES), axis=-1, keepdims=True)
    rest = jnp.where(lane == i1, -jnp.inf, logits)
    m2 = jnp.max(rest, axis=-1, keepdims=True)
    i2 = jnp.min(jnp.where(rest == m2, lane, LANES), axis=-1, keepdims=True)
    e2 = jnp.exp(m2 - m1)
    g1 = 1.0 / (1.0 + e2)
    g2 = e2 / (1.0 + e2)
    route = jnp.where(lane == 0, i1.astype(F32),
                      jnp.where(lane == 1, i2.astype(F32),
                                jnp.where(lane == 2, g1, jnp.where(lane == 3, g2, 0.0))))
    route_ref[...] = route


def _router(x, g, w_router, b_router, *, tm):
    n, d = x.shape
    wr = jnp.zeros((d, LANES), F32).at[:, :N_EXPERTS].set(w_router)
    br = jnp.full((1, LANES), NEG_BIG, F32).at[0, :N_EXPERTS].set(b_router)
    return pl.pallas_call(
        _router_kernel,
        grid=(n // tm,),
        in_specs=[pl.BlockSpec((tm, d), lambda i: (i, 0)), pl.BlockSpec((1, d), lambda i: (0, 0)),
                  pl.BlockSpec((d, LANES), lambda i: (0, 0)), pl.BlockSpec((1, LANES), lambda i: (0, 0))],
        out_specs=[pl.BlockSpec((tm * ROW_TILE, LANES), lambda i: (i, 0)),
                   pl.BlockSpec((tm, LANES), lambda i: (i, 0))],
        out_shape=[jax.ShapeDtypeStruct((n * ROW_TILE, LANES), F32), jax.ShapeDtypeStruct((n, LANES), F32)],
        compiler_params=_cparams("arbitrary"),
        name="router_top2",
    )(x, g.reshape(1, d), wr, br)


ROW_TILE = D_MODEL // LANES


def _store_token_tiles(ref, x):
    rows = x.shape[0]
    for c in range(ROW_TILE):
        ref[pl.ds(c, rows, stride=ROW_TILE), :] = x[:, c * LANES:(c + 1) * LANES]


def _load_token_tiles(ref, rows):
    return jnp.concatenate([ref[pl.ds(c, rows, stride=ROW_TILE), :] for c in range(ROW_TILE)], axis=-1)


def _tile_rows(t):
    start = t * ROW_TILE
    return pl.ds(start if isinstance(start, int) else pl.multiple_of(start, ROW_TILE), ROW_TILE)


class _BlockRows:
    def __init__(self, dst_ref, n_tokens):
        self.dst_ref = dst_ref
        self.last_token = n_tokens - 1

    def gather_row(self, blk, r, h_hbm, xbuf, sem):
        tok = lax.shift_right_logical(self.dst_ref[blk, r], TOP_K.bit_length() - 1)
        tok = jnp.minimum(tok, self.last_token)
        pltpu.make_async_copy(h_hbm.at[_tile_rows(tok), :], xbuf.at[_tile_rows(r), :],
                              sem).start(priority=self._queue(r))

    def scatter_row(self, blk, r, ybuf, y_hbm, sem):
        pltpu.make_async_copy(ybuf.at[_tile_rows(r), :], y_hbm.at[_tile_rows(self.dst_ref[blk, r]), :],
                              sem).start(priority=self._queue(r))

    @staticmethod
    def _queue(r):
        return r % 2 if isinstance(r, int) else 0

    def all_rows(self, row_fn):
        def body(g, c):
            for j in range(SUBLANES):
                row_fn(g * SUBLANES + j)
            return c

        lax.fori_loop(0, MOE_TM // SUBLANES, body, 0)

    @staticmethod
    def wait_gather(h_hbm, xbuf, sem):
        pltpu.make_async_copy(h_hbm.at[pl.ds(0, MOE_TM * ROW_TILE), :], xbuf, sem).wait()

    @staticmethod
    def wait_scatter(ybuf, y_hbm, sem):
        pltpu.make_async_copy(ybuf, y_hbm.at[pl.ds(0, MOE_TM * ROW_TILE), :], sem).wait()


def _expert_kernel(be_ref, used_ref, dst_ref, h_hbm, wg_ref, wu_ref, wd_ref, y_hbm,
                   xbuf, ybuf, gsem, ssem, *, n_tokens):
    i = pl.program_id(0)
    used = used_ref[0]
    slot = i % 2
    rows = _BlockRows(dst_ref, n_tokens)
    gather_hooks = SWIGLU_HOOKS // 2
    gather_per_hook = MOE_TM // gather_hooks
    scatter_per_hook = MOE_TM // SWIGLU_HOOKS

    def compute(scatter_previous):
        _BlockRows.wait_gather(h_hbm, xbuf.at[slot], gsem.at[slot])
        x = _load_token_tiles(xbuf.at[slot], MOE_TM).astype(BF16)
        nxt = jnp.minimum(i + 1, used - 1)
        calls = [0]

        def start_some():
            k = calls[0]
            calls[0] += 1
            if k < gather_hooks:
                for r in range(k * gather_per_hook, (k + 1) * gather_per_hook):
                    rows.gather_row(nxt, r, h_hbm, xbuf.at[1 - slot], gsem.at[1 - slot])
            if scatter_previous:
                for r in range(k * scatter_per_hook, (k + 1) * scatter_per_hook):
                    rows.scatter_row(i - 1, r, ybuf.at[1 - slot], y_hbm, ssem.at[1 - slot])

        y = _swiglu_tile(x, wg_ref, wu_ref, wd_ref, hook=start_some)
        assert calls[0] == SWIGLU_HOOKS

        @pl.when(i >= 2)
        def _():
            _BlockRows.wait_scatter(ybuf.at[slot], y_hbm, ssem.at[slot])

        _store_token_tiles(ybuf.at[slot], y)

    @pl.when(i == 0)
    def _():
        rows.all_rows(lambda r: rows.gather_row(0, r, h_hbm, xbuf.at[0], gsem.at[0]))
        ybuf[1] = jnp.zeros((MOE_TM * ROW_TILE, LANES), F32)
        for half in range(2):
            clear = pltpu.make_async_copy(
                ybuf.at[1],
                y_hbm.at[pl.ds((n_tokens * TOP_K + half * MOE_TM) * ROW_TILE, MOE_TM * ROW_TILE), :],
                ssem.at[half])
            clear.start()
            clear.wait()
        compute(False)

    @pl.when(jnp.logical_and(i >= 1, i < used))
    def _():
        compute(True)

    @pl.when(i == used)
    def _():
        _BlockRows.wait_gather(h_hbm, xbuf.at[slot], gsem.at[slot])
        _BlockRows.wait_scatter(ybuf.at[slot], y_hbm, ssem.at[slot])
        rows.all_rows(lambda r: rows.scatter_row(i - 1, r, ybuf.at[1 - slot], y_hbm, ssem.at[1 - slot]))
        _BlockRows.wait_scatter(ybuf.at[1 - slot], y_hbm, ssem.at[1 - slot])


def _experts(h, block_expert, used, row_dst, wg, wu, wd, n_tokens):
    d = D_MODEL
    n_blocks = row_dst.shape[0]
    f = wg.shape[-1]
    assert n_tokens * TOP_K >= 2 * MOE_TM
    one = pl.Buffered(1)
    grid_spec = pltpu.PrefetchScalarGridSpec(
        num_scalar_prefetch=3,
        grid=(n_blocks,),
        in_specs=[pl.BlockSpec(memory_space=pl.ANY),
                  pl.BlockSpec((None, d, f), lambda i, be, us, ds: (be[i], 0, 0), pipeline_mode=one),
                  pl.BlockSpec((None, d, f), lambda i, be, us, ds: (be[i], 0, 0), pipeline_mode=one),
                  pl.BlockSpec((None, f, d), lambda i, be, us, ds: (be[i], 0, 0), pipeline_mode=one)],
        out_specs=pl.BlockSpec(memory_space=pl.ANY),
        scratch_shapes=[pltpu.VMEM((2, MOE_TM * ROW_TILE, LANES), F32),
                        pltpu.VMEM((2, MOE_TM * ROW_TILE, LANES), F32),
                        pltpu.SemaphoreType.DMA((2,)), pltpu.SemaphoreType.DMA((2,))],
    )
    return pl.pallas_call(
        functools.partial(_expert_kernel, n_tokens=n_tokens),
        grid_spec=grid_spec,
        out_shape=jax.ShapeDtypeStruct(((n_tokens * TOP_K + 2 * MOE_TM) * ROW_TILE, LANES), F32),
        compiler_params=_cparams("arbitrary"),
        name="expert_swiglu",
    )(block_expert, used, row_dst, h, wg, wu, wd)


def _combine_kernel(y_ref, x_ref, route_ref, o_ref):
    tt = x_ref.shape[0]
    route = route_ref[...]
    out = x_ref[...]
    for k in range(TOP_K):
        yk = jnp.concatenate([y_ref[pl.ds(k * ROW_TILE + c, tt, stride=TOP_K * ROW_TILE), :]
                              for c in range(ROW_TILE)], axis=-1)
        out = out + route[:, TOP_K + k:TOP_K + k + 1] * yk
    o_ref[...] = out


def _combine(x, ys, route):
    n, d = x.shape
    tt = COMBINE_TT
    return pl.pallas_call(
        _combine_kernel,
        grid=(n // tt,),
        in_specs=[pl.BlockSpec((tt * TOP_K * ROW_TILE, LANES), lambda i: (i, 0)),
                  pl.BlockSpec((tt, d), lambda i: (i, 0)),
                  pl.BlockSpec((tt, LANES), lambda i: (i, 0))],
        out_specs=pl.BlockSpec((tt, d), lambda i: (i, 0)),
        out_shape=jax.ShapeDtypeStruct((n, d), F32),
        compiler_params=_cparams("arbitrary"),
        name="expert_combine",
    )(ys, x, route)


def _routed_ffn(x, g, w_router, b_router, wg, wu, wd, *, tm):
    n, d = x.shape
    h, route = _router(x, g, w_router, b_router, tm=tm)
    e_flat = route[:, :TOP_K].astype(jnp.int32).reshape(-1)
    onehot = (e_flat[:, None] == jnp.arange(N_EXPERTS)[None, :]).astype(jnp.int32)
    csum = jnp.cumsum(onehot, axis=0)
    counts = csum[-1]
    rank = jnp.sum((csum - onehot) * onehot, axis=1)
    padded = (counts + MOE_TM - 1) // MOE_TM * MOE_TM
    pend = jnp.cumsum(padded)
    pstart = pend - padded
    dest = (pstart[e_flat] + rank).astype(jnp.int32)
    n_blocks = (n * TOP_K) // MOE_TM + N_EXPERTS
    scratch_rows = n * TOP_K + jnp.arange(n_blocks * MOE_TM, dtype=jnp.int32) % (2 * MOE_TM)
    row_dst = scratch_rows.at[dest].set(jnp.arange(n * TOP_K, dtype=jnp.int32)).reshape(n_blocks, MOE_TM)
    block_expert = jnp.clip(jnp.searchsorted(pend, jnp.arange(n_blocks) * MOE_TM, side='right'),
                            0, N_EXPERTS - 1).astype(jnp.int32)
    used = (pend[-1:] // MOE_TM).astype(jnp.int32)
    ys = _experts(h, block_expert, used, row_dst, wg, wu, wd, n)
    return _combine(x, ys, route)


def kernel(x, attn_norm_g, w_in, q_norm_g, k_norm_g, conv_w, conv_b, conv_norm_g, conv_norm_b,
           w_branch_a, w_branch_b, w_branch_c, w_out, ffn_norm_g, w_ffn_gate, w_ffn_up,
           w_ffn_down, w_router, b_router, w_exp_gate, w_exp_up, w_exp_down):
    bsz, seq, d = x.shape
    depth = attn_norm_g.shape[0]
    n = bsz * seq
    tm = 512
    xf = x.reshape(n, d)
    s_dil, s_sb, s_glu = 3 * DIL_WIDTH, 3 * DIL_WIDTH + 3 * SB_WIDTH, 3 * DIL_WIDTH + 3 * SB_WIDTH + 2 * CONV_CH
    for layer in range(depth):
        wl = w_in[layer]
        w_perm = jnp.concatenate([wl[:, s_glu:], wl[:, s_sb:s_glu], wl[:, s_dil:s_sb], wl[:, :s_dil]],
                                 axis=1).astype(BF16)
        proj = _norm_matmul(xf, attn_norm_g[layer], w_perm, tm=tm, tn=IN_COLS // 2)
        o_groups, lse_groups = [], []
        for gi, (window, dilation) in enumerate(DIL_GROUPS):
            o_g, lse_g = _dil_attention(proj, q_norm_g[layer], k_norm_g[layer], gi, window, dilation,
                                        bsz, seq)
            o_groups.append(o_g)
            lse_groups.append(lse_g)
        o_b = _sb_attention(proj, bsz, seq)
        o_c = _conformer_conv(proj, conv_w[layer], conv_b[layer], conv_norm_g[layer],
                              conv_norm_b[layer], bsz, seq)
        xf = _merge(o_groups, lse_groups, o_b, o_c, proj, xf,
                    w_branch_a[layer].astype(BF16), w_branch_b[layer].astype(BF16),
                    w_branch_c[layer].astype(BF16), w_out[layer].astype(BF16), tm=tm)
        i = layer // 2
        if layer % 2 == 0:
            xf = _dense_ffn(xf, ffn_norm_g[layer], w_ffn_gate[i].astype(BF16), w_ffn_up[i].astype(BF16),
                            w_ffn_down[i].astype(BF16), tm=tm)
        else:
            xf = _routed_ffn(xf, ffn_norm_g[layer], w_router[i], b_router[i],
                             w_exp_gate[i].astype(BF16), w_exp_up[i].astype(BF16),
                             w_exp_down[i].astype(BF16), tm=tm)
    return xf.reshape(bsz, seq, d)
```

```python
import functools
import math

import jax
import jax.numpy as jnp
from jax import lax
from jax.experimental import pallas as pl
from jax.experimental.pallas import tpu as pltpu

F32 = jnp.float32
BF16 = jnp.bfloat16

D_MODEL = 1024
HEAD_DIM = 64
DIL_GROUPS = ((128, 1), (512, 4), (2048, 16))
DIL_HEADS_PER_GROUP = 4
DIL_HEADS = len(DIL_GROUPS) * DIL_HEADS_PER_GROUP
DIL_WIDTH = DIL_HEADS * HEAD_DIM
DIL_OUT = DIL_HEADS_PER_GROUP * HEAD_DIM
DIL_BLOCK = 128
SB_HEADS = 8
SB_WIDTH = SB_HEADS * HEAD_DIM
CONV_CH = D_MODEL // 2
CONV_WIDTH = 31
N_BRANCH = 3
IN_COLS = 3 * DIL_WIDTH + 3 * SB_WIDTH + 2 * CONV_CH + N_BRANCH * D_MODEL
D_FF = 2816
N_EXPERTS = 8
TOP_K = 2
EPS = 1e-6
ALIBI_MAX_BIAS = 8.0
NEG_BIG = -1e30

COL_GATES = 0
COL_GLU = COL_GATES + N_BRANCH * D_MODEL
COL_SB = COL_GLU + 2 * CONV_CH
COL_DIL = COL_SB + 3 * SB_WIDTH

LANES = 128
SUBLANES = 8
VMEM_LIMIT = 56 * 1024 * 1024

DIL_UNITS = 4
SB_TQ = 512
SB_TK = 128
SB_CHUNK = 2
SB_DEAD_RUN = 128.0
CONV_ROWS = 64
CONV_PAD = 32
FF_CHUNK = 1408
MOE_TM = 512
COMBINE_TT = 256


def _cparams(*sem):
    return pltpu.CompilerParams(dimension_semantics=sem, vmem_limit_bytes=VMEM_LIMIT)


def _sigmoid(x):
    return 1.0 / (1.0 + jnp.exp(-x))


def _norm_matmul_kernel(x_ref, g_ref, w_ref, o_ref):
    x = x_ref[...]
    ms = jnp.mean(x * x, axis=-1, keepdims=True)
    h = (x * lax.rsqrt(ms + EPS)) * g_ref[...]
    o_ref[...] = jnp.dot(h.astype(BF16), w_ref[...], preferred_element_type=F32).astype(o_ref.dtype)


def _norm_matmul(x, g, w, *, tm, tn):
    n, d = x.shape
    e = w.shape[1]
    return pl.pallas_call(
        _norm_matmul_kernel,
        grid=(e // tn, n // tm),
        in_specs=[pl.BlockSpec((tm, d), lambda j, i: (i, 0)),
                  pl.BlockSpec((1, d), lambda j, i: (0, 0)),
                  pl.BlockSpec((d, tn), lambda j, i: (0, j))],
        out_specs=pl.BlockSpec((tm, tn), lambda j, i: (i, j)),
        out_shape=jax.ShapeDtypeStruct((n, e), BF16),
        compiler_params=_cparams("arbitrary", "arbitrary"),
        name="norm_in_proj",
    )(x, g.reshape(1, d), w)


def _same_head_matrix(w):
    r = lax.broadcasted_iota(jnp.int32, (w, w), 0) // HEAD_DIM
    c = lax.broadcasted_iota(jnp.int32, (w, w), 1) // HEAD_DIM
    return (r == c).astype(BF16)


def _head_rms_scale(t, same_head):
    ssq = jnp.dot((t * t).astype(BF16), same_head, preferred_element_type=F32)
    return lax.rsqrt(ssq * (1.0 / HEAD_DIM) + EPS)


def _dil_attn_kernel(q_ref, k_ref, v_ref, qg_ref, kg_ref, bias_ref, o_ref, lse_ref,
                     qn_scr, kn_scr, v_scr, o_scr, *, seq, dilation):
    blk, d = DIL_BLOCK, dilation
    span = blk * d
    log2d = d.bit_length() - 1
    same_head = _same_head_matrix(DIL_OUT)
    qg = qg_ref[...] * (1.0 / math.sqrt(HEAD_DIM))
    kg = kg_ref[...]
    chunk = 256

    pairs = DIL_OUT // LANES
    pair_lanes = [slice(p * LANES, (p + 1) * LANES) for p in range(pairs)]
    for p in range(pairs):
        kn_scr[p, pl.ds(0, span), :] = jnp.zeros((span, LANES), F32)
        v_scr[p, pl.ds(0, span), :] = jnp.zeros((span, LANES), F32)

    def norm(i, c):
        rows = pl.ds(pl.multiple_of(i * chunk, chunk), chunk)
        prows = pl.ds(pl.multiple_of(span + i * chunk, blk), chunk)
        q = q_ref[0, rows, :].astype(F32)
        qn = q * _head_rms_scale(q, same_head) * qg
        k = k_ref[0, rows, :].astype(F32)
        kn = k * _head_rms_scale(k, same_head) * kg
        v = v_ref[0, rows, :].astype(F32)
        for p in range(pairs):
            qn_scr[p, rows, :] = qn[:, pair_lanes[p]]
            kn_scr[p, prows, :] = kn[:, pair_lanes[p]]
            v_scr[p, prows, :] = v[:, pair_lanes[p]]
        return c

    lax.fori_loop(0, seq // chunk, norm, 0)

    low_half = lax.broadcasted_iota(jnp.int32, (blk, LANES), 1) < HEAD_DIM

    def sub_rows(base, count):
        return pl.ds(pl.multiple_of(base, blk), count) if d == 1 else pl.ds(base, count, stride=d)

    def load(u):
        if d == 1:
            n, base = u, u * span
        else:
            n = u >> log2d
            base = (u & (d - 1)) + n * span
        q = [qn_scr[p, sub_rows(base, blk), :] for p in range(pairs)]
        kk = [kn_scr[p, sub_rows(base, 2 * blk), :].astype(BF16) for p in range(pairs)]
        vv = [v_scr[p, sub_rows(base, 2 * blk), :].astype(BF16) for p in range(pairs)]
        first = jnp.where(n == 0, 1, 0)
        return base, q, kk, vv, first

    def scores(unit):
        _, q, kk, _, first = unit
        out = []
        for h in range(DIL_HEADS_PER_GROUP):
            keep = low_half if h % 2 == 0 else jnp.logical_not(low_half)
            qh = jnp.where(keep, q[h // 2], 0.0).astype(BF16)
            s = lax.dot_general(qh, kk[h // 2], (((1,), (1,)), ((), ())), preferred_element_type=F32)
            out.append(s + bias_ref[h, first])
        return out

    def finish(unit, s_list):
        base, _, _, vv, _ = unit
        rows = sub_rows(base, blk)
        for p in range(pairs):
            oh, lh = [], []
            for hh in range(2):
                s = s_list[2 * p + hh]
                m = jnp.max(s, axis=-1, keepdims=True)
                e = jnp.exp(s - m)
                den = jnp.sum(e, axis=-1, keepdims=True)
                oh.append(jnp.dot(e.astype(BF16), vv[p], preferred_element_type=F32) * (1.0 / den))
                lh.append(m + jnp.log(den))
            o_scr[p, rows, :] = jnp.where(low_half, oh[0], oh[1])
            lse_ref[0, p, rows, :] = jnp.where(low_half, lh[0], lh[1])

    def body(it, c):
        units = [load(it * DIL_UNITS + i) for i in range(DIL_UNITS)]
        s = [None] * DIL_UNITS
        for step in range(DIL_UNITS + 1):
            if step < DIL_UNITS:
                s[step] = scores(units[step])
            if step >= 1:
                finish(units[step - 1], s[step - 1])
        return c

    lax.fori_loop(0, seq // blk // DIL_UNITS, body, 0)

    def emit(i, c):
        rows = pl.ds(pl.multiple_of(i * chunk, chunk), chunk)
        o_ref[0, rows, :] = jnp.concatenate([o_scr[p, rows, :] for p in range(pairs)],
                                            axis=-1).astype(o_ref.dtype)
        return c

    lax.fori_loop(0, seq // chunk, emit, 0)


def _dil_bias_table(group, window, dilation):
    reach = window // dilation
    assert reach <= DIL_BLOCK
    slopes = 2.0 ** (-ALIBI_MAX_BIAS * jnp.arange(1, DIL_HEADS + 1, dtype=F32) / DIL_HEADS)
    slopes = slopes[group * DIL_HEADS_PER_GROUP:(group + 1) * DIL_HEADS_PER_GROUP]
    qi = jnp.arange(DIL_BLOCK)[:, None] + DIL_BLOCK
    ki = jnp.arange(2 * DIL_BLOCK)[None, :]
    dist = qi - ki
    valid = (dist >= 0) & (dist <= reach)
    bias = -slopes[:, None, None] * (dist * dilation).astype(F32)[None]
    general = jnp.where(valid[None], bias, NEG_BIG)
    first = jnp.where((valid & (ki >= DIL_BLOCK))[None], bias, NEG_BIG)
    return jnp.stack([general, first], axis=1)


def _dil_attention(proj, q_gain, k_gain, group, window, dilation, bsz, seq):
    e = proj.shape[-1]
    assert (seq // DIL_BLOCK) % DIL_UNITS == 0
    assert seq % (DIL_BLOCK * dilation) == 0 and dilation & (dilation - 1) == 0
    view = proj.reshape(bsz, seq, e)
    w = DIL_OUT
    pairs = w // LANES
    pad = DIL_BLOCK * dilation

    def col(base):
        off = (base + group * w) // w
        return lambda b: (b, 0, off)

    bias = _dil_bias_table(group, window, dilation)
    gain = lambda g: jnp.tile(g, DIL_HEADS_PER_GROUP).reshape(1, w)
    o, lse = pl.pallas_call(
        functools.partial(_dil_attn_kernel, seq=seq, dilation=dilation),
        grid=(bsz,),
        in_specs=[pl.BlockSpec((1, seq, w), col(COL_DIL)),
                  pl.BlockSpec((1, seq, w), col(COL_DIL + DIL_WIDTH)),
                  pl.BlockSpec((1, seq, w), col(COL_DIL + 2 * DIL_WIDTH)),
                  pl.BlockSpec((1, w), lambda b: (0, 0)),
                  pl.BlockSpec((1, w), lambda b: (0, 0)),
                  pl.BlockSpec(bias.shape, lambda b: (0, 0, 0, 0))],
        out_specs=[pl.BlockSpec((1, seq, w), lambda b: (b, 0, 0)),
                   pl.BlockSpec((1, pairs, seq, LANES), lambda b: (b, 0, 0, 0))],
        out_shape=[jax.ShapeDtypeStruct((bsz, seq, w), BF16),
                   jax.ShapeDtypeStruct((bsz, pairs, seq, LANES), F32)],
        scratch_shapes=[pltpu.VMEM((pairs, seq, LANES), F32), pltpu.VMEM((pairs, pad + seq, LANES), F32),
                        pltpu.VMEM((pairs, pad + seq, LANES), F32), pltpu.VMEM((pairs, seq, LANES), F32)],
        compiler_params=_cparams("arbitrary"),
        name=f"dilated_attn_g{group}",
    )(view, view, view, gain(q_gain), gain(k_gain), bias)
    return o.reshape(bsz * seq, w), lse


def _sb_attn_kernel(q_ref, k_ref, v_ref, o_ref, vcat_scr, acc_scr, *, seq):
    tq, tk, nblk = SB_TQ, SB_TK, SB_TQ // SB_TK
    qi = pl.program_id(2)

    @pl.when(qi == 0)
    def _():
        chan = lax.broadcasted_iota(jnp.int32, (LANES, tk), 0)

        def build(kb, c):
            vt = v_ref[0, pl.ds(pl.multiple_of(kb * tk, tk), tk), :].astype(F32).T
            vcat_scr[kb] = jnp.concatenate(
                [jnp.where(chan < HEAD_DIM, vt, 0.0), jnp.where(chan >= HEAD_DIM, vt, 0.0)],
                axis=1).astype(BF16)
            return c

        lax.fori_loop(0, seq // tk, build, 0)

    lane = lax.broadcasted_iota(jnp.int32, (tq, LANES), 1)
    q = q_ref[0] * (1.0 / math.sqrt(HEAD_DIM))
    zero = jnp.zeros_like(q)
    qcat = jnp.concatenate([jnp.where(lane < HEAD_DIM, q, zero), jnp.where(lane >= HEAD_DIM, q, zero)],
                           axis=0)
    qcat_t = qcat.astype(F32).T.astype(BF16)
    neg_tri = jnp.where(lax.broadcasted_iota(jnp.int32, (tk, tk), 1)
                        >= lax.broadcasted_iota(jnp.int32, (tk, tk), 0), -1.0, 0.0).astype(BF16)
    acc_scr[...] = jnp.zeros_like(acc_scr)

    def scores(kb, rel):
        kk = k_ref[0, pl.ds(pl.multiple_of(kb * tk, tk), tk), :]
        zt = lax.dot_general(kk, qcat, (((1,), (1,)), ((), ())), preferred_element_type=F32)
        neg_abs = lax.bitcast_convert_type(
            lax.bitcast_convert_type(zt, jnp.uint32) | jnp.uint32(0x80000000), F32)
        sp = jnp.maximum(zt, 0.0) + jnp.log(1.0 + jnp.exp(neg_abs))
        before = None
        if rel is not None:
            kpos = lax.broadcasted_iota(jnp.int32, (tk, 2 * tq), 0) + rel * tk
            qpos = lax.broadcasted_iota(jnp.int32, (tk, 2 * tq), 1) & (tq - 1)
            before = kpos < qpos
            sp = jnp.where(before, sp, 0.0)
        return kk, zt[0:1, :], sp.astype(BF16), before

    def weights(state, run):
        kk, zt0, sp, before = state
        arg = jnp.dot(jnp.concatenate([neg_tri, kk], axis=1),
                      jnp.concatenate([sp, qcat_t], axis=0), preferred_element_type=F32)
        a = jnp.exp(arg - run)
        if before is not None:
            a = jnp.where(before, a, 0.0)
        a = a.astype(BF16)
        acat = jnp.concatenate([a[:, :tq], a[:, tq:]], axis=0)
        return acat, run + (zt0 - arg[0:1, :])

    def values(kb, acat):
        acc_scr[...] += jnp.dot(vcat_scr[kb], acat, preferred_element_type=F32)

    def run_blocks(blocks, run):
        n = len(blocks)
        st, ac = [None] * n, [None] * n
        for step in range(n + 2):
            if step < n:
                st[step] = scores(*blocks[step])
            if 0 <= step - 1 < n:
                ac[step - 1], run = weights(st[step - 1], run)
            if 0 <= step - 2 < n:
                values(blocks[step - 2][0], ac[step - 2])
        return run

    zero_run = jnp.zeros((1, 2 * tq), F32)
    diagonal = [(qi * nblk + rel, rel) for rel in reversed(range(nblk))]
    n_chunks = qi * (nblk // SB_CHUNK)

    def chunk_blocks(it):
        base = (n_chunks - 1 - it) * SB_CHUNK
        return [(base + j, None) for j in reversed(range(SB_CHUNK))]

    @pl.when(qi == 0)
    def _():
        run_blocks(diagonal, zero_run)

    @pl.when(qi > 0)
    def _():
        run = run_blocks(diagonal + chunk_blocks(0), zero_run)

        def alive(run):
            return (jnp.min(run) < SB_DEAD_RUN).astype(jnp.int32)

        def more(carry):
            it, _, go = carry
            return jnp.logical_and(it < n_chunks, go > 0)

        def chunk(carry):
            it, run, _ = carry
            run = run_blocks(chunk_blocks(it), run)
            return it + 1, run, alive(run)

        lax.while_loop(more, chunk, (jnp.int32(1), run, alive(run)))

    o_ref[0] = acc_scr[...].T.astype(o_ref.dtype)


def _sb_attention(proj, bsz, seq):
    e = proj.shape[-1]
    view = proj.reshape(bsz, seq, e)
    pairs = SB_WIDTH // LANES
    qo, ko, vo = COL_SB // LANES, (COL_SB + SB_WIDTH) // LANES, (COL_SB + 2 * SB_WIDTH) // LANES
    out = pl.pallas_call(
        functools.partial(_sb_attn_kernel, seq=seq),
        grid=(bsz, pairs, seq // SB_TQ),
        in_specs=[pl.BlockSpec((1, SB_TQ, LANES), lambda b, p, i: (b, i, qo + p)),
                  pl.BlockSpec((1, seq, LANES), lambda b, p, i: (b, 0, ko + p)),
                  pl.BlockSpec((1, seq, LANES), lambda b, p, i: (b, 0, vo + p))],
        out_specs=pl.BlockSpec((1, SB_TQ, LANES), lambda b, p, i: (b, i, p)),
        out_shape=jax.ShapeDtypeStruct((bsz, seq, SB_WIDTH), BF16),
        scratch_shapes=[pltpu.VMEM((seq // SB_TK, LANES, 2 * SB_TK), BF16),
                        pltpu.VMEM((LANES, SB_TQ), F32)],
        compiler_params=_cparams("arbitrary", "arbitrary", "arbitrary"),
        name="stick_breaking_attn",
    )(view, view, view)
    return out.reshape(bsz * seq, SB_WIDTH)


def _conv_kernel(val_ref, gate_ref, w_ref, b_ref, g_ref, beta_ref, o_ref, u_scr, sh_scr, *, seq):
    tr, pad, half = CONV_ROWS, CONV_PAD, CONV_CH // 2
    u_scr[pl.ds(0, pad), :] = jnp.zeros((pad, CONV_CH), F32)

    def glu(i, c):
        rows = pl.ds(pl.multiple_of(i * 256, 256), 256)
        val = val_ref[0, rows, :].astype(F32)
        gate = gate_ref[0, rows, :].astype(F32)
        u_scr[pl.ds(pl.multiple_of(pad + i * 256, SUBLANES), 256), :] = val * _sigmoid(gate)
        return c

    lax.fori_loop(0, seq // 256, glu, 0)

    def tile(i, c):
        t0 = pl.multiple_of(i * tr, tr)
        parts = []
        for ch in range(2):
            cs = slice(ch * half, (ch + 1) * half)
            win = u_scr[pl.ds(t0, tr + pad), cs]
            acc = jnp.zeros((tr, half), F32)
            first = pad - (CONV_WIDTH - 1)
            for r in range(SUBLANES):
                offs = [o for o in range(first, first + CONV_WIDTH) if o % SUBLANES == r]
                n_rows = offs[-1] - r + tr
                sh_scr[r, pl.ds(0, n_rows), :] = win[r:r + n_rows, :]
                for o in offs:
                    w = o - first
                    acc = acc + sh_scr[r, pl.ds(o - r, tr), :] * w_ref[w:w + 1, cs]
            parts.append(acc)
        y = jnp.concatenate(parts, axis=-1) + b_ref[...]
        mu = jnp.mean(y, axis=-1, keepdims=True)
        yc = y - mu
        var = jnp.mean(yc * yc, axis=-1, keepdims=True)
        yn = yc * lax.rsqrt(var + EPS) * g_ref[...] + beta_ref[...]
        o_ref[0, pl.ds(t0, tr), :] = (yn * _sigmoid(yn)).astype(o_ref.dtype)
        return c

    lax.fori_loop(0, seq // tr, tile, 0)


def _conformer_conv(proj, conv_w, conv_b, norm_g, norm_b, bsz, seq):
    e = proj.shape[-1]
    view = proj.reshape(bsz, seq, e)
    c = CONV_CH
    voff, goff = COL_GLU // c, (COL_GLU + c) // c
    const = lambda b: (0, 0)
    out = pl.pallas_call(
        functools.partial(_conv_kernel, seq=seq),
        grid=(bsz,),
        in_specs=[pl.BlockSpec((1, seq, c), lambda b: (b, 0, voff)),
                  pl.BlockSpec((1, seq, c), lambda b: (b, 0, goff)),
                  pl.BlockSpec((CONV_WIDTH, c), const),
                  pl.BlockSpec((1, c), const), pl.BlockSpec((1, c), const), pl.BlockSpec((1, c), const)],
        out_specs=pl.BlockSpec((1, seq, c), lambda b: (b, 0, 0)),
        out_shape=jax.ShapeDtypeStruct((bsz, seq, c), BF16),
        scratch_shapes=[pltpu.VMEM((seq + CONV_PAD, c), F32),
                        pltpu.VMEM((SUBLANES, CONV_ROWS + CONV_PAD, c // 2), F32)],
        compiler_params=_cparams("arbitrary"),
        name="conformer_conv",
    )(view, view, conv_w, conv_b.reshape(1, c), norm_g.reshape(1, c), norm_b.reshape(1, c))
    return out.reshape(bsz * seq, c)


def _merge_kernel(o1_ref, o2_ref, o3_ref, l1_ref, l2_ref, l3_ref, ob_ref, oc_ref,
                  ga_ref, gb_ref, gc_ref, x_ref, wa_ref, wb_ref, wc_ref, wo_ref, out_ref):
    by_lanes = lambda ref: jnp.concatenate([ref[p] for p in range(ref.shape[0])], axis=-1)
    l1, l2, l3 = by_lanes(l1_ref), by_lanes(l2_ref), by_lanes(l3_ref)
    m = jnp.maximum(jnp.maximum(l1, l2), l3)
    e1, e2, e3 = jnp.exp(l1 - m), jnp.exp(l2 - m), jnp.exp(l3 - m)
    o_a = (e1 * o1_ref[...].astype(F32) + e2 * o2_ref[...].astype(F32)
           + e3 * o3_ref[...].astype(F32)) / (e1 + e2 + e3)
    ya = jnp.dot(o_a.astype(BF16), wa_ref[...], preferred_element_type=F32)
    yb = jnp.dot(ob_ref[...], wb_ref[...], preferred_element_type=F32)
    yc = jnp.dot(oc_ref[...], wc_ref[...], preferred_element_type=F32)
    merged = (_sigmoid(ga_ref[...].astype(F32)) * ya + _sigmoid(gb_ref[...].astype(F32)) * yb
              + _sigmoid(gc_ref[...].astype(F32)) * yc)
    out_ref[...] = x_ref[...] + jnp.dot(merged.astype(BF16), wo_ref[...], preferred_element_type=F32)


def _merge(o_groups, lse_groups, o_b, o_c, proj, x, wa, wb, wc, wo, *, tm):
    n, d = x.shape
    row = lambda w, j=0: pl.BlockSpec((tm, w), lambda i: (i, j))
    full = lambda a: pl.BlockSpec(a.shape, lambda i: (0, 0), pipeline_mode=pl.Buffered(1))
    g0 = COL_GATES // d
    _, pairs, seq, _ = lse_groups[0].shape
    per_batch = seq // tm
    lse = pl.BlockSpec((None, pairs, tm, LANES), lambda i: (i // per_batch, 0, i % per_batch, 0))
    return pl.pallas_call(
        _merge_kernel,
        grid=(n // tm,),
        in_specs=[row(DIL_OUT), row(DIL_OUT), row(DIL_OUT), lse, lse, lse,
                  row(SB_WIDTH), row(CONV_CH), row(d, g0), row(d, g0 + 1), row(d, g0 + 2), row(d),
                  full(wa), full(wb), full(wc), full(wo)],
        out_specs=row(d),
        out_shape=jax.ShapeDtypeStruct((n, d), F32),
        compiler_params=_cparams("arbitrary"),
        name="branch_merge_out_proj",
    )(*o_groups, *lse_groups, o_b, o_c, proj, proj, proj, x, wa, wb, wc, wo)


MXU_COLS = 256


def _col_pieces(lo, hi):
    return [(p, min(p + MXU_COLS, hi)) for p in range(lo, hi, MXU_COLS)]


SWIGLU_HOOKS = (D_FF // FF_CHUNK) * (2 * len(_col_pieces(0, FF_CHUNK)) + len(_col_pieces(0, D_MODEL)))


def _swiglu_tile(h, wg_ref, wu_ref, wd_ref, hook=None):
    def matmul(x, w_ref, rows, lo, hi):
        if hook is None:
            return jnp.dot(x, w_ref[rows, lo:hi], preferred_element_type=F32)
        pieces = []
        for p, q in _col_pieces(lo, hi):
            pieces.append(jnp.dot(x, w_ref[rows, p:q], preferred_element_type=F32))
            hook()
        return jnp.concatenate(pieces, axis=-1)

    y = None
    for c in range(D_FF // FF_CHUNK):
        lo, hi = c * FF_CHUNK, (c + 1) * FF_CHUNK
        a = matmul(h, wg_ref, slice(None), lo, hi)
        u = matmul(h, wu_ref, slice(None), lo, hi)
        act = (a * _sigmoid(a) * u).astype(BF16)
        part = matmul(act, wd_ref, slice(lo, hi), 0, D_MODEL)
        y = part if y is None else y + part
    return y


def _dense_ffn_kernel(x_ref, g_ref, wg_ref, wu_ref, wd_ref, o_ref):
    x = x_ref[...]
    ms = jnp.mean(x * x, axis=-1, keepdims=True)
    h = ((x * lax.rsqrt(ms + EPS)) * g_ref[...]).astype(BF16)
    o_ref[...] = x + _swiglu_tile(h, wg_ref, wu_ref, wd_ref)


def _dense_ffn(x, g, wg, wu, wd, *, tm):
    n, d = x.shape
    full = lambda a: pl.BlockSpec(a.shape, lambda i: (0, 0), pipeline_mode=pl.Buffered(1))
    return pl.pallas_call(
        _dense_ffn_kernel,
        grid=(n // tm,),
        in_specs=[pl.BlockSpec((tm, d), lambda i: (i, 0)), pl.BlockSpec((1, d), lambda i: (0, 0)),
                  full(wg), full(wu), full(wd)],
        out_specs=pl.BlockSpec((tm, d), lambda i: (i, 0)),
        out_shape=jax.ShapeDtypeStruct((n, d), F32),
        compiler_params=_cparams("arbitrary"),
        name="dense_swiglu",
    )(x, g.reshape(1, d), wg, wu, wd)


def _router_kernel(x_ref, g_ref, wr_ref, br_ref, h_ref, route_ref):
    x = x_ref[...]
    ms = jnp.mean(x * x, axis=-1, keepdims=True)
    h = (x * lax.rsqrt(ms + EPS)) * g_ref[...]
    _store_token_tiles(h_ref, h)
    h_hi = h.astype(BF16)
    h_lo = (h - h_hi.astype(F32)).astype(BF16)
    w = wr_ref[...]
    w_hi = w.astype(BF16)
    w_lo = (w - w_hi.astype(F32)).astype(BF16)
    logits = (jnp.dot(h_hi, w_hi, preferred_element_type=F32)
              + jnp.dot(h_lo, w_hi, preferred_element_type=F32)
              + jnp.dot(h_hi, w_lo, preferred_element_type=F32)) + br_ref[...]
    lane = lax.broadcasted_iota(jnp.int32, logits.shape, 1)
    m1 = jnp.max(logits, axis=-1, keepdims=True)
    i1 = jnp.min(jnp.where(logits == m1, lane, LANES), axis=-1, keepdims=True)
    rest = jnp.where(lane == i1, -jnp.inf, logits)
    m2 = jnp.max(rest, axis=-1, keepdims=True)
    i2 = jnp.min(jnp.where(rest == m2, lane, LANES), axis=-1, keepdims=True)
    e2 = jnp.exp(m2 - m1)
    g1 = 1.0 / (1.0 + e2)
    g2 = e2 / (1.0 + e2)
    route = jnp.where(lane == 0, i1.astype(F32),
                      jnp.where(lane == 1, i2.astype(F32),
                                jnp.where(lane == 2, g1, jnp.where(lane == 3, g2, 0.0))))
    route_ref[...] = route


def _router(x, g, w_router, b_router, *, tm):
    n, d = x.shape
    wr = jnp.zeros((d, LANES), F32).at[:, :N_EXPERTS].set(w_router)
    br = jnp.full((1, LANES), NEG_BIG, F32).at[0, :N_EXPERTS].set(b_router)
    return pl.pallas_call(
        _router_kernel,
        grid=(n // tm,),
        in_specs=[pl.BlockSpec((tm, d), lambda i: (i, 0)), pl.BlockSpec((1, d), lambda i: (0, 0)),
                  pl.BlockSpec((d, LANES), lambda i: (0, 0)), pl.BlockSpec((1, LANES), lambda i: (0, 0))],
        out_specs=[pl.BlockSpec((tm * ROW_TILE, LANES), lambda i: (i, 0)),
                   pl.BlockSpec((tm, LANES), lambda i: (i, 0))],
        out_shape=[jax.ShapeDtypeStruct((n * ROW_TILE, LANES), F32), jax.ShapeDtypeStruct((n, LANES), F32)],
        compiler_params=_cparams("arbitrary"),
        name="router_top2",
    )(x, g.reshape(1, d), wr, br)


ROW_TILE = D_MODEL // LANES


def _store_token_tiles(ref, x):
    rows = x.shape[0]
    for c in range(ROW_TILE):
        ref[pl.ds(c, rows, stride=ROW_TILE), :] = x[:, c * LANES:(c + 1) * LANES]


def _load_token_tiles(ref, rows):
    return jnp.concatenate([ref[pl.ds(c, rows, stride=ROW_TILE), :] for c in range(ROW_TILE)], axis=-1)


def _tile_rows(t):
    start = t * ROW_TILE
    return pl.ds(start if isinstance(start, int) else pl.multiple_of(start, ROW_TILE), ROW_TILE)


class _BlockRows:
    def __init__(self, dst_ref, n_tokens):
        self.dst_ref = dst_ref
        self.last_token = n_tokens - 1

    def gather_row(self, blk, r, h_hbm, xbuf, sem):
        tok = lax.shift_right_logical(self.dst_ref[blk, r], TOP_K.bit_length() - 1)
        tok = jnp.minimum(tok, self.last_token)
        pltpu.make_async_copy(h_hbm.at[_tile_rows(tok), :], xbuf.at[_tile_rows(r), :],
                              sem).start(priority=self._queue(r))

    def scatter_row(self, blk, r, ybuf, y_hbm, sem):
        pltpu.make_async_copy(ybuf.at[_tile_rows(r), :], y_hbm.at[_tile_rows(self.dst_ref[blk, r]), :],
                              sem).start(priority=self._queue(r))

    @staticmethod
    def _queue(r):
        return r % 2 if isinstance(r, int) else 0

    def all_rows(self, row_fn):
        def body(g, c):
            for j in range(SUBLANES):
                row_fn(g * SUBLANES + j)
            return c

        lax.fori_loop(0, MOE_TM // SUBLANES, body, 0)

    @staticmethod
    def wait_gather(h_hbm, xbuf, sem):
        pltpu.make_async_copy(h_hbm.at[pl.ds(0, MOE_TM * ROW_TILE), :], xbuf, sem).wait()

    @staticmethod
    def wait_scatter(ybuf, y_hbm, sem):
        pltpu.make_async_copy(ybuf, y_hbm.at[pl.ds(0, MOE_TM * ROW_TILE), :], sem).wait()


def _expert_kernel(be_ref, used_ref, dst_ref, h_hbm, wg_ref, wu_ref, wd_ref, y_hbm,
                   xbuf, ybuf, gsem, ssem, *, n_tokens):
    i = pl.program_id(0)
    used = used_ref[0]
    slot = i % 2
    rows = _BlockRows(dst_ref, n_tokens)
    gather_hooks = SWIGLU_HOOKS // 2
    gather_per_hook = MOE_TM // gather_hooks
    scatter_per_hook = MOE_TM // SWIGLU_HOOKS

    def compute(scatter_previous):
        _BlockRows.wait_gather(h_hbm, xbuf.at[slot], gsem.at[slot])
        x = _load_token_tiles(xbuf.at[slot], MOE_TM).astype(BF16)
        nxt = jnp.minimum(i + 1, used - 1)
        calls = [0]

        def start_some():
            k = calls[0]
            calls[0] += 1
            if k < gather_hooks:
                for r in range(k * gather_per_hook, (k + 1) * gather_per_hook):
                    rows.gather_row(nxt, r, h_hbm, xbuf.at[1 - slot], gsem.at[1 - slot])
            if scatter_previous:
                for r in range(k * scatter_per_hook, (k + 1) * scatter_per_hook):
                    rows.scatter_row(i - 1, r, ybuf.at[1 - slot], y_hbm, ssem.at[1 - slot])

        y = _swiglu_tile(x, wg_ref, wu_ref, wd_ref, hook=start_some)
        assert calls[0] == SWIGLU_HOOKS

        @pl.when(i >= 2)
        def _():
            _BlockRows.wait_scatter(ybuf.at[slot], y_hbm, ssem.at[slot])

        _store_token_tiles(ybuf.at[slot], y)

    @pl.when(i == 0)
    def _():
        rows.all_rows(lambda r: rows.gather_row(0, r, h_hbm, xbuf.at[0], gsem.at[0]))
        ybuf[1] = jnp.zeros((MOE_TM * ROW_TILE, LANES), F32)
        for half in range(2):
            clear = pltpu.make_async_copy(
                ybuf.at[1],
                y_hbm.at[pl.ds((n_tokens * TOP_K + half * MOE_TM) * ROW_TILE, MOE_TM * ROW_TILE), :],
                ssem.at[half])
            clear.start()
            clear.wait()
        compute(False)

    @pl.when(jnp.logical_and(i >= 1, i < used))
    def _():
        compute(True)

    @pl.when(i == used)
    def _():
        _BlockRows.wait_gather(h_hbm, xbuf.at[slot], gsem.at[slot])
        _BlockRows.wait_scatter(ybuf.at[slot], y_hbm, ssem.at[slot])
        rows.all_rows(lambda r: rows.scatter_row(i - 1, r, ybuf.at[1 - slot], y_hbm, ssem.at[1 - slot]))
        _BlockRows.wait_scatter(ybuf.at[1 - slot], y_hbm, ssem.at[1 - slot])


def _experts(h, block_expert, used, row_dst, wg, wu, wd, n_tokens):
    d = D_MODEL
    n_blocks = row_dst.shape[0]
    f = wg.shape[-1]
    assert n_tokens * TOP_K >= 2 * MOE_TM
    one = pl.Buffered(1)
    grid_spec = pltpu.PrefetchScalarGridSpec(
        num_scalar_prefetch=3,
        grid=(n_blocks,),
        in_specs=[pl.BlockSpec(memory_space=pl.ANY),
                  pl.BlockSpec((None, d, f), lambda i, be, us, ds: (be[i], 0, 0), pipeline_mode=one),
                  pl.BlockSpec((None, d, f), lambda i, be, us, ds: (be[i], 0, 0), pipeline_mode=one),
                  pl.BlockSpec((None, f, d), lambda i, be, us, ds: (be[i], 0, 0), pipeline_mode=one)],
        out_specs=pl.BlockSpec(memory_space=pl.ANY),
        scratch_shapes=[pltpu.VMEM((2, MOE_TM * ROW_TILE, LANES), F32),
                        pltpu.VMEM((2, MOE_TM * ROW_TILE, LANES), F32),
                        pltpu.SemaphoreType.DMA((2,)), pltpu.SemaphoreType.DMA((2,))],
    )
    return pl.pallas_call(
        functools.partial(_expert_kernel, n_tokens=n_tokens),
        grid_spec=grid_spec,
        out_shape=jax.ShapeDtypeStruct(((n_tokens * TOP_K + 2 * MOE_TM) * ROW_TILE, LANES), F32),
        compiler_params=_cparams("arbitrary"),
        name="expert_swiglu",
    )(block_expert, used, row_dst, h, wg, wu, wd)


def _combine_kernel(y_ref, x_ref, route_ref, o_ref):
    tt = x_ref.shape[0]
    route = route_ref[...]
    out = x_ref[...]
    for k in range(TOP_K):
        yk = jnp.concatenate([y_ref[pl.ds(k * ROW_TILE + c, tt, stride=TOP_K * ROW_TILE), :]
                              for c in range(ROW_TILE)], axis=-1)
        out = out + route[:, TOP_K + k:TOP_K + k + 1] * yk
    o_ref[...] = out


def _combine(x, ys, route):
    n, d = x.shape
    tt = COMBINE_TT
    return pl.pallas_call(
        _combine_kernel,
        grid=(n // tt,),
        in_specs=[pl.BlockSpec((tt * TOP_K * ROW_TILE, LANES), lambda i: (i, 0)),
                  pl.BlockSpec((tt, d), lambda i: (i, 0)),
                  pl.BlockSpec((tt, LANES), lambda i: (i, 0))],
        out_specs=pl.BlockSpec((tt, d), lambda i: (i, 0)),
        out_shape=jax.ShapeDtypeStruct((n, d), F32),
        compiler_params=_cparams("arbitrary"),
        name="expert_combine",
    )(ys, x, route)


def _routed_ffn(x, g, w_router, b_router, wg, wu, wd, *, tm):
    n, d = x.shape
    h, route = _router(x, g, w_router, b_router, tm=tm)
    e_flat = route[:, :TOP_K].astype(jnp.int32).reshape(-1)
    onehot = (e_flat[:, None] == jnp.arange(N_EXPERTS)[None, :]).astype(jnp.int32)
    csum = jnp.cumsum(onehot, axis=0)
    counts = csum[-1]
    rank = jnp.sum((csum - onehot) * onehot, axis=1)
    padded = (counts + MOE_TM - 1) // MOE_TM * MOE_TM
    pend = jnp.cumsum(padded)
    pstart = pend - padded
    dest = (pstart[e_flat] + rank).astype(jnp.int32)
    n_blocks = (n * TOP_K) // MOE_TM + N_EXPERTS
    scratch_rows = n * TOP_K + jnp.arange(n_blocks * MOE_TM, dtype=jnp.int32) % (2 * MOE_TM)
    row_dst = scratch_rows.at[dest].set(jnp.arange(n * TOP_K, dtype=jnp.int32)).reshape(n_blocks, MOE_TM)
    block_expert = jnp.clip(jnp.searchsorted(pend, jnp.arange(n_blocks) * MOE_TM, side='right'),
                            0, N_EXPERTS - 1).astype(jnp.int32)
    used = (pend[-1:] // MOE_TM).astype(jnp.int32)
    ys = _experts(h, block_expert, used, row_dst, wg, wu, wd, n)
    return _combine(x, ys, route)


def kernel(x, attn_norm_g, w_in, q_norm_g, k_norm_g, conv_w, conv_b, conv_norm_g, conv_norm_b,
           w_branch_a, w_branch_b, w_branch_c, w_out, ffn_norm_g, w_ffn_gate, w_ffn_up,
           w_ffn_down, w_router, b_router, w_exp_gate, w_exp_up, w_exp_down):
    bsz, seq, d = x.shape
    depth = attn_norm_g.shape[0]
    n = bsz * seq
    tm = 512
    xf = x.reshape(n, d)
    s_dil, s_sb, s_glu = 3 * DIL_WIDTH, 3 * DIL_WIDTH + 3 * SB_WIDTH, 3 * DIL_WIDTH + 3 * SB_WIDTH + 2 * CONV_CH
    for layer in range(depth):
        wl = w_in[layer]
        w_perm = jnp.concatenate([wl[:, s_glu:], wl[:, s_sb:s_glu], wl[:, s_dil:s_sb], wl[:, :s_dil]],
                                 axis=1).astype(BF16)
        proj = _norm_matmul(xf, attn_norm_g[layer], w_perm, tm=tm, tn=IN_COLS // 2)
        o_groups, lse_groups = [], []
        for gi, (window, dilation) in enumerate(DIL_GROUPS):
            o_g, lse_g = _dil_attention(proj, q_norm_g[layer], k_norm_g[layer], gi, window, dilation,
                                        bsz, seq)
            o_groups.append(o_g)
            lse_groups.append(lse_g)
        o_b = _sb_attention(proj, bsz, seq)
        o_c = _conformer_conv(proj, conv_w[layer], conv_b[layer], conv_norm_g[layer],
                              conv_norm_b[layer], bsz, seq)
        xf = _merge(o_groups, lse_groups, o_b, o_c, proj, xf,
                    w_branch_a[layer].astype(BF16), w_branch_b[layer].astype(BF16),
                    w_branch_c[layer].astype(BF16), w_out[layer].astype(BF16), tm=tm)
        i = layer // 2
        if layer % 2 == 0:
            xf = _dense_ffn(xf, ffn_norm_g[layer], w_ffn_gate[i].astype(BF16), w_ffn_up[i].astype(BF16),
                            w_ffn_down[i].astype(BF16), tm=tm)
        else:
            xf = _routed_ffn(xf, ffn_norm_g[layer], w_router[i], b_router[i],
                             w_exp_gate[i].astype(BF16), w_exp_up[i].astype(BF16),
                             w_exp_down[i].astype(BF16), tm=tm)
    return xf.reshape(bsz, seq, d)
```

```python
import functools
import math

import jax
import jax.numpy as jnp
from jax import lax
from jax.experimental import pallas as pl
from jax.experimental.pallas import tpu as pltpu

F32 = jnp.float32
BF16 = jnp.bfloat16

D_MODEL = 1024
HEAD_DIM = 64
DIL_GROUPS = ((128, 1), (512, 4), (2048, 16))
DIL_HEADS_PER_GROUP = 4
DIL_HEADS = len(DIL_GROUPS) * DIL_HEADS_PER_GROUP
DIL_WIDTH = DIL_HEADS * HEAD_DIM
DIL_OUT = DIL_HEADS_PER_GROUP * HEAD_DIM
DIL_BLOCK = 128
SB_HEADS = 8
SB_WIDTH = SB_HEADS * HEAD_DIM
CONV_CH = D_MODEL // 2
CONV_WIDTH = 31
N_BRANCH = 3
IN_COLS = 3 * DIL_WIDTH + 3 * SB_WIDTH + 2 * CONV_CH + N_BRANCH * D_MODEL
D_FF = 2816
N_EXPERTS = 8
TOP_K = 2
EPS = 1e-6
ALIBI_MAX_BIAS = 8.0
NEG_BIG = -1e30

COL_GATES = 0
COL_GLU = COL_GATES + N_BRANCH * D_MODEL
PROJ_A_COLS = COL_GLU + 2 * CONV_CH
COL_SB = 0
COL_DIL = COL_SB + 3 * SB_WIDTH

LANES = 128
SUBLANES = 8
VMEM_LIMIT = 56 * 1024 * 1024

DIL_UNITS = 4
SB_TQ = 512
SB_TK = 128
SB_CHUNK = 2
SB_DEAD_RUN = 128.0
CONV_ROWS = 64
CONV_PAD = 32
FF_SPLIT = 1536
MOE_TM = 512
COMBINE_TT = 256


def _cparams(*sem):
    return pltpu.CompilerParams(dimension_semantics=sem, vmem_limit_bytes=VMEM_LIMIT)


def _sigmoid(x):
    return 1.0 / (1.0 + jnp.exp(-x))


def _norm_matmul_kernel(x_ref, g_ref, w_ref, o_ref):
    x = x_ref[...]
    ms = jnp.mean(x * x, axis=-1, keepdims=True)
    h = (x * lax.rsqrt(ms + EPS)) * g_ref[...]
    o_ref[...] = jnp.dot(h.astype(BF16), w_ref[...], preferred_element_type=F32).astype(o_ref.dtype)


def _norm_matmul(x, g, w, *, tm):
    n, d = x.shape
    e = w.shape[1]
    return pl.pallas_call(
        _norm_matmul_kernel,
        grid=(n // tm,),
        in_specs=[pl.BlockSpec((tm, d), lambda i: (i, 0)),
                  pl.BlockSpec((1, d), lambda i: (0, 0)),
                  pl.BlockSpec((d, e), lambda i: (0, 0), pipeline_mode=pl.Buffered(1))],
        out_specs=pl.BlockSpec((tm, e), lambda i: (i, 0)),
        out_shape=jax.ShapeDtypeStruct((n, e), BF16),
        compiler_params=_cparams("arbitrary"),
        name="norm_in_proj",
    )(x, g.reshape(1, d), w)


def _same_head_matrix(w):
    r = lax.broadcasted_iota(jnp.int32, (w, w), 0) // HEAD_DIM
    c = lax.broadcasted_iota(jnp.int32, (w, w), 1) // HEAD_DIM
    return (r == c).astype(BF16)


def _head_rms_scale(t, same_head):
    ssq = jnp.dot((t * t).astype(BF16), same_head, preferred_element_type=F32)
    return lax.rsqrt(ssq * (1.0 / HEAD_DIM) + EPS)


def _dil_attn_kernel(q_ref, k_ref, v_ref, qg_ref, kg_ref, bias_ref, o_ref, lse_ref,
                     qn_scr, kn_scr, v_scr, o_scr, *, seq, dilation):
    blk, d = DIL_BLOCK, dilation
    span = blk * d
    log2d = d.bit_length() - 1
    same_head = _same_head_matrix(DIL_OUT)
    qg = qg_ref[...] * (1.0 / math.sqrt(HEAD_DIM))
    kg = kg_ref[...]
    chunk = 256

    pairs = DIL_OUT // LANES
    pair_lanes = [slice(p * LANES, (p + 1) * LANES) for p in range(pairs)]
    for p in range(pairs):
        kn_scr[p, pl.ds(0, span), :] = jnp.zeros((span, LANES), F32)
        v_scr[p, pl.ds(0, span), :] = jnp.zeros((span, LANES), F32)

    def norm(i, c):
        rows = pl.ds(pl.multiple_of(i * chunk, chunk), chunk)
        prows = pl.ds(pl.multiple_of(span + i * chunk, blk), chunk)
        q = q_ref[0, rows, :].astype(F32)
        qn = q * _head_rms_scale(q, same_head) * qg
        k = k_ref[0, rows, :].astype(F32)
        kn = k * _head_rms_scale(k, same_head) * kg
        v = v_ref[0, rows, :].astype(F32)
        for p in range(pairs):
            qn_scr[p, rows, :] = qn[:, pair_lanes[p]]
            kn_scr[p, prows, :] = kn[:, pair_lanes[p]]
            v_scr[p, prows, :] = v[:, pair_lanes[p]]
        return c

    lax.fori_loop(0, seq // chunk, norm, 0)

    low_half = lax.broadcasted_iota(jnp.int32, (blk, LANES), 1) < HEAD_DIM

    def sub_rows(base, count):
        return pl.ds(pl.multiple_of(base, blk), count) if d == 1 else pl.ds(base, count, stride=d)

    def load(u):
        if d == 1:
            n, base = u, u * span
        else:
            n = u >> log2d
            base = (u & (d - 1)) + n * span
        q = [qn_scr[p, sub_rows(base, blk), :] for p in range(pairs)]
        kk = [kn_scr[p, sub_rows(base, 2 * blk), :].astype(BF16) for p in range(pairs)]
        vv = [v_scr[p, sub_rows(base, 2 * blk), :].astype(BF16) for p in range(pairs)]
        first = jnp.where(n == 0, 1, 0)
        return base, q, kk, vv, first

    def scores(unit):
        _, q, kk, _, first = unit
        out = []
        for h in range(DIL_HEADS_PER_GROUP):
            keep = low_half if h % 2 == 0 else jnp.logical_not(low_half)
            qh = jnp.where(keep, q[h // 2], 0.0).astype(BF16)
            s = lax.dot_general(qh, kk[h // 2], (((1,), (1,)), ((), ())), preferred_element_type=F32)
            out.append(s + bias_ref[h, first])
        return out

    def finish(unit, s_list):
        base, _, _, vv, _ = unit
        rows = sub_rows(base, blk)
        for p in range(pairs):
            oh, lh = [], []
            for hh in range(2):
                s = s_list[2 * p + hh]
                m = jnp.max(s, axis=-1, keepdims=True)
                e = jnp.exp(s - m)
                den = jnp.sum(e, axis=-1, keepdims=True)
                oh.append(jnp.dot(e.astype(BF16), vv[p], preferred_element_type=F32) * (1.0 / den))
                lh.append(m + jnp.log(den))
            o_scr[p, rows, :] = jnp.where(low_half, oh[0], oh[1])
            lse_ref[0, p, rows, :] = jnp.where(low_half, lh[0], lh[1])

    def body(it, c):
        units = [load(it * DIL_UNITS + i) for i in range(DIL_UNITS)]
        s = [None] * DIL_UNITS
        for step in range(DIL_UNITS + 1):
            if step < DIL_UNITS:
                s[step] = scores(units[step])
            if step >= 1:
                finish(units[step - 1], s[step - 1])
        return c

    lax.fori_loop(0, seq // blk // DIL_UNITS, body, 0)

    def emit(i, c):
        rows = pl.ds(pl.multiple_of(i * chunk, chunk), chunk)
        o_ref[0, rows, :] = jnp.concatenate([o_scr[p, rows, :] for p in range(pairs)],
                                            axis=-1).astype(o_ref.dtype)
        return c

    lax.fori_loop(0, seq // chunk, emit, 0)


def _dil_bias_table(group, window, dilation):
    reach = window // dilation
    assert reach <= DIL_BLOCK
    slopes = 2.0 ** (-ALIBI_MAX_BIAS * jnp.arange(1, DIL_HEADS + 1, dtype=F32) / DIL_HEADS)
    slopes = slopes[group * DIL_HEADS_PER_GROUP:(group + 1) * DIL_HEADS_PER_GROUP]
    qi = jnp.arange(DIL_BLOCK)[:, None] + DIL_BLOCK
    ki = jnp.arange(2 * DIL_BLOCK)[None, :]
    dist = qi - ki
    valid = (dist >= 0) & (dist <= reach)
    bias = -slopes[:, None, None] * (dist * dilation).astype(F32)[None]
    general = jnp.where(valid[None], bias, NEG_BIG)
    first = jnp.where((valid & (ki >= DIL_BLOCK))[None], bias, NEG_BIG)
    return jnp.stack([general, first], axis=1)


def _dil_attention(proj, q_gain, k_gain, group, window, dilation, bsz, seq):
    e = proj.shape[-1]
    assert (seq // DIL_BLOCK) % DIL_UNITS == 0
    assert seq % (DIL_BLOCK * dilation) == 0 and dilation & (dilation - 1) == 0
    view = proj.reshape(bsz, seq, e)
    w = DIL_OUT
    pairs = w // LANES
    pad = DIL_BLOCK * dilation

    def col(base):
        off = (base + group * w) // w
        return lambda b: (b, 0, off)

    bias = _dil_bias_table(group, window, dilation)
    gain = lambda g: jnp.tile(g, DIL_HEADS_PER_GROUP).reshape(1, w)
    o, lse = pl.pallas_call(
        functools.partial(_dil_attn_kernel, seq=seq, dilation=dilation),
        grid=(bsz,),
        in_specs=[pl.BlockSpec((1, seq, w), col(COL_DIL)),
                  pl.BlockSpec((1, seq, w), col(COL_DIL + DIL_WIDTH)),
                  pl.BlockSpec((1, seq, w), col(COL_DIL + 2 * DIL_WIDTH)),
                  pl.BlockSpec((1, w), lambda b: (0, 0)),
                  pl.BlockSpec((1, w), lambda b: (0, 0)),
                  pl.BlockSpec(bias.shape, lambda b: (0, 0, 0, 0))],
        out_specs=[pl.BlockSpec((1, seq, w), lambda b: (b, 0, 0)),
                   pl.BlockSpec((1, pairs, seq, LANES), lambda b: (b, 0, 0, 0))],
        out_shape=[jax.ShapeDtypeStruct((bsz, seq, w), BF16),
                   jax.ShapeDtypeStruct((bsz, pairs, seq, LANES), F32)],
        scratch_shapes=[pltpu.VMEM((pairs, seq, LANES), F32), pltpu.VMEM((pairs, pad + seq, LANES), F32),
                        pltpu.VMEM((pairs, pad + seq, LANES), F32), pltpu.VMEM((pairs, seq, LANES), F32)],
        compiler_params=_cparams("arbitrary"),
        name=f"dilated_attn_g{group}",
    )(view, view, view, gain(q_gain), gain(k_gain), bias)
    return o.reshape(bsz * seq, w), lse


def _sb_attn_kernel(q_ref, k_ref, v_ref, o_ref, vcat_scr, acc_scr, *, seq):
    tq, tk, nblk = SB_TQ, SB_TK, SB_TQ // SB_TK
    qi = pl.program_id(2)

    @pl.when(qi == 0)
    def _():
        chan = lax.broadcasted_iota(jnp.int32, (LANES, tk), 0)

        def build(kb, c):
            vt = v_ref[0, pl.ds(pl.multiple_of(kb * tk, tk), tk), :].astype(F32).T
            vcat_scr[kb] = jnp.concatenate(
                [jnp.where(chan < HEAD_DIM, vt, 0.0), jnp.where(chan >= HEAD_DIM, vt, 0.0)],
                axis=1).astype(BF16)
            return c

        lax.fori_loop(0, seq // tk, build, 0)

    lane = lax.broadcasted_iota(jnp.int32, (tq, LANES), 1)
    q = q_ref[0] * (1.0 / math.sqrt(HEAD_DIM))
    zero = jnp.zeros_like(q)
    qcat = jnp.concatenate([jnp.where(lane < HEAD_DIM, q, zero), jnp.where(lane >= HEAD_DIM, q, zero)],
                           axis=0)
    qcat_t = qcat.astype(F32).T.astype(BF16)
    neg_tri = jnp.where(lax.broadcasted_iota(jnp.int32, (tk, tk), 1)
                        >= lax.broadcasted_iota(jnp.int32, (tk, tk), 0), -1.0, 0.0).astype(BF16)
    acc_scr[...] = jnp.zeros_like(acc_scr)

    def scores(kb, rel):
        kk = k_ref[0, pl.ds(pl.multiple_of(kb * tk, tk), tk), :]
        zt = lax.dot_general(kk, qcat, (((1,), (1,)), ((), ())), preferred_element_type=F32)
        neg_abs = lax.bitcast_convert_type(
            lax.bitcast_convert_type(zt, jnp.uint32) | jnp.uint32(0x80000000), F32)
        sp = jnp.maximum(zt, 0.0) + jnp.log(1.0 + jnp.exp(neg_abs))
        before = None
        if rel is not None:
            kpos = lax.broadcasted_iota(jnp.int32, (tk, 2 * tq), 0) + rel * tk
            qpos = lax.broadcasted_iota(jnp.int32, (tk, 2 * tq), 1) & (tq - 1)
            before = kpos < qpos
            sp = jnp.where(before, sp, 0.0)
        return kk, zt[0:1, :], sp.astype(BF16), before

    def weights(state, run):
        kk, zt0, sp, before = state
        arg = jnp.dot(jnp.concatenate([neg_tri, kk], axis=1),
                      jnp.concatenate([sp, qcat_t], axis=0), preferred_element_type=F32)
        a = jnp.exp(arg - run)
        if before is not None:
            a = jnp.where(before, a, 0.0)
        a = a.astype(BF16)
        acat = jnp.concatenate([a[:, :tq], a[:, tq:]], axis=0)
        return acat, run + (zt0 - arg[0:1, :])

    def values(kb, acat):
        acc_scr[...] += jnp.dot(vcat_scr[kb], acat, preferred_element_type=F32)

    def run_blocks(blocks, run):
        n = len(blocks)
        st, ac = [None] * n, [None] * n
        for step in range(n + 2):
            if step < n:
                st[step] = scores(*blocks[step])
            if 0 <= step - 1 < n:
                ac[step - 1], run = weights(st[step - 1], run)
            if 0 <= step - 2 < n:
                values(blocks[step - 2][0], ac[step - 2])
        return run

    zero_run = jnp.zeros((1, 2 * tq), F32)
    diagonal = [(qi * nblk + rel, rel) for rel in reversed(range(nblk))]
    n_chunks = qi * (nblk // SB_CHUNK)

    def chunk_blocks(it):
        base = (n_chunks - 1 - it) * SB_CHUNK
        return [(base + j, None) for j in reversed(range(SB_CHUNK))]

    @pl.when(qi == 0)
    def _():
        run_blocks(diagonal, zero_run)

    @pl.when(qi > 0)
    def _():
        run = run_blocks(diagonal + chunk_blocks(0), zero_run)

        def alive(run):
            return (jnp.min(run) < SB_DEAD_RUN).astype(jnp.int32)

        def more(carry):
            it, _, go = carry
            return jnp.logical_and(it < n_chunks, go > 0)

        def chunk(carry):
            it, run, _ = carry
            run = run_blocks(chunk_blocks(it), run)
            return it + 1, run, alive(run)

        lax.while_loop(more, chunk, (jnp.int32(1), run, alive(run)))

    o_ref[0] = acc_scr[...].T.astype(o_ref.dtype)


def _sb_attention(proj, bsz, seq):
    e = proj.shape[-1]
    view = proj.reshape(bsz, seq, e)
    pairs = SB_WIDTH // LANES
    qo, ko, vo = COL_SB // LANES, (COL_SB + SB_WIDTH) // LANES, (COL_SB + 2 * SB_WIDTH) // LANES
    out = pl.pallas_call(
        functools.partial(_sb_attn_kernel, seq=seq),
        grid=(bsz, pairs, seq // SB_TQ),
        in_specs=[pl.BlockSpec((1, SB_TQ, LANES), lambda b, p, i: (b, i, qo + p)),
                  pl.BlockSpec((1, seq, LANES), lambda b, p, i: (b, 0, ko + p)),
                  pl.BlockSpec((1, seq, LANES), lambda b, p, i: (b, 0, vo + p))],
        out_specs=pl.BlockSpec((1, SB_TQ, LANES), lambda b, p, i: (b, i, p)),
        out_shape=jax.ShapeDtypeStruct((bsz, seq, SB_WIDTH), BF16),
        scratch_shapes=[pltpu.VMEM((seq // SB_TK, LANES, 2 * SB_TK), BF16),
                        pltpu.VMEM((LANES, SB_TQ), F32)],
        compiler_params=_cparams("arbitrary", "arbitrary", "arbitrary"),
        name="stick_breaking_attn",
    )(view, view, view)
    return out.reshape(bsz * seq, SB_WIDTH)


def _conv_kernel(val_ref, gate_ref, w_ref, b_ref, g_ref, beta_ref, o_ref, u_scr, y_scr, *, seq):
    tr, pad, slabs = CONV_ROWS, CONV_PAD, CONV_CH // LANES
    lanes = [slice(j * LANES, (j + 1) * LANES) for j in range(slabs)]
    for j in range(slabs):
        u_scr[j, pl.ds(0, pad), :] = jnp.zeros((pad, LANES), F32)

    def glu(i, c):
        rows = pl.ds(pl.multiple_of(i * 256, 256), 256)
        val = val_ref[0, rows, :].astype(F32)
        gate = gate_ref[0, rows, :].astype(F32)
        u = val * _sigmoid(gate)
        for j in range(slabs):
            u_scr[j, pl.ds(pl.multiple_of(pad + i * 256, SUBLANES), 256), :] = u[:, lanes[j]]
        return c

    lax.fori_loop(0, seq // 256, glu, 0)

    first = pad - (CONV_WIDTH - 1)

    def tile(i, c):
        t0 = pl.multiple_of(i * tr, tr)
        acc = [[jnp.zeros((tr // 2, LANES), F32) for _ in range(slabs)] for _ in range(2)]
        for w in range(CONV_WIDTH):
            for j in range(slabs):
                tap = jnp.broadcast_to(w_ref[w:w + 1, lanes[j]], (tr // 2, LANES))
                for phase in range(2):
                    rows = pl.ds(t0 + (first + w + phase), tr // 2, stride=2)
                    acc[phase][j] = acc[phase][j] + u_scr[j, rows, :] * tap
        for phase in range(2):
            y = jnp.concatenate(acc[phase], axis=-1) + b_ref[...]
            mu = jnp.mean(y, axis=-1, keepdims=True)
            yc = y - mu
            var = jnp.mean(yc * yc, axis=-1, keepdims=True)
            yn = yc * lax.rsqrt(var + EPS) * g_ref[...] + beta_ref[...]
            out = yn * _sigmoid(yn)
            for j in range(slabs):
                y_scr[j, pl.ds(phase, tr // 2, stride=2), :] = out[:, lanes[j]]
        o_ref[0, pl.ds(t0, tr), :] = jnp.concatenate([y_scr[j] for j in range(slabs)],
                                                     axis=-1).astype(o_ref.dtype)
        return c

    lax.fori_loop(0, seq // tr, tile, 0)


def _conformer_conv(proj, conv_w, conv_b, norm_g, norm_b, bsz, seq):
    e = proj.shape[-1]
    view = proj.reshape(bsz, seq, e)
    c = CONV_CH
    voff, goff = COL_GLU // c, (COL_GLU + c) // c
    const = lambda b: (0, 0)
    out = pl.pallas_call(
        functools.partial(_conv_kernel, seq=seq),
        grid=(bsz,),
        in_specs=[pl.BlockSpec((1, seq, c), lambda b: (b, 0, voff)),
                  pl.BlockSpec((1, seq, c), lambda b: (b, 0, goff)),
                  pl.BlockSpec((CONV_WIDTH, c), const),
                  pl.BlockSpec((1, c), const), pl.BlockSpec((1, c), const), pl.BlockSpec((1, c), const)],
        out_specs=pl.BlockSpec((1, seq, c), lambda b: (b, 0, 0)),
        out_shape=jax.ShapeDtypeStruct((bsz, seq, c), BF16),
        scratch_shapes=[pltpu.VMEM((c // LANES, seq + CONV_PAD, LANES), F32),
                        pltpu.VMEM((c // LANES, CONV_ROWS, LANES), F32)],
        compiler_params=_cparams("arbitrary"),
        name="conformer_conv",
    )(view, view, conv_w, conv_b.reshape(1, c), norm_g.reshape(1, c), norm_b.reshape(1, c))
    return out.reshape(bsz * seq, c)


def _merge_kernel(o1_ref, o2_ref, o3_ref, l1_ref, l2_ref, l3_ref, ob_ref, oc_ref,
                  ga_ref, gb_ref, gc_ref, x_ref, wa_ref, wb_ref, wc_ref, wo_ref, out_ref):
    by_lanes = lambda ref: jnp.concatenate([ref[p] for p in range(ref.shape[0])], axis=-1)
    l1, l2, l3 = by_lanes(l1_ref), by_lanes(l2_ref), by_lanes(l3_ref)
    m = jnp.maximum(jnp.maximum(l1, l2), l3)
    e1, e2, e3 = jnp.exp(l1 - m), jnp.exp(l2 - m), jnp.exp(l3 - m)
    o_a = (e1 * o1_ref[...].astype(F32) + e2 * o2_ref[...].astype(F32)
           + e3 * o3_ref[...].astype(F32)) / (e1 + e2 + e3)
    ya = jnp.dot(o_a.astype(BF16), wa_ref[...], preferred_element_type=F32)
    yb = jnp.dot(ob_ref[...], wb_ref[...], preferred_element_type=F32)
    yc = jnp.dot(oc_ref[...], wc_ref[...], preferred_element_type=F32)
    merged = (_sigmoid(ga_ref[...].astype(F32)) * ya + _sigmoid(gb_ref[...].astype(F32)) * yb
              + _sigmoid(gc_ref[...].astype(F32)) * yc)
    out_ref[...] = x_ref[...] + jnp.dot(merged.astype(BF16), wo_ref[...], preferred_element_type=F32)


def _merge(o_groups, lse_groups, o_b, o_c, proj, x, wa, wb, wc, wo, *, tm):
    n, d = x.shape
    row = lambda w, j=0: pl.BlockSpec((tm, w), lambda i: (i, j))
    full = lambda a: pl.BlockSpec(a.shape, lambda i: (0, 0), pipeline_mode=pl.Buffered(1))
    g0 = COL_GATES // d
    _, pairs, seq, _ = lse_groups[0].shape
    per_batch = seq // tm
    lse = pl.BlockSpec((None, pairs, tm, LANES), lambda i: (i // per_batch, 0, i % per_batch, 0))
    return pl.pallas_call(
        _merge_kernel,
        grid=(n // tm,),
        in_specs=[row(DIL_OUT), row(DIL_OUT), row(DIL_OUT), lse, lse, lse,
                  row(SB_WIDTH), row(CONV_CH), row(d, g0), row(d, g0 + 1), row(d, g0 + 2), row(d),
                  full(wa), full(wb), full(wc), full(wo)],
        out_specs=row(d),
        out_shape=jax.ShapeDtypeStruct((n, d), F32),
        compiler_params=_cparams("arbitrary"),
        name="branch_merge_out_proj",
    )(*o_groups, *lse_groups, o_b, o_c, proj, proj, proj, x, wa, wb, wc, wo)


MXU_COLS = 256


def _col_pieces(lo, hi):
    return [(p, min(p + MXU_COLS, hi)) for p in range(lo, hi, MXU_COLS)]


FF_CHUNKS = ((0, FF_SPLIT), (FF_SPLIT, D_FF))
SWIGLU_HOOKS = sum(2 * len(_col_pieces(lo, hi)) + len(_col_pieces(0, D_MODEL)) for lo, hi in FF_CHUNKS)


def _swiglu_tile(h, wg_ref, wu_ref, wd_ref, hook=None):
    def matmul(x, w_ref, rows, lo, hi):
        if hook is None:
            return jnp.dot(x, w_ref[rows, lo:hi], preferred_element_type=F32)
        pieces = []
        for p, q in _col_pieces(lo, hi):
            pieces.append(jnp.dot(x, w_ref[rows, p:q], preferred_element_type=F32))
            hook()
        return jnp.concatenate(pieces, axis=-1)

    y = None
    for lo, hi in FF_CHUNKS:
        a = matmul(h, wg_ref, slice(None), lo, hi)
        u = matmul(h, wu_ref, slice(None), lo, hi)
        act = (a * _sigmoid(a) * u).astype(BF16)
        part = matmul(act, wd_ref, slice(lo, hi), 0, D_MODEL)
        y = part if y is None else y + part
    return y


def _dense_ffn_kernel(x_ref, g_ref, wg_ref, wu_ref, wd_ref, o_ref):
    x = x_ref[...]
    ms = jnp.mean(x * x, axis=-1, keepdims=True)
    h = ((x * lax.rsqrt(ms + EPS)) * g_ref[...]).astype(BF16)
    o_ref[...] = x + _swiglu_tile(h, wg_ref, wu_ref, wd_ref)


def _dense_ffn(x, g, wg, wu, wd, *, tm):
    n, d = x.shape
    full = lambda a: pl.BlockSpec(a.shape, lambda i: (0, 0), pipeline_mode=pl.Buffered(1))
    return pl.pallas_call(
        _dense_ffn_kernel,
        grid=(n // tm,),
        in_specs=[pl.BlockSpec((tm, d), lambda i: (i, 0)), pl.BlockSpec((1, d), lambda i: (0, 0)),
                  full(wg), full(wu), full(wd)],
        out_specs=pl.BlockSpec((tm, d), lambda i: (i, 0)),
        out_shape=jax.ShapeDtypeStruct((n, d), F32),
        compiler_params=_cparams("arbitrary"),
        name="dense_swiglu",
    )(x, g.reshape(1, d), wg, wu, wd)


def _router_kernel(x_ref, g_ref, wr_ref, br_ref, h_ref, route_ref):
    x = x_ref[...]
    ms = jnp.mean(x * x, axis=-1, keepdims=True)
    h = (x * lax.rsqrt(ms + EPS)) * g_ref[...]
    _store_token_tiles(h_ref, h)
    h_hi = h.astype(BF16)
    h_lo = (h - h_hi.astype(F32)).astype(BF16)
    w = wr_ref[...]
    w_hi = w.astype(BF16)
    w_lo = (w - w_hi.astype(F32)).astype(BF16)
    logits = (jnp.dot(h_hi, w_hi, preferred_element_type=F32)
              + jnp.dot(h_lo, w_hi, preferred_element_type=F32)
              + jnp.dot(h_hi, w_lo, preferred_element_type=F32)) + br_ref[...]
    lane = lax.broadcasted_iota(jnp.int32, logits.shape, 1)
    m1 = jnp.max(logits, axis=-1, keepdims=True)
    i1 = jnp.min(jnp.where(logits == m1, lane, LANES), axis=-1, keepdims=True)
    rest = jnp.where(lane == i1, -jnp.inf, logits)
    m2 = jnp.max(rest, axis=-1, keepdims=True)
    i2 = jnp.min(jnp.where(rest == m2, lane, LANES), axis=-1, keepdims=True)
    e2 = jnp.exp(m2 - m1)
    g1 = 1.0 / (1.0 + e2)
    g2 = e2 / (1.0 + e2)
    route = jnp.where(lane == 0, i1.astype(F32),
                      jnp.where(lane == 1, i2.astype(F32),
                                jnp.where(lane == 2, g1, jnp.where(lane == 3, g2, 0.0))))
    route_ref[...] = route


def _router(x, g, w_router, b_router, *, tm):
    n, d = x.shape
    wr = jnp.zeros((d, LANES), F32).at[:, :N_EXPERTS].set(w_router)
    br = jnp.full((1, LANES), NEG_BIG, F32).at[0, :N_EXPERTS].set(b_router)
    return pl.pallas_call(
        _router_kernel,
        grid=(n // tm,),
        in_specs=[pl.BlockSpec((tm, d), lambda i: (i, 0)), pl.BlockSpec((1, d), lambda i: (0, 0)),
                  pl.BlockSpec((d, LANES), lambda i: (0, 0)), pl.BlockSpec((1, LANES), lambda i: (0, 0))],
        out_specs=[pl.BlockSpec((tm * ROW_TILE, LANES), lambda i: (i, 0)),
                   pl.BlockSpec((tm, LANES), lambda i: (i, 0))],
        out_shape=[jax.ShapeDtypeStruct((n * ROW_TILE, LANES), F32), jax.ShapeDtypeStruct((n, LANES), F32)],
        compiler_params=_cparams("arbitrary"),
        name="router_top2",
    )(x, g.reshape(1, d), wr, br)


ROW_TILE = D_MODEL // LANES


def _store_token_tiles(ref, x):
    rows = x.shape[0]
    for c in range(ROW_TILE):
        ref[pl.ds(c, rows, stride=ROW_TILE), :] = x[:, c * LANES:(c + 1) * LANES]


def _load_token_tiles(ref, rows):
    return jnp.concatenate([ref[pl.ds(c, rows, stride=ROW_TILE), :] for c in range(ROW_TILE)], axis=-1)


def _tile_rows(t):
    start = t * ROW_TILE
    return pl.ds(start if isinstance(start, int) else pl.multiple_of(start, ROW_TILE), ROW_TILE)


class _BlockRows:
    def __init__(self, dst_ref, n_tokens):
        self.dst_ref = dst_ref
        self.last_token = n_tokens - 1

    def gather_row(self, blk, r, h_hbm, xbuf, sem):
        tok = lax.shift_right_logical(self.dst_ref[blk, r], TOP_K.bit_length() - 1)
        tok = jnp.minimum(tok, self.last_token)
        pltpu.make_async_copy(h_hbm.at[_tile_rows(tok), :], xbuf.at[_tile_rows(r), :],
                              sem).start(priority=self._queue(r))

    def scatter_row(self, blk, r, ybuf, y_hbm, sem):
        pltpu.make_async_copy(ybuf.at[_tile_rows(r), :], y_hbm.at[_tile_rows(self.dst_ref[blk, r]), :],
                              sem).start(priority=self._queue(r))

    @staticmethod
    def _queue(r):
        return r % 2 if isinstance(r, int) else 0

    def all_rows(self, row_fn):
        def body(g, c):
            for j in range(SUBLANES):
                row_fn(g * SUBLANES + j)
            return c

        lax.fori_loop(0, MOE_TM // SUBLANES, body, 0)

    @staticmethod
    def wait_gather(h_hbm, xbuf, sem):
        pltpu.make_async_copy(h_hbm.at[pl.ds(0, MOE_TM * ROW_TILE), :], xbuf, sem).wait()

    @staticmethod
    def wait_scatter(ybuf, y_hbm, sem):
        pltpu.make_async_copy(ybuf, y_hbm.at[pl.ds(0, MOE_TM * ROW_TILE), :], sem).wait()


def _expert_kernel(be_ref, used_ref, dst_ref, h_hbm, wg_ref, wu_ref, wd_ref, y_hbm,
                   xbuf, ybuf, gsem, ssem, *, n_tokens):
    i = pl.program_id(0)
    used = used_ref[0]
    slot = i % 2
    rows = _BlockRows(dst_ref, n_tokens)
    gather_hooks = SWIGLU_HOOKS // 2
    share = lambda k, hooks: range(k * MOE_TM // hooks, (k + 1) * MOE_TM // hooks)

    def compute(scatter_previous):
        _BlockRows.wait_gather(h_hbm, xbuf.at[slot], gsem.at[slot])
        x = _load_token_tiles(xbuf.at[slot], MOE_TM).astype(BF16)
        nxt = jnp.minimum(i + 1, used - 1)
        calls = [0]

        def start_some():
            k = calls[0]
            calls[0] += 1
            if k < gather_hooks:
                for r in share(k, gather_hooks):
                    rows.gather_row(nxt, r, h_hbm, xbuf.at[1 - slot], gsem.at[1 - slot])
            if scatter_previous:
                for r in share(k, SWIGLU_HOOKS):
                    rows.scatter_row(i - 1, r, ybuf.at[1 - slot], y_hbm, ssem.at[1 - slot])

        y = _swiglu_tile(x, wg_ref, wu_ref, wd_ref, hook=start_some)
        assert calls[0] == SWIGLU_HOOKS

        @pl.when(i >= 2)
        def _():
            _BlockRows.wait_scatter(ybuf.at[slot], y_hbm, ssem.at[slot])

        _store_token_tiles(ybuf.at[slot], y)

    @pl.when(i == 0)
    def _():
        rows.all_rows(lambda r: rows.gather_row(0, r, h_hbm, xbuf.at[0], gsem.at[0]))
        ybuf[1] = jnp.zeros((MOE_TM * ROW_TILE, LANES), F32)
        for half in range(2):
            clear = pltpu.make_async_copy(
                ybuf.at[1],
                y_hbm.at[pl.ds((n_tokens * TOP_K + half * MOE_TM) * ROW_TILE, MOE_TM * ROW_TILE), :],
                ssem.at[half])
            clear.start()
            clear.wait()
        compute(False)

    @pl.when(jnp.logical_and(i >= 1, i < used))
    def _():
        compute(True)

    @pl.when(i == used)
    def _():
        _BlockRows.wait_gather(h_hbm, xbuf.at[slot], gsem.at[slot])
        _BlockRows.wait_scatter(ybuf.at[slot], y_hbm, ssem.at[slot])
        rows.all_rows(lambda r: rows.scatter_row(i - 1, r, ybuf.at[1 - slot], y_hbm, ssem.at[1 - slot]))
        _BlockRows.wait_scatter(ybuf.at[1 - slot], y_hbm, ssem.at[1 - slot])


def _experts(h, block_expert, used, row_dst, wg, wu, wd, n_tokens):
    d = D_MODEL
    n_blocks = row_dst.shape[0]
    f = wg.shape[-1]
    assert n_tokens * TOP_K >= 2 * MOE_TM
    one = pl.Buffered(1)
    grid_spec = pltpu.PrefetchScalarGridSpec(
        num_scalar_prefetch=3,
        grid=(n_blocks,),
        in_specs=[pl.BlockSpec(memory_space=pl.ANY),
                  pl.BlockSpec((None, d, f), lambda i, be, us, ds: (be[i], 0, 0), pipeline_mode=one),
                  pl.BlockSpec((None, d, f), lambda i, be, us, ds: (be[i], 0, 0), pipeline_mode=one),
                  pl.BlockSpec((None, f, d), lambda i, be, us, ds: (be[i], 0, 0), pipeline_mode=one)],
        out_specs=pl.BlockSpec(memory_space=pl.ANY),
        scratch_shapes=[pltpu.VMEM((2, MOE_TM * ROW_TILE, LANES), F32),
                        pltpu.VMEM((2, MOE_TM * ROW_TILE, LANES), F32),
                        pltpu.SemaphoreType.DMA((2,)), pltpu.SemaphoreType.DMA((2,))],
    )
    return pl.pallas_call(
        functools.partial(_expert_kernel, n_tokens=n_tokens),
        grid_spec=grid_spec,
        out_shape=jax.ShapeDtypeStruct(((n_tokens * TOP_K + 2 * MOE_TM) * ROW_TILE, LANES), F32),
        compiler_params=_cparams("arbitrary"),
        name="expert_swiglu",
    )(block_expert, used, row_dst, h, wg, wu, wd)


def _combine_kernel(y_ref, x_ref, route_ref, o_ref):
    tt = x_ref.shape[0]
    route = route_ref[...]
    out = x_ref[...]
    for k in range(TOP_K):
        yk = jnp.concatenate([y_ref[pl.ds(k * ROW_TILE + c, tt, stride=TOP_K * ROW_TILE), :]
                              for c in range(ROW_TILE)], axis=-1)
        out = out + route[:, TOP_K + k:TOP_K + k + 1] * yk
    o_ref[...] = out


def _combine(x, ys, route):
    n, d = x.shape
    tt = COMBINE_TT
    return pl.pallas_call(
        _combine_kernel,
        grid=(n // tt,),
        in_specs=[pl.BlockSpec((tt * TOP_K * ROW_TILE, LANES), lambda i: (i, 0)),
                  pl.BlockSpec((tt, d), lambda i: (i, 0)),
                  pl.BlockSpec((tt, LANES), lambda i: (i, 0))],
        out_specs=pl.BlockSpec((tt, d), lambda i: (i, 0)),
        out_shape=jax.ShapeDtypeStruct((n, d), F32),
        compiler_params=_cparams("arbitrary"),
        name="expert_combine",
    )(ys, x, route)


def _routed_ffn(x, g, w_router, b_router, wg, wu, wd, *, tm):
    n, d = x.shape
    h, route = _router(x, g, w_router, b_router, tm=tm)
    e_flat = route[:, :TOP_K].astype(jnp.int32).reshape(-1)
    onehot = (e_flat[:, None] == jnp.arange(N_EXPERTS)[None, :]).astype(jnp.int32)
    csum = jnp.cumsum(onehot, axis=0)
    counts = csum[-1]
    rank = jnp.sum((csum - onehot) * onehot, axis=1)
    padded = (counts + MOE_TM - 1) // MOE_TM * MOE_TM
    pend = jnp.cumsum(padded)
    pstart = pend - padded
    dest = (pstart[e_flat] + rank).astype(jnp.int32)
    n_blocks = (n * TOP_K) // MOE_TM + N_EXPERTS
    scratch_rows = n * TOP_K + jnp.arange(n_blocks * MOE_TM, dtype=jnp.int32) % (2 * MOE_TM)
    row_dst = scratch_rows.at[dest].set(jnp.arange(n * TOP_K, dtype=jnp.int32)).reshape(n_blocks, MOE_TM)
    block_expert = jnp.clip(jnp.searchsorted(pend, jnp.arange(n_blocks) * MOE_TM, side='right'),
                            0, N_EXPERTS - 1).astype(jnp.int32)
    used = (pend[-1:] // MOE_TM).astype(jnp.int32)
    ys = _experts(h, block_expert, used, row_dst, wg, wu, wd, n)
    return _combine(x, ys, route)


def kernel(x, attn_norm_g, w_in, q_norm_g, k_norm_g, conv_w, conv_b, conv_norm_g, conv_norm_b,
           w_branch_a, w_branch_b, w_branch_c, w_out, ffn_norm_g, w_ffn_gate, w_ffn_up,
           w_ffn_down, w_router, b_router, w_exp_gate, w_exp_up, w_exp_down):
    bsz, seq, d = x.shape
    depth = attn_norm_g.shape[0]
    n = bsz * seq
    tm = 512
    xf = x.reshape(n, d)
    s_dil, s_sb, s_glu = 3 * DIL_WIDTH, 3 * DIL_WIDTH + 3 * SB_WIDTH, 3 * DIL_WIDTH + 3 * SB_WIDTH + 2 * CONV_CH
    for layer in range(depth):
        wl = w_in[layer]
        w_perm = jnp.concatenate([wl[:, s_glu:], wl[:, s_sb:s_glu], wl[:, s_dil:s_sb], wl[:, :s_dil]],
                                 axis=1).astype(BF16)
        proj_a = _norm_matmul(xf, attn_norm_g[layer], w_perm[:, :PROJ_A_COLS], tm=tm)
        proj_b = _norm_matmul(xf, attn_norm_g[layer], w_perm[:, PROJ_A_COLS:], tm=tm)
        o_groups, lse_groups = [], []
        for gi, (window, dilation) in enumerate(DIL_GROUPS):
            o_g, lse_g = _dil_attention(proj_b, q_norm_g[layer], k_norm_g[layer], gi, window, dilation,
                                        bsz, seq)
            o_groups.append(o_g)
            lse_groups.append(lse_g)
        o_b = _sb_attention(proj_b, bsz, seq)
        o_c = _conformer_conv(proj_a, conv_w[layer], conv_b[layer], conv_norm_g[layer],
                              conv_norm_b[layer], bsz, seq)
        xf = _merge(o_groups, lse_groups, o_b, o_c, proj_a, xf,
                    w_branch_a[layer].astype(BF16), w_branch_b[layer].astype(BF16),
                    w_branch_c[layer].astype(BF16), w_out[layer].astype(BF16), tm=tm)
        i = layer // 2
        if layer % 2 == 0:
            xf = _dense_ffn(xf, ffn_norm_g[layer], w_ffn_gate[i].astype(BF16), w_ffn_up[i].astype(BF16),
                            w_ffn_down[i].astype(BF16), tm=tm)
        else:
            xf = _routed_ffn(xf, ffn_norm_g[layer], w_router[i], b_router[i],
                             w_exp_gate[i].astype(BF16), w_exp_up[i].astype(BF16),
                             w_exp_down[i].astype(BF16), tm=tm)
    return xf.reshape(bsz, seq, d)
```

```python
import functools
import math

import jax
import jax.numpy as jnp
from jax import lax
from jax.experimental import pallas as pl
from jax.experimental.pallas import tpu as pltpu

F32 = jnp.float32
BF16 = jnp.bfloat16

D_MODEL = 1024
HEAD_DIM = 64
DIL_GROUPS = ((128, 1), (512, 4), (2048, 16))
DIL_HEADS_PER_GROUP = 4
DIL_HEADS = len(DIL_GROUPS) * DIL_HEADS_PER_GROUP
DIL_WIDTH = DIL_HEADS * HEAD_DIM
DIL_OUT = DIL_HEADS_PER_GROUP * HEAD_DIM
DIL_BLOCK = 128
SB_HEADS = 8
SB_WIDTH = SB_HEADS * HEAD_DIM
CONV_CH = D_MODEL // 2
CONV_WIDTH = 31
N_BRANCH = 3
IN_COLS = 3 * DIL_WIDTH + 3 * SB_WIDTH + 2 * CONV_CH + N_BRANCH * D_MODEL
D_FF = 2816
N_EXPERTS = 8
TOP_K = 2
EPS = 1e-6
ALIBI_MAX_BIAS = 8.0
NEG_BIG = -1e30

COL_GATES = 0
COL_GLU = COL_GATES + N_BRANCH * D_MODEL
PROJ_A_COLS = COL_GLU + 2 * CONV_CH
COL_SB = 0
COL_DIL = COL_SB + 3 * SB_WIDTH

LANES = 128
SUBLANES = 8
VMEM_LIMIT = 56 * 1024 * 1024

DIL_UNITS = 4
SB_TQ = 512
SB_TK = 128
SB_PAIRS = 2
SB_CHUNK = 2
SB_DEAD_RUN = 128.0
CONV_ROWS = 64
CONV_PAD = 32
FF_SPLIT = 1536
MOE_TM = 512
COMBINE_TT = 256


def _cparams(*sem):
    return pltpu.CompilerParams(dimension_semantics=sem, vmem_limit_bytes=VMEM_LIMIT)


def _sigmoid(x):
    return 1.0 / (1.0 + jnp.exp(-x))


def _norm_matmul_kernel(x_ref, g_ref, w_ref, o_ref):
    x = x_ref[...]
    ms = jnp.mean(x * x, axis=-1, keepdims=True)
    h = (x * lax.rsqrt(ms + EPS)) * g_ref[...]
    o_ref[...] = jnp.dot(h.astype(BF16), w_ref[...], preferred_element_type=F32).astype(o_ref.dtype)


def _norm_matmul(x, g, w, *, tm):
    n, d = x.shape
    e = w.shape[1]
    return pl.pallas_call(
        _norm_matmul_kernel,
        grid=(n // tm,),
        in_specs=[pl.BlockSpec((tm, d), lambda i: (i, 0)),
                  pl.BlockSpec((1, d), lambda i: (0, 0)),
                  pl.BlockSpec((d, e), lambda i: (0, 0), pipeline_mode=pl.Buffered(1))],
        out_specs=pl.BlockSpec((tm, e), lambda i: (i, 0)),
        out_shape=jax.ShapeDtypeStruct((n, e), BF16),
        compiler_params=_cparams("arbitrary"),
        name="norm_in_proj",
    )(x, g.reshape(1, d), w)


def _same_head_matrix(w):
    r = lax.broadcasted_iota(jnp.int32, (w, w), 0) // HEAD_DIM
    c = lax.broadcasted_iota(jnp.int32, (w, w), 1) // HEAD_DIM
    return (r == c).astype(BF16)


def _head_rms_scale(t, same_head):
    ssq = jnp.dot((t * t).astype(BF16), same_head, preferred_element_type=F32)
    return lax.rsqrt(ssq * (1.0 / HEAD_DIM) + EPS)


def _dil_attn_kernel(q_ref, k_ref, v_ref, qg_ref, kg_ref, bias_ref, o_ref, lse_ref,
                     qn_scr, kn_scr, v_scr, o_scr, *, seq, dilation):
    blk, d = DIL_BLOCK, dilation
    span = blk * d
    log2d = d.bit_length() - 1
    same_head = _same_head_matrix(DIL_OUT)
    qg = qg_ref[...] * (1.0 / math.sqrt(HEAD_DIM))
    kg = kg_ref[...]
    chunk = 256

    pairs = DIL_OUT // LANES
    pair_lanes = [slice(p * LANES, (p + 1) * LANES) for p in range(pairs)]
    for p in range(pairs):
        kn_scr[p, pl.ds(0, span), :] = jnp.zeros((span, LANES), F32)
        v_scr[p, pl.ds(0, span), :] = jnp.zeros((span, LANES), F32)

    def norm(i, c):
        rows = pl.ds(pl.multiple_of(i * chunk, chunk), chunk)
        prows = pl.ds(pl.multiple_of(span + i * chunk, blk), chunk)
        q = q_ref[0, rows, :].astype(F32)
        qn = q * _head_rms_scale(q, same_head) * qg
        k = k_ref[0, rows, :].astype(F32)
        kn = k * _head_rms_scale(k, same_head) * kg
        v = v_ref[0, rows, :].astype(F32)
        for p in range(pairs):
            qn_scr[p, rows, :] = qn[:, pair_lanes[p]]
            kn_scr[p, prows, :] = kn[:, pair_lanes[p]]
            v_scr[p, prows, :] = v[:, pair_lanes[p]]
        return c

    lax.fori_loop(0, seq // chunk, norm, 0)

    low_half = lax.broadcasted_iota(jnp.int32, (blk, LANES), 1) < HEAD_DIM

    def sub_rows(base, count):
        return pl.ds(pl.multiple_of(base, blk), count) if d == 1 else pl.ds(base, count, stride=d)

    def load(u):
        if d == 1:
            n, base = u, u * span
        else:
            n = u >> log2d
            base = (u & (d - 1)) + n * span
        q = [qn_scr[p, sub_rows(base, blk), :] for p in range(pairs)]
        kk = [kn_scr[p, sub_rows(base, 2 * blk), :].astype(BF16) for p in range(pairs)]
        vv = [v_scr[p, sub_rows(base, 2 * blk), :].astype(BF16) for p in range(pairs)]
        first = jnp.where(n == 0, 1, 0)
        return base, q, kk, vv, first

    def scores(unit):
        _, q, kk, _, first = unit
        out = []
        for h in range(DIL_HEADS_PER_GROUP):
            keep = low_half if h % 2 == 0 else jnp.logical_not(low_half)
            qh = jnp.where(keep, q[h // 2], 0.0).astype(BF16)
            s = lax.dot_general(qh, kk[h // 2], (((1,), (1,)), ((), ())), preferred_element_type=F32)
            out.append(s + bias_ref[h, first])
        return out

    def finish(unit, s_list):
        base, _, _, vv, _ = unit
        rows = sub_rows(base, blk)
        for p in range(pairs):
            oh, lh = [], []
            for hh in range(2):
                s = s_list[2 * p + hh]
                m = jnp.max(s, axis=-1, keepdims=True)
                e = jnp.exp(s - m)
                den = jnp.sum(e, axis=-1, keepdims=True)
                oh.append(jnp.dot(e.astype(BF16), vv[p], preferred_element_type=F32) * (1.0 / den))
                lh.append(m + jnp.log(den))
            o_scr[p, rows, :] = jnp.where(low_half, oh[0], oh[1])
            lse_ref[0, p, rows, :] = jnp.where(low_half, lh[0], lh[1])

    def body(it, c):
        units = [load(it * DIL_UNITS + i) for i in range(DIL_UNITS)]
        s = [None] * DIL_UNITS
        for step in range(DIL_UNITS + 1):
            if step < DIL_UNITS:
                s[step] = scores(units[step])
            if step >= 1:
                finish(units[step - 1], s[step - 1])
        return c

    lax.fori_loop(0, seq // blk // DIL_UNITS, body, 0)

    def emit(i, c):
        rows = pl.ds(pl.multiple_of(i * chunk, chunk), chunk)
        o_ref[0, rows, :] = jnp.concatenate([o_scr[p, rows, :] for p in range(pairs)],
                                            axis=-1).astype(o_ref.dtype)
        return c

    lax.fori_loop(0, seq // chunk, emit, 0)


def _dil_bias_table(group, window, dilation):
    reach = window // dilation
    assert reach <= DIL_BLOCK
    slopes = 2.0 ** (-ALIBI_MAX_BIAS * jnp.arange(1, DIL_HEADS + 1, dtype=F32) / DIL_HEADS)
    slopes = slopes[group * DIL_HEADS_PER_GROUP:(group + 1) * DIL_HEADS_PER_GROUP]
    qi = jnp.arange(DIL_BLOCK)[:, None] + DIL_BLOCK
    ki = jnp.arange(2 * DIL_BLOCK)[None, :]
    dist = qi - ki
    valid = (dist >= 0) & (dist <= reach)
    bias = -slopes[:, None, None] * (dist * dilation).astype(F32)[None]
    general = jnp.where(valid[None], bias, NEG_BIG)
    first = jnp.where((valid & (ki >= DIL_BLOCK))[None], bias, NEG_BIG)
    return jnp.stack([general, first], axis=1)


def _dil_attention(proj, q_gain, k_gain, group, window, dilation, bsz, seq):
    e = proj.shape[-1]
    assert (seq // DIL_BLOCK) % DIL_UNITS == 0
    assert seq % (DIL_BLOCK * dilation) == 0 and dilation & (dilation - 1) == 0
    view = proj.reshape(bsz, seq, e)
    w = DIL_OUT
    pairs = w // LANES
    pad = DIL_BLOCK * dilation

    def col(base):
        off = (base + group * w) // w
        return lambda b: (b, 0, off)

    bias = _dil_bias_table(group, window, dilation)
    gain = lambda g: jnp.tile(g, DIL_HEADS_PER_GROUP).reshape(1, w)
    o, lse = pl.pallas_call(
        functools.partial(_dil_attn_kernel, seq=seq, dilation=dilation),
        grid=(bsz,),
        in_specs=[pl.BlockSpec((1, seq, w), col(COL_DIL)),
                  pl.BlockSpec((1, seq, w), col(COL_DIL + DIL_WIDTH)),
                  pl.BlockSpec((1, seq, w), col(COL_DIL + 2 * DIL_WIDTH)),
                  pl.BlockSpec((1, w), lambda b: (0, 0)),
                  pl.BlockSpec((1, w), lambda b: (0, 0)),
                  pl.BlockSpec(bias.shape, lambda b: (0, 0, 0, 0))],
        out_specs=[pl.BlockSpec((1, seq, w), lambda b: (b, 0, 0)),
                   pl.BlockSpec((1, pairs, seq, LANES), lambda b: (b, 0, 0, 0))],
        out_shape=[jax.ShapeDtypeStruct((bsz, seq, w), BF16),
                   jax.ShapeDtypeStruct((bsz, pairs, seq, LANES), F32)],
        scratch_shapes=[pltpu.VMEM((pairs, seq, LANES), F32), pltpu.VMEM((pairs, pad + seq, LANES), F32),
                        pltpu.VMEM((pairs, pad + seq, LANES), F32), pltpu.VMEM((pairs, seq, LANES), F32)],
        compiler_params=_cparams("arbitrary"),
        name=f"dilated_attn_g{group}",
    )(view, view, view, gain(q_gain), gain(k_gain), bias)
    return o.reshape(bsz * seq, w), lse


def _sb_attn_kernel(q_ref, k_ref, v_ref, o_ref, vcat_scr, acc_scr, *, seq):
    tq, tk, nblk = SB_TQ, SB_TK, SB_TQ // SB_TK
    qi = pl.program_id(2)

    @pl.when(qi == 0)
    def _():
        chan = lax.broadcasted_iota(jnp.int32, (LANES, tk), 0)

        def build(kb, c):
            vt = v_ref[0, pl.ds(pl.multiple_of(kb * tk, tk), tk), :].astype(F32).T
            vcat_scr[kb] = jnp.concatenate(
                [jnp.where(chan < HEAD_DIM, vt, 0.0), jnp.where(chan >= HEAD_DIM, vt, 0.0)],
                axis=1).astype(BF16)
            return c

        lax.fori_loop(0, seq // tk, build, 0)

    lane = lax.broadcasted_iota(jnp.int32, (tq, LANES), 1)
    q = q_ref[0] * (1.0 / math.sqrt(HEAD_DIM))
    zero = jnp.zeros_like(q)
    qcat = jnp.concatenate([jnp.where(lane < HEAD_DIM, q, zero), jnp.where(lane >= HEAD_DIM, q, zero)],
                           axis=0)
    qcat_t = qcat.astype(F32).T.astype(BF16)
    neg_tri = jnp.where(lax.broadcasted_iota(jnp.int32, (tk, tk), 1)
                        >= lax.broadcasted_iota(jnp.int32, (tk, tk), 0), -1.0, 0.0).astype(BF16)
    acc_scr[...] = jnp.zeros_like(acc_scr)

    def scores(kb, rel):
        kk = k_ref[0, pl.ds(pl.multiple_of(kb * tk, tk), tk), :]
        zt = lax.dot_general(kk, qcat, (((1,), (1,)), ((), ())), preferred_element_type=F32)
        neg_abs = lax.bitcast_convert_type(
            lax.bitcast_convert_type(zt, jnp.uint32) | jnp.uint32(0x80000000), F32)
        sp = jnp.maximum(zt, 0.0) + jnp.log(1.0 + jnp.exp(neg_abs))
        before = None
        if rel is not None:
            kpos = lax.broadcasted_iota(jnp.int32, (tk, 2 * tq), 0) + rel * tk
            qpos = lax.broadcasted_iota(jnp.int32, (tk, 2 * tq), 1) & (tq - 1)
            before = kpos < qpos
            sp = jnp.where(before, sp, 0.0)
        return kk, zt[0:1, :], sp.astype(BF16), before

    def weights(state, run):
        kk, zt0, sp, before = state
        arg = jnp.dot(jnp.concatenate([neg_tri, kk], axis=1),
                      jnp.concatenate([sp, qcat_t], axis=0), preferred_element_type=F32)
        a = jnp.exp(arg - run)
        if before is not None:
            a = jnp.where(before, a, 0.0)
        a = a.astype(BF16)
        acat = jnp.concatenate([a[:, :tq], a[:, tq:]], axis=0)
        return acat, run + (zt0 - arg[0:1, :])

    def values(kb, acat):
        acc_scr[...] += jnp.dot(vcat_scr[kb], acat, preferred_element_type=F32)

    def run_blocks(blocks, run):
        n = len(blocks)
        st, ac = [None] * n, [None] * n
        for step in range(n + 2):
            if step < n:
                st[step] = scores(*blocks[step])
            if 0 <= step - 1 < n:
                ac[step - 1], run = weights(st[step - 1], run)
            if 0 <= step - 2 < n:
                values(blocks[step - 2][0], ac[step - 2])
        return run

    zero_run = jnp.zeros((1, 2 * tq), F32)
    diagonal = [(qi * nblk + rel, rel) for rel in reversed(range(nblk))]
    n_chunks = qi * (nblk // SB_CHUNK)

    def chunk_blocks(it):
        base = (n_chunks - 1 - it) * SB_CHUNK
        return [(base + j, None) for j in reversed(range(SB_CHUNK))]

    @pl.when(qi == 0)
    def _():
        run_blocks(diagonal, zero_run)

    @pl.when(qi > 0)
    def _():
        run = run_blocks(diagonal + chunk_blocks(0), zero_run)

        def alive(run):
            return (jnp.min(run) < SB_DEAD_RUN).astype(jnp.int32)

        def more(carry):
            it, _, go = carry
            return jnp.logical_and(it < n_chunks, go > 0)

        def chunk(carry):
            it, run, _ = carry
            run = run_blocks(chunk_blocks(it), run)
            return it + 1, run, alive(run)

        lax.while_loop(more, chunk, (jnp.int32(1), run, alive(run)))

    o_ref[0] = acc_scr[...].T.astype(o_ref.dtype)


def _sb_attention(proj, bsz, seq):
    e = proj.shape[-1]
    view = proj.reshape(bsz, seq, e)
    pairs = SB_WIDTH // LANES
    qo, ko, vo = COL_SB // LANES, (COL_SB + SB_WIDTH) // LANES, (COL_SB + 2 * SB_WIDTH) // LANES
    out = pl.pallas_call(
        functools.partial(_sb_attn_kernel, seq=seq),
        grid=(bsz, pairs, seq // SB_TQ),
        in_specs=[pl.BlockSpec((1, SB_TQ, LANES), lambda b, p, i: (b, i, qo + p)),
                  pl.BlockSpec((1, seq, LANES), lambda b, p, i: (b, 0, ko + p)),
                  pl.BlockSpec((1, seq, LANES), lambda b, p, i: (b, 0, vo + p))],
        out_specs=pl.BlockSpec((1, SB_TQ, LANES), lambda b, p, i: (b, i, p)),
        out_shape=jax.ShapeDtypeStruct((bsz, seq, SB_WIDTH), BF16),
        scratch_shapes=[pltpu.VMEM((seq // SB_TK, LANES, 2 * SB_TK), BF16),
                        pltpu.VMEM((LANES, SB_TQ), F32)],
        compiler_params=_cparams("arbitrary", "arbitrary", "arbitrary"),
        name="stick_breaking_attn",
    )(view, view, view)
    return out.reshape(bsz * seq, SB_WIDTH)


def _sb_attn_kernel2(q_ref, k_ref, v_ref, o_ref, vcat_scr, acc_scr, *, seq):
    tq, tk, nblk = SB_TQ, SB_TK, SB_TQ // SB_TK
    qi = pl.program_id(2)
    pair_lanes = [slice(p * LANES, (p + 1) * LANES) for p in range(SB_PAIRS)]

    def key_rows(kb):
        return pl.ds(pl.multiple_of(kb * tk, tk), tk)

    @pl.when(qi == 0)
    def _():
        chan = lax.broadcasted_iota(jnp.int32, (LANES, tk), 0)

        def build(kb, c):
            for p in range(SB_PAIRS):
                vt = v_ref[0, key_rows(kb), pair_lanes[p]].astype(F32).T
                vcat_scr[p, kb] = jnp.concatenate(
                    [jnp.where(chan < HEAD_DIM, vt, 0.0), jnp.where(chan >= HEAD_DIM, vt, 0.0)],
                    axis=1).astype(BF16)
            return c

        lax.fori_loop(0, seq // tk, build, 0)

    lane = lax.broadcasted_iota(jnp.int32, (tq, LANES), 1)
    qcat, qcat_t = [], []
    for p in range(SB_PAIRS):
        q = q_ref[0, :, pair_lanes[p]] * (1.0 / math.sqrt(HEAD_DIM))
        zero = jnp.zeros_like(q)
        both = jnp.concatenate([jnp.where(lane < HEAD_DIM, q, zero), jnp.where(lane >= HEAD_DIM, q, zero)],
                               axis=0)
        qcat.append(both)
        qcat_t.append(both.astype(F32).T.astype(BF16))
    neg_tri = jnp.where(lax.broadcasted_iota(jnp.int32, (tk, tk), 1)
                        >= lax.broadcasted_iota(jnp.int32, (tk, tk), 0), -1.0, 0.0).astype(BF16)
    acc_scr[...] = jnp.zeros_like(acc_scr)

    def first_query(rel):
        width = tq
        while rel is not None and tq - width // 2 <= rel * tk and width // 2 >= LANES:
            width //= 2
        return tq - width

    def window(x, lo, axis):
        if lo == 0:
            return x
        return jnp.concatenate([lax.slice_in_dim(x, lo, tq, axis=axis),
                                lax.slice_in_dim(x, tq + lo, 2 * tq, axis=axis)], axis=axis)

    def scores(p, kb, rel):
        lo = first_query(rel)
        wq = tq - lo
        kk = k_ref[0, key_rows(kb), pair_lanes[p]]
        zt = lax.dot_general(kk, window(qcat[p], lo, 0), (((1,), (1,)), ((), ())),
                             preferred_element_type=F32)
        neg_abs = lax.bitcast_convert_type(
            lax.bitcast_convert_type(zt, jnp.uint32) | jnp.uint32(0x80000000), F32)
        sp = jnp.maximum(zt, 0.0) + jnp.log(1.0 + jnp.exp(neg_abs))
        before = None
        if rel is not None:
            kpos = lax.broadcasted_iota(jnp.int32, (tk, 2 * wq), 0) + rel * tk
            qpos = (lax.broadcasted_iota(jnp.int32, (tk, 2 * wq), 1) & (wq - 1)) + lo
            before = kpos < qpos
            sp = jnp.where(before, sp, 0.0)
        return kk, zt[0:1, :], sp.astype(BF16), before, lo

    def weights(p, state, run):
        kk, zt0, sp, before, lo = state
        wq = tq - lo
        arg = jnp.dot(jnp.concatenate([neg_tri, kk], axis=1),
                      jnp.concatenate([sp, window(qcat_t[p], lo, 1)], axis=0),
                      preferred_element_type=F32)
        a = jnp.exp(arg - window(run, lo, 1))
        if before is not None:
            a = jnp.where(before, a, 0.0)
        a = a.astype(BF16)
        acat = jnp.concatenate([a[:, :wq], a[:, wq:]], axis=0)
        col_sums = zt0 - arg[0:1, :]
        if lo:
            none = jnp.zeros((1, lo), F32)
            col_sums = jnp.concatenate([none, col_sums[:, :wq], none, col_sums[:, wq:]], axis=1)
        return (acat, lo), run + col_sums

    def values(p, kb, acat_lo):
        acat, lo = acat_lo
        acc_scr[p, :, lo:] += jnp.dot(vcat_scr[p, kb], acat, preferred_element_type=F32)

    def run_blocks(blocks, runs):
        n = len(blocks)
        runs = list(runs)
        st = [[None] * n for _ in range(SB_PAIRS)]
        ac = [[None] * n for _ in range(SB_PAIRS)]
        for step in range(n + 2):
            for p in range(SB_PAIRS):
                if step < n:
                    st[p][step] = scores(p, *blocks[step])
            for p in range(SB_PAIRS):
                if 0 <= step - 1 < n:
                    ac[p][step - 1], runs[p] = weights(p, st[p][step - 1], runs[p])
            for p in range(SB_PAIRS):
                if 0 <= step - 2 < n:
                    values(p, blocks[step - 2][0], ac[p][step - 2])
        return tuple(runs)

    zero_runs = tuple(jnp.zeros((1, 2 * tq), F32) for _ in range(SB_PAIRS))
    diagonal = [(qi * nblk + rel, rel) for rel in reversed(range(nblk))]
    n_chunks = qi * (nblk // SB_CHUNK)

    def chunk_blocks(it):
        base = (n_chunks - 1 - it) * SB_CHUNK
        return [(base + j, None) for j in reversed(range(SB_CHUNK))]

    @pl.when(qi == 0)
    def _():
        run_blocks(diagonal, zero_runs)

    @pl.when(qi > 0)
    def _():
        runs = run_blocks(diagonal + chunk_blocks(0), zero_runs)

        def alive(runs):
            lowest = functools.reduce(jnp.minimum, [jnp.min(r) for r in runs])
            return (lowest < SB_DEAD_RUN).astype(jnp.int32)

        def more(carry):
            it, _, go = carry
            return jnp.logical_and(it < n_chunks, go > 0)

        def chunk(carry):
            it, runs, _ = carry
            runs = run_blocks(chunk_blocks(it), runs)
            return it + 1, runs, alive(runs)

        lax.while_loop(more, chunk, (jnp.int32(1), runs, alive(runs)))

    o_ref[0] = jnp.concatenate([acc_scr[p].T for p in range(SB_PAIRS)], axis=-1).astype(o_ref.dtype)


def _sb_attention2(proj, bsz, seq):
    e = proj.shape[-1]
    view = proj.reshape(bsz, seq, e)
    w = SB_PAIRS * LANES
    qo, ko, vo = COL_SB // w, (COL_SB + SB_WIDTH) // w, (COL_SB + 2 * SB_WIDTH) // w
    out = pl.pallas_call(
        functools.partial(_sb_attn_kernel2, seq=seq),
        grid=(bsz, SB_WIDTH // w, seq // SB_TQ),
        in_specs=[pl.BlockSpec((1, SB_TQ, w), lambda b, g, i: (b, i, qo + g)),
                  pl.BlockSpec((1, seq, w), lambda b, g, i: (b, 0, ko + g)),
                  pl.BlockSpec((1, seq, w), lambda b, g, i: (b, 0, vo + g))],
        out_specs=pl.BlockSpec((1, SB_TQ, w), lambda b, g, i: (b, i, g)),
        out_shape=jax.ShapeDtypeStruct((bsz, seq, SB_WIDTH), BF16),
        scratch_shapes=[pltpu.VMEM((SB_PAIRS, seq // SB_TK, LANES, 2 * SB_TK), BF16),
                        pltpu.VMEM((SB_PAIRS, LANES, SB_TQ), F32)],
        compiler_params=_cparams("arbitrary", "arbitrary", "arbitrary"),
        name="stick_breaking_attn",
    )(view, view, view)
    return out.reshape(bsz * seq, SB_WIDTH)


def _conv_kernel(val_ref, gate_ref, w_ref, b_ref, g_ref, beta_ref, o_ref, u_scr, y_scr, *, seq):
    tr, pad, slabs = CONV_ROWS, CONV_PAD, CONV_CH // LANES
    lanes = [slice(j * LANES, (j + 1) * LANES) for j in range(slabs)]
    for j in range(slabs):
        u_scr[j, pl.ds(0, pad), :] = jnp.zeros((pad, LANES), F32)

    def glu(i, c):
        rows = pl.ds(pl.multiple_of(i * 256, 256), 256)
        val = val_ref[0, rows, :].astype(F32)
        gate = gate_ref[0, rows, :].astype(F32)
        u = val * _sigmoid(gate)
        for j in range(slabs):
            u_scr[j, pl.ds(pl.multiple_of(pad + i * 256, SUBLANES), 256), :] = u[:, lanes[j]]
        return c

    lax.fori_loop(0, seq // 256, glu, 0)

    first = pad - (CONV_WIDTH - 1)

    def tile(i, c):
        t0 = pl.multiple_of(i * tr, tr)
        acc = [[jnp.zeros((tr // 2, LANES), F32) for _ in range(slabs)] for _ in range(2)]
        for w in range(CONV_WIDTH):
            for j in range(slabs):
                tap = jnp.broadcast_to(w_ref[w:w + 1, lanes[j]], (tr // 2, LANES))
                for phase in range(2):
                    rows = pl.ds(t0 + (first + w + phase), tr // 2, stride=2)
                    acc[phase][j] = acc[phase][j] + u_scr[j, rows, :] * tap
        for phase in range(2):
            y = jnp.concatenate(acc[phase], axis=-1) + b_ref[...]
            mu = jnp.mean(y, axis=-1, keepdims=True)
            yc = y - mu
            var = jnp.mean(yc * yc, axis=-1, keepdims=True)
            yn = yc * lax.rsqrt(var + EPS) * g_ref[...] + beta_ref[...]
            out = yn * _sigmoid(yn)
            for j in range(slabs):
                y_scr[j, pl.ds(phase, tr // 2, stride=2), :] = out[:, lanes[j]]
        o_ref[0, pl.ds(t0, tr), :] = jnp.concatenate([y_scr[j] for j in range(slabs)],
                                                     axis=-1).astype(o_ref.dtype)
        return c

    lax.fori_loop(0, seq // tr, tile, 0)


def _conformer_conv(proj, conv_w, conv_b, norm_g, norm_b, bsz, seq):
    e = proj.shape[-1]
    view = proj.reshape(bsz, seq, e)
    c = CONV_CH
    voff, goff = COL_GLU // c, (COL_GLU + c) // c
    const = lambda b: (0, 0)
    out = pl.pallas_call(
        functools.partial(_conv_kernel, seq=seq),
        grid=(bsz,),
        in_specs=[pl.BlockSpec((1, seq, c), lambda b: (b, 0, voff)),
                  pl.BlockSpec((1, seq, c), lambda b: (b, 0, goff)),
                  pl.BlockSpec((CONV_WIDTH, c), const),
                  pl.BlockSpec((1, c), const), pl.BlockSpec((1, c), const), pl.BlockSpec((1, c), const)],
        out_specs=pl.BlockSpec((1, seq, c), lambda b: (b, 0, 0)),
        out_shape=jax.ShapeDtypeStruct((bsz, seq, c), BF16),
        scratch_shapes=[pltpu.VMEM((c // LANES, seq + CONV_PAD, LANES), F32),
                        pltpu.VMEM((c // LANES, CONV_ROWS, LANES), F32)],
        compiler_params=_cparams("arbitrary"),
        name="conformer_conv",
    )(view, view, conv_w, conv_b.reshape(1, c), norm_g.reshape(1, c), norm_b.reshape(1, c))
    return out.reshape(bsz * seq, c)


def _merge_kernel(o1_ref, o2_ref, o3_ref, l1_ref, l2_ref, l3_ref, ob_ref, oc_ref,
                  ga_ref, gb_ref, gc_ref, x_ref, wa_ref, wb_ref, wc_ref, wo_ref, out_ref):
    by_lanes = lambda ref: jnp.concatenate([ref[p] for p in range(ref.shape[0])], axis=-1)
    l1, l2, l3 = by_lanes(l1_ref), by_lanes(l2_ref), by_lanes(l3_ref)
    m = jnp.maximum(jnp.maximum(l1, l2), l3)
    e1, e2, e3 = jnp.exp(l1 - m), jnp.exp(l2 - m), jnp.exp(l3 - m)
    o_a = (e1 * o1_ref[...].astype(F32) + e2 * o2_ref[...].astype(F32)
           + e3 * o3_ref[...].astype(F32)) / (e1 + e2 + e3)
    ya = jnp.dot(o_a.astype(BF16), wa_ref[...], preferred_element_type=F32)
    yb = jnp.dot(ob_ref[...], wb_ref[...], preferred_element_type=F32)
    yc = jnp.dot(oc_ref[...], wc_ref[...], preferred_element_type=F32)
    merged = (_sigmoid(ga_ref[...].astype(F32)) * ya + _sigmoid(gb_ref[...].astype(F32)) * yb
              + _sigmoid(gc_ref[...].astype(F32)) * yc)
    out_ref[...] = x_ref[...] + jnp.dot(merged.astype(BF16), wo_ref[...], preferred_element_type=F32)


def _merge(o_groups, lse_groups, o_b, o_c, proj, x, wa, wb, wc, wo, *, tm):
    n, d = x.shape
    row = lambda w, j=0: pl.BlockSpec((tm, w), lambda i: (i, j))
    full = lambda a: pl.BlockSpec(a.shape, lambda i: (0, 0), pipeline_mode=pl.Buffered(1))
    g0 = COL_GATES // d
    _, pairs, seq, _ = lse_groups[0].shape
    per_batch = seq // tm
    lse = pl.BlockSpec((None, pairs, tm, LANES), lambda i: (i // per_batch, 0, i % per_batch, 0))
    return pl.pallas_call(
        _merge_kernel,
        grid=(n // tm,),
        in_specs=[row(DIL_OUT), row(DIL_OUT), row(DIL_OUT), lse, lse, lse,
                  row(SB_WIDTH), row(CONV_CH), row(d, g0), row(d, g0 + 1), row(d, g0 + 2), row(d),
                  full(wa), full(wb), full(wc), full(wo)],
        out_specs=row(d),
        out_shape=jax.ShapeDtypeStruct((n, d), F32),
        compiler_params=_cparams("arbitrary"),
        name="branch_merge_out_proj",
    )(*o_groups, *lse_groups, o_b, o_c, proj, proj, proj, x, wa, wb, wc, wo)


MXU_COLS = 256


def _col_pieces(lo, hi):
    return [(p, min(p + MXU_COLS, hi)) for p in range(lo, hi, MXU_COLS)]


FF_CHUNKS = ((0, FF_SPLIT), (FF_SPLIT, D_FF))
SWIGLU_HOOKS = sum(2 * len(_col_pieces(lo, hi)) + len(_col_pieces(0, D_MODEL)) for lo, hi in FF_CHUNKS)


def _swiglu_tile(h, wg_ref, wu_ref, wd_ref, hook=None):
    def matmul(x, w_ref, rows, lo, hi):
        if hook is None:
            return jnp.dot(x, w_ref[rows, lo:hi], preferred_element_type=F32)
        pieces = []
        for p, q in _col_pieces(lo, hi):
            pieces.append(jnp.dot(x, w_ref[rows, p:q], preferred_element_type=F32))
            hook()
        return jnp.concatenate(pieces, axis=-1)

    y = None
    for lo, hi in FF_CHUNKS:
        a = matmul(h, wg_ref, slice(None), lo, hi)
        u = matmul(h, wu_ref, slice(None), lo, hi)
        act = (a * _sigmoid(a) * u).astype(BF16)
        part = matmul(act, wd_ref, slice(lo, hi), 0, D_MODEL)
        y = part if y is None else y + part
    return y


def _dense_ffn_kernel(x_ref, g_ref, wg_ref, wu_ref, wd_ref, o_ref):
    x = x_ref[...]
    ms = jnp.mean(x * x, axis=-1, keepdims=True)
    h = ((x * lax.rsqrt(ms + EPS)) * g_ref[...]).astype(BF16)
    o_ref[...] = x + _swiglu_tile(h, wg_ref, wu_ref, wd_ref)


def _dense_ffn(x, g, wg, wu, wd, *, tm):
    n, d = x.shape
    full = lambda a: pl.BlockSpec(a.shape, lambda i: (0, 0), pipeline_mode=pl.Buffered(1))
    return pl.pallas_call(
        _dense_ffn_kernel,
        grid=(n // tm,),
        in_specs=[pl.BlockSpec((tm, d), lambda i: (i, 0)), pl.BlockSpec((1, d), lambda i: (0, 0)),
                  full(wg), full(wu), full(wd)],
        out_specs=pl.BlockSpec((tm, d), lambda i: (i, 0)),
        out_shape=jax.ShapeDtypeStruct((n, d), F32),
        compiler_params=_cparams("arbitrary"),
        name="dense_swiglu",
    )(x, g.reshape(1, d), wg, wu, wd)


def _router_kernel(x_ref, g_ref, wr_ref, br_ref, h_ref, route_ref):
    x = x_ref[...]
    ms = jnp.mean(x * x, axis=-1, keepdims=True)
    h = (x * lax.rsqrt(ms + EPS)) * g_ref[...]
    _store_token_tiles(h_ref, h)
    h_hi = h.astype(BF16)
    h_lo = (h - h_hi.astype(F32)).astype(BF16)
    w = wr_ref[...]
    w_hi = w.astype(BF16)
    w_lo = (w - w_hi.astype(F32)).astype(BF16)
    logits = (jnp.dot(h_hi, w_hi, preferred_element_type=F32)
              + jnp.dot(h_lo, w_hi, preferred_element_type=F32)
              + jnp.dot(h_hi, w_lo, preferred_element_type=F32)) + br_ref[...]
    lane = lax.broadcasted_iota(jnp.int32, logits.shape, 1)
    m1 = jnp.max(logits, axis=-1, keepdims=True)
    i1 = jnp.min(jnp.where(logits == m1, lane, LANES), axis=-1, keepdims=True)
    rest = jnp.where(lane == i1, -jnp.inf, logits)
    m2 = jnp.max(rest, axis=-1, keepdims=True)
    i2 = jnp.min(jnp.where(rest == m2, lane, LANES), axis=-1, keepdims=True)
    e2 = jnp.exp(m2 - m1)
    g1 = 1.0 / (1.0 + e2)
    g2 = e2 / (1.0 + e2)
    route = jnp.where(lane == 0, i1.astype(F32),
                      jnp.where(lane == 1, i2.astype(F32),
                                jnp.where(lane == 2, g1, jnp.where(lane == 3, g2, 0.0))))
    route_ref[...] = route


def _router(x, g, w_router, b_router, *, tm):
    n, d = x.shape
    wr = jnp.zeros((d, LANES), F32).at[:, :N_EXPERTS].set(w_router)
    br = jnp.full((1, LANES), NEG_BIG, F32).at[0, :N_EXPERTS].set(b_router)
    return pl.pallas_call(
        _router_kernel,
        grid=(n // tm,),
        in_specs=[pl.BlockSpec((tm, d), lambda i: (i, 0)), pl.BlockSpec((1, d), lambda i: (0, 0)),
                  pl.BlockSpec((d, LANES), lambda i: (0, 0)), pl.BlockSpec((1, LANES), lambda i: (0, 0))],
        out_specs=[pl.BlockSpec((tm * ROW_TILE, LANES), lambda i: (i, 0)),
                   pl.BlockSpec((tm, LANES), lambda i: (i, 0))],
        out_shape=[jax.ShapeDtypeStruct((n * ROW_TILE, LANES), F32), jax.ShapeDtypeStruct((n, LANES), F32)],
        compiler_params=_cparams("arbitrary"),
        name="router_top2",
    )(x, g.reshape(1, d), wr, br)


ROW_TILE = D_MODEL // LANES


def _store_token_tiles(ref, x):
    rows = x.shape[0]
    for c in range(ROW_TILE):
        ref[pl.ds(c, rows, stride=ROW_TILE), :] = x[:, c * LANES:(c + 1) * LANES]


def _load_token_tiles(ref, rows):
    return jnp.concatenate([ref[pl.ds(c, rows, stride=ROW_TILE), :] for c in range(ROW_TILE)], axis=-1)


def _tile_rows(t):
    start = t * ROW_TILE
    return pl.ds(start if isinstance(start, int) else pl.multiple_of(start, ROW_TILE), ROW_TILE)


class _BlockRows:
    def __init__(self, dst_ref, n_tokens):
        self.dst_ref = dst_ref
        self.last_token = n_tokens - 1

    def gather_row(self, blk, r, h_hbm, xbuf, sem):
        tok = lax.shift_right_logical(self.dst_ref[blk, r], TOP_K.bit_length() - 1)
        tok = jnp.minimum(tok, self.last_token)
        pltpu.make_async_copy(h_hbm.at[_tile_rows(tok), :], xbuf.at[_tile_rows(r), :],
                              sem).start(priority=self._queue(r))

    def scatter_row(self, blk, r, ybuf, y_hbm, sem):
        pltpu.make_async_copy(ybuf.at[_tile_rows(r), :], y_hbm.at[_tile_rows(self.dst_ref[blk, r]), :],
                              sem).start(priority=self._queue(r))

    @staticmethod
    def _queue(r):
        return r % 2 if isinstance(r, int) else 0

    def all_rows(self, row_fn):
        def body(g, c):
            for j in range(SUBLANES):
                row_fn(g * SUBLANES + j)
            return c

        lax.fori_loop(0, MOE_TM // SUBLANES, body, 0)

    @staticmethod
    def wait_gather(h_hbm, xbuf, sem):
        pltpu.make_async_copy(h_hbm.at[pl.ds(0, MOE_TM * ROW_TILE), :], xbuf, sem).wait()

    @staticmethod
    def wait_scatter(ybuf, y_hbm, sem):
        pltpu.make_async_copy(ybuf, y_hbm.at[pl.ds(0, MOE_TM * ROW_TILE), :], sem).wait()


def _expert_kernel(be_ref, used_ref, dst_ref, h_hbm, wg_ref, wu_ref, wd_ref, y_hbm,
                   xbuf, ybuf, gsem, ssem, *, n_tokens):
    i = pl.program_id(0)
    used = used_ref[0]
    slot = i % 2
    rows = _BlockRows(dst_ref, n_tokens)
    gather_hooks = SWIGLU_HOOKS // 2
    share = lambda k, hooks: range(k * MOE_TM // hooks, (k + 1) * MOE_TM // hooks)

    def compute(scatter_previous):
        _BlockRows.wait_gather(h_hbm, xbuf.at[slot], gsem.at[slot])
        x = _load_token_tiles(xbuf.at[slot], MOE_TM).astype(BF16)
        nxt = jnp.minimum(i + 1, used - 1)
        calls = [0]

        def start_some():
            k = calls[0]
            calls[0] += 1
            if k < gather_hooks:
                for r in share(k, gather_hooks):
                    rows.gather_row(nxt, r, h_hbm, xbuf.at[1 - slot], gsem.at[1 - slot])
            if scatter_previous:
                for r in share(k, SWIGLU_HOOKS):
                    rows.scatter_row(i - 1, r, ybuf.at[1 - slot], y_hbm, ssem.at[1 - slot])

        y = _swiglu_tile(x, wg_ref, wu_ref, wd_ref, hook=start_some)
        assert calls[0] == SWIGLU_HOOKS

        @pl.when(i >= 2)
        def _():
            _BlockRows.wait_scatter(ybuf.at[slot], y_hbm, ssem.at[slot])

        _store_token_tiles(ybuf.at[slot], y)

    @pl.when(i == 0)
    def _():
        rows.all_rows(lambda r: rows.gather_row(0, r, h_hbm, xbuf.at[0], gsem.at[0]))
        ybuf[1] = jnp.zeros((MOE_TM * ROW_TILE, LANES), F32)
        for half in range(2):
            clear = pltpu.make_async_copy(
                ybuf.at[1],
                y_hbm.at[pl.ds((n_tokens * TOP_K + half * MOE_TM) * ROW_TILE, MOE_TM * ROW_TILE), :],
                ssem.at[half])
            clear.start()
            clear.wait()
        compute(False)

    @pl.when(jnp.logical_and(i >= 1, i < used))
    def _():
        compute(True)

    @pl.when(i == used)
    def _():
        _BlockRows.wait_gather(h_hbm, xbuf.at[slot], gsem.at[slot])
        _BlockRows.wait_scatter(ybuf.at[slot], y_hbm, ssem.at[slot])
        rows.all_rows(lambda r: rows.scatter_row(i - 1, r, ybuf.at[1 - slot], y_hbm, ssem.at[1 - slot]))
        _BlockRows.wait_scatter(ybuf.at[1 - slot], y_hbm, ssem.at[1 - slot])


def _experts(h, block_expert, used, row_dst, wg, wu, wd, n_tokens):
    d = D_MODEL
    n_blocks = row_dst.shape[0]
    f = wg.shape[-1]
    assert n_tokens * TOP_K >= 2 * MOE_TM
    one = pl.Buffered(1)
    grid_spec = pltpu.PrefetchScalarGridSpec(
        num_scalar_prefetch=3,
        grid=(n_blocks,),
        in_specs=[pl.BlockSpec(memory_space=pl.ANY),
                  pl.BlockSpec((None, d, f), lambda i, be, us, ds: (be[i], 0, 0), pipeline_mode=one),
                  pl.BlockSpec((None, d, f), lambda i, be, us, ds: (be[i], 0, 0), pipeline_mode=one),
                  pl.BlockSpec((None, f, d), lambda i, be, us, ds: (be[i], 0, 0), pipeline_mode=one)],
        out_specs=pl.BlockSpec(memory_space=pl.ANY),
        scratch_shapes=[pltpu.VMEM((2, MOE_TM * ROW_TILE, LANES), F32),
                        pltpu.VMEM((2, MOE_TM * ROW_TILE, LANES), F32),
                        pltpu.SemaphoreType.DMA((2,)), pltpu.SemaphoreType.DMA((2,))],
    )
    return pl.pallas_call(
        functools.partial(_expert_kernel, n_tokens=n_tokens),
        grid_spec=grid_spec,
        out_shape=jax.ShapeDtypeStruct(((n_tokens * TOP_K + 2 * MOE_TM) * ROW_TILE, LANES), F32),
        compiler_params=_cparams("arbitrary"),
        name="expert_swiglu",
    )(block_expert, used, row_dst, h, wg, wu, wd)


def _combine_kernel(y_ref, x_ref, route_ref, o_ref):
    tt = x_ref.shape[0]
    route = route_ref[...]
    out = x_ref[...]
    for k in range(TOP_K):
        yk = jnp.concatenate([y_ref[pl.ds(k * ROW_TILE + c, tt, stride=TOP_K * ROW_TILE), :]
                              for c in range(ROW_TILE)], axis=-1)
        out = out + route[:, TOP_K + k:TOP_K + k + 1] * yk
    o_ref[...] = out


def _combine(x, ys, route):
    n, d = x.shape
    tt = COMBINE_TT
    return pl.pallas_call(
        _combine_kernel,
        grid=(n // tt,),
        in_specs=[pl.BlockSpec((tt * TOP_K * ROW_TILE, LANES), lambda i: (i, 0)),
                  pl.BlockSpec((tt, d), lambda i: (i, 0)),
                  pl.BlockSpec((tt, LANES), lambda i: (i, 0))],
        out_specs=pl.BlockSpec((tt, d), lambda i: (i, 0)),
        out_shape=jax.ShapeDtypeStruct((n, d), F32),
        compiler_params=_cparams("arbitrary"),
        name="expert_combine",
    )(ys, x, route)


def _routed_ffn(x, g, w_router, b_router, wg, wu, wd, *, tm):
    n, d = x.shape
    h, route = _router(x, g, w_router, b_router, tm=tm)
    e_flat = route[:, :TOP_K].astype(jnp.int32).reshape(-1)
    onehot = (e_flat[:, None] == jnp.arange(N_EXPERTS)[None, :]).astype(jnp.int32)
    csum = jnp.cumsum(onehot, axis=0)
    counts = csum[-1]
    rank = jnp.sum((csum - onehot) * onehot, axis=1)
    padded = (counts + MOE_TM - 1) // MOE_TM * MOE_TM
    pend = jnp.cumsum(padded)
    pstart = pend - padded
    dest = (pstart[e_flat] + rank).astype(jnp.int32)
    n_blocks = (n * TOP_K) // MOE_TM + N_EXPERTS
    scratch_rows = n * TOP_K + jnp.arange(n_blocks * MOE_TM, dtype=jnp.int32) % (2 * MOE_TM)
    row_dst = scratch_rows.at[dest].set(jnp.arange(n * TOP_K, dtype=jnp.int32)).reshape(n_blocks, MOE_TM)
    block_expert = jnp.clip(jnp.searchsorted(pend, jnp.arange(n_blocks) * MOE_TM, side='right'),
                            0, N_EXPERTS - 1).astype(jnp.int32)
    used = (pend[-1:] // MOE_TM).astype(jnp.int32)
    ys = _experts(h, block_expert, used, row_dst, wg, wu, wd, n)
    return _combine(x, ys, route)


def kernel(x, attn_norm_g, w_in, q_norm_g, k_norm_g, conv_w, conv_b, conv_norm_g, conv_norm_b,
           w_branch_a, w_branch_b, w_branch_c, w_out, ffn_norm_g, w_ffn_gate, w_ffn_up,
           w_ffn_down, w_router, b_router, w_exp_gate, w_exp_up, w_exp_down):
    bsz, seq, d = x.shape
    depth = attn_norm_g.shape[0]
    n = bsz * seq
    tm = 512
    xf = x.reshape(n, d)
    s_dil, s_sb, s_glu = 3 * DIL_WIDTH, 3 * DIL_WIDTH + 3 * SB_WIDTH, 3 * DIL_WIDTH + 3 * SB_WIDTH + 2 * CONV_CH
    for layer in range(depth):
        wl = w_in[layer]
        w_perm = jnp.concatenate([wl[:, s_glu:], wl[:, s_sb:s_glu], wl[:, s_dil:s_sb], wl[:, :s_dil]],
                                 axis=1).astype(BF16)
        proj_a = _norm_matmul(xf, attn_norm_g[layer], w_perm[:, :PROJ_A_COLS], tm=tm)
        proj_b = _norm_matmul(xf, attn_norm_g[layer], w_perm[:, PROJ_A_COLS:], tm=tm)
        o_groups, lse_groups = [], []
        for gi, (window, dilation) in enumerate(DIL_GROUPS):
            o_g, lse_g = _dil_attention(proj_b, q_norm_g[layer], k_norm_g[layer], gi, window, dilation,
                                        bsz, seq)
            o_groups.append(o_g)
            lse_groups.append(lse_g)
        o_b = _sb_attention2(proj_b, bsz, seq)
        o_c = _conformer_conv(proj_a, conv_w[layer], conv_b[layer], conv_norm_g[layer],
                              conv_norm_b[layer], bsz, seq)
        xf = _merge(o_groups, lse_groups, o_b, o_c, proj_a, xf,
                    w_branch_a[layer].astype(BF16), w_branch_b[layer].astype(BF16),
                    w_branch_c[layer].astype(BF16), w_out[layer].astype(BF16), tm=tm)
        i = layer // 2
        if layer % 2 == 0:
            xf = _dense_ffn(xf, ffn_norm_g[layer], w_ffn_gate[i].astype(BF16), w_ffn_up[i].astype(BF16),
                            w_ffn_down[i].astype(BF16), tm=tm)
        else:
            xf = _routed_ffn(xf, ffn_norm_g[layer], w_router[i], b_router[i],
                             w_exp_gate[i].astype(BF16), w_exp_up[i].astype(BF16),
                             w_exp_down[i].astype(BF16), tm=tm)
    return xf.reshape(bsz, seq, d)
```

```python
import functools
import math

import jax
import jax.numpy as jnp
from jax import lax
from jax.experimental import pallas as pl
from jax.experimental.pallas import tpu as pltpu

F32 = jnp.float32
BF16 = jnp.bfloat16

D_MODEL = 1024
HEAD_DIM = 64
DIL_GROUPS = ((128, 1), (512, 4), (2048, 16))
DIL_HEADS_PER_GROUP = 4
DIL_HEADS = len(DIL_GROUPS) * DIL_HEADS_PER_GROUP
DIL_WIDTH = DIL_HEADS * HEAD_DIM
DIL_OUT = DIL_HEADS_PER_GROUP * HEAD_DIM
DIL_BLOCK = 128
SB_HEADS = 8
SB_WIDTH = SB_HEADS * HEAD_DIM
CONV_CH = D_MODEL // 2
CONV_WIDTH = 31
N_BRANCH = 3
IN_COLS = 3 * DIL_WIDTH + 3 * SB_WIDTH + 2 * CONV_CH + N_BRANCH * D_MODEL
D_FF = 2816
N_EXPERTS = 8
TOP_K = 2
EPS = 1e-6
ALIBI_MAX_BIAS = 8.0
NEG_BIG = -1e30

COL_GATES = 0
COL_GLU = COL_GATES + N_BRANCH * D_MODEL
PROJ_A_COLS = COL_GLU + 2 * CONV_CH
COL_SB = 0
COL_DIL = COL_SB + 3 * SB_WIDTH

LANES = 128
SUBLANES = 8
VMEM_LIMIT = 56 * 1024 * 1024

DIL_UNITS = 4
SB_TQ = 512
SB_TK = 128
SB_PAIRS = 2
SB_CHUNK = 2
SB_DEAD_RUN = 128.0
CONV_ROWS = 64
CONV_PAD = 32
FF_SPLIT = 1536
MOE_TM = 512
COMBINE_TT = 512


def _cparams(*sem):
    return pltpu.CompilerParams(dimension_semantics=sem, vmem_limit_bytes=VMEM_LIMIT)


def _sigmoid(x):
    return 1.0 / (1.0 + jnp.exp(-x))


def _norm_matmul_kernel(x_ref, g_ref, w_ref, o_ref):
    x = x_ref[...]
    ms = jnp.mean(x * x, axis=-1, keepdims=True)
    h = (x * lax.rsqrt(ms + EPS)) * g_ref[...]
    o_ref[...] = jnp.dot(h.astype(BF16), w_ref[...], preferred_element_type=F32).astype(o_ref.dtype)


def _norm_matmul(x, g, w, *, tm):
    n, d = x.shape
    e = w.shape[1]
    return pl.pallas_call(
        _norm_matmul_kernel,
        grid=(n // tm,),
        in_specs=[pl.BlockSpec((tm, d), lambda i: (i, 0)),
                  pl.BlockSpec((1, d), lambda i: (0, 0)),
                  pl.BlockSpec((d, e), lambda i: (0, 0), pipeline_mode=pl.Buffered(1))],
        out_specs=pl.BlockSpec((tm, e), lambda i: (i, 0)),
        out_shape=jax.ShapeDtypeStruct((n, e), BF16),
        compiler_params=_cparams("arbitrary"),
        name="norm_in_proj",
    )(x, g.reshape(1, d), w)


def _same_head_matrix(w):
    r = lax.broadcasted_iota(jnp.int32, (w, w), 0) // HEAD_DIM
    c = lax.broadcasted_iota(jnp.int32, (w, w), 1) // HEAD_DIM
    return (r == c).astype(BF16)


def _head_rms_scale(t, same_head):
    ssq = jnp.dot((t * t).astype(BF16), same_head, preferred_element_type=F32)
    return lax.rsqrt(ssq * (1.0 / HEAD_DIM) + EPS)


def _dil_attn_kernel(q_ref, k_ref, v_ref, qg_ref, kg_ref, bias_ref, o_ref, lse_ref,
                     qn_scr, kn_scr, v_scr, o_scr, *, seq, dilation):
    blk, d = DIL_BLOCK, dilation
    span = blk * d
    log2d = d.bit_length() - 1
    same_head = _same_head_matrix(DIL_OUT)
    qg = qg_ref[...] * (1.0 / math.sqrt(HEAD_DIM))
    kg = kg_ref[...]
    chunk = 256

    pairs = DIL_OUT // LANES
    pair_lanes = [slice(p * LANES, (p + 1) * LANES) for p in range(pairs)]
    for p in range(pairs):
        kn_scr[p, pl.ds(0, span), :] = jnp.zeros((span, LANES), F32)
        v_scr[p, pl.ds(0, span), :] = jnp.zeros((span, LANES), F32)

    def norm(i, c):
        rows = pl.ds(pl.multiple_of(i * chunk, chunk), chunk)
        prows = pl.ds(pl.multiple_of(span + i * chunk, blk), chunk)
        q = q_ref[0, rows, :].astype(F32)
        qn = q * _head_rms_scale(q, same_head) * qg
        k = k_ref[0, rows, :].astype(F32)
        kn = k * _head_rms_scale(k, same_head) * kg
        v = v_ref[0, rows, :].astype(F32)
        for p in range(pairs):
            qn_scr[p, rows, :] = qn[:, pair_lanes[p]]
            kn_scr[p, prows, :] = kn[:, pair_lanes[p]]
            v_scr[p, prows, :] = v[:, pair_lanes[p]]
        return c

    lax.fori_loop(0, seq // chunk, norm, 0)

    low_half = lax.broadcasted_iota(jnp.int32, (blk, LANES), 1) < HEAD_DIM

    def sub_rows(base, count):
        return pl.ds(pl.multiple_of(base, blk), count) if d == 1 else pl.ds(base, count, stride=d)

    def load(u):
        if d == 1:
            n, base = u, u * span
        else:
            n = u >> log2d
            base = (u & (d - 1)) + n * span
        q = [qn_scr[p, sub_rows(base, blk), :] for p in range(pairs)]
        kk = [kn_scr[p, sub_rows(base, 2 * blk), :].astype(BF16) for p in range(pairs)]
        vv = [v_scr[p, sub_rows(base, 2 * blk), :].astype(BF16) for p in range(pairs)]
        first = jnp.where(n == 0, 1, 0)
        return base, q, kk, vv, first

    def scores(unit):
        _, q, kk, _, first = unit
        out = []
        for h in range(DIL_HEADS_PER_GROUP):
            keep = low_half if h % 2 == 0 else jnp.logical_not(low_half)
            qh = jnp.where(keep, q[h // 2], 0.0).astype(BF16)
            s = lax.dot_general(qh, kk[h // 2], (((1,), (1,)), ((), ())), preferred_element_type=F32)
            out.append(s + bias_ref[h, first])
        return out

    def finish(unit, s_list):
        base, _, _, vv, _ = unit
        rows = sub_rows(base, blk)
        for p in range(pairs):
            oh, lh = [], []
            for hh in range(2):
                s = s_list[2 * p + hh]
                m = jnp.max(s, axis=-1, keepdims=True)
                e = jnp.exp(s - m)
                den = jnp.sum(e, axis=-1, keepdims=True)
                oh.append(jnp.dot(e.astype(BF16), vv[p], preferred_element_type=F32) * (1.0 / den))
                lh.append(m + jnp.log(den))
            o_scr[p, rows, :] = jnp.where(low_half, oh[0], oh[1])
            lse_ref[0, p, rows, :] = jnp.where(low_half, lh[0], lh[1])

    def body(it, c):
        units = [load(it * DIL_UNITS + i) for i in range(DIL_UNITS)]
        s = [None] * DIL_UNITS
        for step in range(DIL_UNITS + 1):
            if step < DIL_UNITS:
                s[step] = scores(units[step])
            if step >= 1:
                finish(units[step - 1], s[step - 1])
        return c

    lax.fori_loop(0, seq // blk // DIL_UNITS, body, 0)

    def emit(i, c):
        rows = pl.ds(pl.multiple_of(i * chunk, chunk), chunk)
        o_ref[0, rows, :] = jnp.concatenate([o_scr[p, rows, :] for p in range(pairs)],
                                            axis=-1).astype(o_ref.dtype)
        return c

    lax.fori_loop(0, seq // chunk, emit, 0)


def _dil_bias_table(group, window, dilation):
    reach = window // dilation
    assert reach <= DIL_BLOCK
    slopes = 2.0 ** (-ALIBI_MAX_BIAS * jnp.arange(1, DIL_HEADS + 1, dtype=F32) / DIL_HEADS)
    slopes = slopes[group * DIL_HEADS_PER_GROUP:(group + 1) * DIL_HEADS_PER_GROUP]
    qi = jnp.arange(DIL_BLOCK)[:, None] + DIL_BLOCK
    ki = jnp.arange(2 * DIL_BLOCK)[None, :]
    dist = qi - ki
    valid = (dist >= 0) & (dist <= reach)
    bias = -slopes[:, None, None] * (dist * dilation).astype(F32)[None]
    general = jnp.where(valid[None], bias, NEG_BIG)
    first = jnp.where((valid & (ki >= DIL_BLOCK))[None], bias, NEG_BIG)
    return jnp.stack([general, first], axis=1)


def _dil_attention(proj, q_gain, k_gain, group, window, dilation, bsz, seq):
    e = proj.shape[-1]
    assert (seq // DIL_BLOCK) % DIL_UNITS == 0
    assert seq % (DIL_BLOCK * dilation) == 0 and dilation & (dilation - 1) == 0
    view = proj.reshape(bsz, seq, e)
    w = DIL_OUT
    pairs = w // LANES
    pad = DIL_BLOCK * dilation

    def col(base):
        off = (base + group * w) // w
        return lambda b: (b, 0, off)

    bias = _dil_bias_table(group, window, dilation)
    gain = lambda g: jnp.tile(g, DIL_HEADS_PER_GROUP).reshape(1, w)
    o, lse = pl.pallas_call(
        functools.partial(_dil_attn_kernel, seq=seq, dilation=dilation),
        grid=(bsz,),
        in_specs=[pl.BlockSpec((1, seq, w), col(COL_DIL)),
                  pl.BlockSpec((1, seq, w), col(COL_DIL + DIL_WIDTH)),
                  pl.BlockSpec((1, seq, w), col(COL_DIL + 2 * DIL_WIDTH)),
                  pl.BlockSpec((1, w), lambda b: (0, 0)),
                  pl.BlockSpec((1, w), lambda b: (0, 0)),
                  pl.BlockSpec(bias.shape, lambda b: (0, 0, 0, 0))],
        out_specs=[pl.BlockSpec((1, seq, w), lambda b: (b, 0, 0)),
                   pl.BlockSpec((1, pairs, seq, LANES), lambda b: (b, 0, 0, 0))],
        out_shape=[jax.ShapeDtypeStruct((bsz, seq, w), BF16),
                   jax.ShapeDtypeStruct((bsz, pairs, seq, LANES), F32)],
        scratch_shapes=[pltpu.VMEM((pairs, seq, LANES), F32), pltpu.VMEM((pairs, pad + seq, LANES), F32),
                        pltpu.VMEM((pairs, pad + seq, LANES), F32), pltpu.VMEM((pairs, seq, LANES), F32)],
        compiler_params=_cparams("arbitrary"),
        name=f"dilated_attn_g{group}",
    )(view, view, view, gain(q_gain), gain(k_gain), bias)
    return o.reshape(bsz * seq, w), lse


def _sb_attn_kernel(q_ref, k_ref, v_ref, o_ref, vcat_scr, acc_scr, *, seq):
    tq, tk, nblk = SB_TQ, SB_TK, SB_TQ // SB_TK
    qi = pl.program_id(2)
    pair_lanes = [slice(p * LANES, (p + 1) * LANES) for p in range(SB_PAIRS)]

    def key_rows(kb):
        return pl.ds(pl.multiple_of(kb * tk, tk), tk)

    @pl.when(qi == 0)
    def _():
        chan = lax.broadcasted_iota(jnp.int32, (LANES, tk), 0)

        def build(kb, c):
            for p in range(SB_PAIRS):
                vt = v_ref[0, key_rows(kb), pair_lanes[p]].astype(F32).T
                vcat_scr[p, kb] = jnp.concatenate(
                    [jnp.where(chan < HEAD_DIM, vt, 0.0), jnp.where(chan >= HEAD_DIM, vt, 0.0)],
                    axis=1).astype(BF16)
            return c

        lax.fori_loop(0, seq // tk, build, 0, unroll=4)

    lane = lax.broadcasted_iota(jnp.int32, (tq, LANES), 1)
    qcat, qcat_t = [], []
    for p in range(SB_PAIRS):
        q = q_ref[0, :, pair_lanes[p]] * (1.0 / math.sqrt(HEAD_DIM))
        zero = jnp.zeros_like(q)
        both = jnp.concatenate([jnp.where(lane < HEAD_DIM, q, zero), jnp.where(lane >= HEAD_DIM, q, zero)],
                               axis=0)
        qcat.append(both)
        qcat_t.append(both.astype(F32).T.astype(BF16))
    neg_tri = jnp.where(lax.broadcasted_iota(jnp.int32, (tk, tk), 1)
                        >= lax.broadcasted_iota(jnp.int32, (tk, tk), 0), -1.0, 0.0).astype(BF16)
    acc_scr[...] = jnp.zeros_like(acc_scr)

    def first_query(rel):
        width = tq
        while rel is not None and tq - width // 2 <= rel * tk and width // 2 >= LANES:
            width //= 2
        return tq - width

    def window(x, lo, axis):
        if lo == 0:
            return x
        return jnp.concatenate([lax.slice_in_dim(x, lo, tq, axis=axis),
                                lax.slice_in_dim(x, tq + lo, 2 * tq, axis=axis)], axis=axis)

    def scores(p, kb, rel):
        lo = first_query(rel)
        wq = tq - lo
        kk = k_ref[0, key_rows(kb), pair_lanes[p]]
        zt = lax.dot_general(kk, window(qcat[p], lo, 0), (((1,), (1,)), ((), ())),
                             preferred_element_type=F32)
        neg_abs = lax.bitcast_convert_type(
            lax.bitcast_convert_type(zt, jnp.uint32) | jnp.uint32(0x80000000), F32)
        sp = jnp.maximum(zt, 0.0) + jnp.log(1.0 + jnp.exp(neg_abs))
        before = None
        if rel is not None:
            kpos = lax.broadcasted_iota(jnp.int32, (tk, 2 * wq), 0) + rel * tk
            qpos = (lax.broadcasted_iota(jnp.int32, (tk, 2 * wq), 1) & (wq - 1)) + lo
            before = kpos < qpos
            sp = jnp.where(before, sp, 0.0)
        return kk, zt[0:1, :], sp.astype(BF16), before, lo

    def weights(p, state, run):
        kk, zt0, sp, before, lo = state
        wq = tq - lo
        arg = jnp.dot(jnp.concatenate([neg_tri, kk], axis=1),
                      jnp.concatenate([sp, window(qcat_t[p], lo, 1)], axis=0),
                      preferred_element_type=F32)
        a = jnp.exp(arg - window(run, lo, 1))
        if before is not None:
            a = jnp.where(before, a, 0.0)
        a = a.astype(BF16)
        acat = jnp.concatenate([a[:, :wq], a[:, wq:]], axis=0)
        col_sums = zt0 - arg[0:1, :]
        if lo:
            none = jnp.zeros((1, lo), F32)
            col_sums = jnp.concatenate([none, col_sums[:, :wq], none, col_sums[:, wq:]], axis=1)
        return (acat, lo), run + col_sums

    def values(p, kb, acat_lo):
        acat, lo = acat_lo
        acc_scr[p, :, lo:] += jnp.dot(vcat_scr[p, kb], acat, preferred_element_type=F32)

    def run_blocks(blocks, runs):
        n = len(blocks)
        runs = list(runs)
        st = [[None] * n for _ in range(SB_PAIRS)]
        ac = [[None] * n for _ in range(SB_PAIRS)]
        for step in range(n + 2):
            for p in range(SB_PAIRS):
                if step < n:
                    st[p][step] = scores(p, *blocks[step])
            for p in range(SB_PAIRS):
                if 0 <= step - 1 < n:
                    ac[p][step - 1], runs[p] = weights(p, st[p][step - 1], runs[p])
            for p in range(SB_PAIRS):
                if 0 <= step - 2 < n:
                    values(p, blocks[step - 2][0], ac[p][step - 2])
        return tuple(runs)

    zero_runs = tuple(jnp.zeros((1, 2 * tq), F32) for _ in range(SB_PAIRS))
    diagonal = [(qi * nblk + rel, rel) for rel in reversed(range(nblk))]
    n_chunks = qi * (nblk // SB_CHUNK)

    def chunk_blocks(it):
        base = (n_chunks - 1 - it) * SB_CHUNK
        return [(base + j, None) for j in reversed(range(SB_CHUNK))]

    @pl.when(qi == 0)
    def _():
        run_blocks(diagonal, zero_runs)

    @pl.when(qi > 0)
    def _():
        runs = run_blocks(diagonal + chunk_blocks(0), zero_runs)

        def alive(runs):
            lowest = functools.reduce(jnp.minimum, [jnp.min(r) for r in runs])
            return (lowest < SB_DEAD_RUN).astype(jnp.int32)

        def more(carry):
            it, _, go = carry
            return jnp.logical_and(it < n_chunks, go > 0)

        def chunk(carry):
            it, runs, _ = carry
            runs = run_blocks(chunk_blocks(it), runs)
            return it + 1, runs, alive(runs)

        lax.while_loop(more, chunk, (jnp.int32(1), runs, alive(runs)))

    o_ref[0] = jnp.concatenate([acc_scr[p].T for p in range(SB_PAIRS)], axis=-1).astype(o_ref.dtype)


def _sb_attention(proj, bsz, seq):
    e = proj.shape[-1]
    view = proj.reshape(bsz, seq, e)
    w = SB_PAIRS * LANES
    qo, ko, vo = COL_SB // w, (COL_SB + SB_WIDTH) // w, (COL_SB + 2 * SB_WIDTH) // w
    out = pl.pallas_call(
        functools.partial(_sb_attn_kernel, seq=seq),
        grid=(bsz, SB_WIDTH // w, seq // SB_TQ),
        in_specs=[pl.BlockSpec((1, SB_TQ, w), lambda b, g, i: (b, i, qo + g)),
                  pl.BlockSpec((1, seq, w), lambda b, g, i: (b, 0, ko + g)),
                  pl.BlockSpec((1, seq, w), lambda b, g, i: (b, 0, vo + g))],
        out_specs=pl.BlockSpec((1, SB_TQ, w), lambda b, g, i: (b, i, g)),
        out_shape=jax.ShapeDtypeStruct((bsz, seq, SB_WIDTH), BF16),
        scratch_shapes=[pltpu.VMEM((SB_PAIRS, seq // SB_TK, LANES, 2 * SB_TK), BF16),
                        pltpu.VMEM((SB_PAIRS, LANES, SB_TQ), F32)],
        compiler_params=_cparams("arbitrary", "arbitrary", "arbitrary"),
        name="stick_breaking_attn",
    )(view, view, view)
    return out.reshape(bsz * seq, SB_WIDTH)


def _conv_kernel(val_ref, gate_ref, w_ref, b_ref, g_ref, beta_ref, o_ref, u_scr, y_scr, *, seq):
    tr, pad, slabs = CONV_ROWS, CONV_PAD, CONV_CH // LANES
    lanes = [slice(j * LANES, (j + 1) * LANES) for j in range(slabs)]
    for j in range(slabs):
        u_scr[j, pl.ds(0, pad), :] = jnp.zeros((pad, LANES), F32)

    def glu(i, c):
        rows = pl.ds(pl.multiple_of(i * 256, 256), 256)
        val = val_ref[0, rows, :].astype(F32)
        gate = gate_ref[0, rows, :].astype(F32)
        u = val * _sigmoid(gate)
        for j in range(slabs):
            u_scr[j, pl.ds(pl.multiple_of(pad + i * 256, SUBLANES), 256), :] = u[:, lanes[j]]
        return c

    lax.fori_loop(0, seq // 256, glu, 0)

    first = pad - (CONV_WIDTH - 1)

    def tile(i, c):
        t0 = pl.multiple_of(i * tr, tr)
        groups = tr // 2 // SUBLANES
        acc = [[jnp.zeros((groups, SUBLANES, LANES), F32) for _ in range(slabs)] for _ in range(2)]
        for j in range(slabs):
            taps = [jnp.broadcast_to(w_ref[w:w + 1, lanes[j]], (SUBLANES, LANES))[None]
                    for w in range(CONV_WIDTH)]
            for k in range(CONV_WIDTH + 1):
                x = u_scr[j, pl.ds(t0 + (first + k), tr // 2, stride=2), :].reshape(groups, SUBLANES, LANES)
                if k < CONV_WIDTH:
                    acc[0][j] = acc[0][j] + x * taps[k]
                if k >= 1:
                    acc[1][j] = acc[1][j] + x * taps[k - 1]
        for phase in range(2):
            y = jnp.concatenate([a.reshape(tr // 2, LANES) for a in acc[phase]], axis=-1) + b_ref[...]
            mu = jnp.mean(y, axis=-1, keepdims=True)
            yc = y - mu
            var = jnp.mean(yc * yc, axis=-1, keepdims=True)
            yn = yc * lax.rsqrt(var + EPS) * g_ref[...] + beta_ref[...]
            out = yn * _sigmoid(yn)
            for j in range(slabs):
                y_scr[j, pl.ds(phase, tr // 2, stride=2), :] = out[:, lanes[j]]
        o_ref[0, pl.ds(t0, tr), :] = jnp.concatenate([y_scr[j] for j in range(slabs)],
                                                     axis=-1).astype(o_ref.dtype)
        return c

    lax.fori_loop(0, seq // tr, tile, 0)


def _conformer_conv(proj, conv_w, conv_b, norm_g, norm_b, bsz, seq):
    e = proj.shape[-1]
    view = proj.reshape(bsz, seq, e)
    c = CONV_CH
    voff, goff = COL_GLU // c, (COL_GLU + c) // c
    const = lambda b: (0, 0)
    out = pl.pallas_call(
        functools.partial(_conv_kernel, seq=seq),
        grid=(bsz,),
        in_specs=[pl.BlockSpec((1, seq, c), lambda b: (b, 0, voff)),
                  pl.BlockSpec((1, seq, c), lambda b: (b, 0, goff)),
                  pl.BlockSpec((CONV_WIDTH, c), const),
                  pl.BlockSpec((1, c), const), pl.BlockSpec((1, c), const), pl.BlockSpec((1, c), const)],
        out_specs=pl.BlockSpec((1, seq, c), lambda b: (b, 0, 0)),
        out_shape=jax.ShapeDtypeStruct((bsz, seq, c), BF16),
        scratch_shapes=[pltpu.VMEM((c // LANES, seq + CONV_PAD, LANES), F32),
                        pltpu.VMEM((c // LANES, CONV_ROWS, LANES), F32)],
        compiler_params=_cparams("arbitrary"),
        name="conformer_conv",
    )(view, view, conv_w, conv_b.reshape(1, c), norm_g.reshape(1, c), norm_b.reshape(1, c))
    return out.reshape(bsz * seq, c)


def _merge_kernel(o1_ref, o2_ref, o3_ref, l1_ref, l2_ref, l3_ref, ob_ref, oc_ref,
                  ga_ref, gb_ref, gc_ref, x_ref, wa_ref, wb_ref, wc_ref, wo_ref, out_ref):
    by_lanes = lambda ref: jnp.concatenate([ref[p] for p in range(ref.shape[0])], axis=-1)
    l1, l2, l3 = by_lanes(l1_ref), by_lanes(l2_ref), by_lanes(l3_ref)
    m = jnp.maximum(jnp.maximum(l1, l2), l3)
    e1, e2, e3 = jnp.exp(l1 - m), jnp.exp(l2 - m), jnp.exp(l3 - m)
    o_a = (e1 * o1_ref[...].astype(F32) + e2 * o2_ref[...].astype(F32)
           + e3 * o3_ref[...].astype(F32)) / (e1 + e2 + e3)
    ya = jnp.dot(o_a.astype(BF16), wa_ref[...], preferred_element_type=F32)
    yb = jnp.dot(ob_ref[...], wb_ref[...], preferred_element_type=F32)
    yc = jnp.dot(oc_ref[...], wc_ref[...], preferred_element_type=F32)
    merged = (_sigmoid(ga_ref[...].astype(F32)) * ya + _sigmoid(gb_ref[...].astype(F32)) * yb
              + _sigmoid(gc_ref[...].astype(F32)) * yc)
    out_ref[...] = x_ref[...] + jnp.dot(merged.astype(BF16), wo_ref[...], preferred_element_type=F32)


def _merge(o_groups, lse_groups, o_b, o_c, proj, x, wa, wb, wc, wo, *, tm):
    n, d = x.shape
    row = lambda w, j=0: pl.BlockSpec((tm, w), lambda i: (i, j))
    full = lambda a: pl.BlockSpec(a.shape, lambda i: (0, 0), pipeline_mode=pl.Buffered(1))
    g0 = COL_GATES // d
    _, pairs, seq, _ = lse_groups[0].shape
    per_batch = seq // tm
    lse = pl.BlockSpec((None, pairs, tm, LANES), lambda i: (i // per_batch, 0, i % per_batch, 0))
    return pl.pallas_call(
        _merge_kernel,
        grid=(n // tm,),
        in_specs=[row(DIL_OUT), row(DIL_OUT), row(DIL_OUT), lse, lse, lse,
                  row(SB_WIDTH), row(CONV_CH), row(d, g0), row(d, g0 + 1), row(d, g0 + 2), row(d),
                  full(wa), full(wb), full(wc), full(wo)],
        out_specs=row(d),
        out_shape=jax.ShapeDtypeStruct((n, d), F32),
        compiler_params=_cparams("arbitrary"),
        name="branch_merge_out_proj",
    )(*o_groups, *lse_groups, o_b, o_c, proj, proj, proj, x, wa, wb, wc, wo)


MXU_COLS = 256


def _col_pieces(lo, hi):
    return [(p, min(p + MXU_COLS, hi)) for p in range(lo, hi, MXU_COLS)]


FF_CHUNKS = ((0, FF_SPLIT), (FF_SPLIT, D_FF))
SWIGLU_HOOKS = sum(2 * len(_col_pieces(lo, hi)) + len(_col_pieces(0, D_MODEL)) for lo, hi in FF_CHUNKS)


def _swiglu_tile(h, wg_ref, wu_ref, wd_ref, hook=None):
    def matmul(x, w_ref, rows, lo, hi):
        if hook is None:
            return jnp.dot(x, w_ref[rows, lo:hi], preferred_element_type=F32)
        pieces = []
        for p, q in _col_pieces(lo, hi):
            pieces.append(jnp.dot(x, w_ref[rows, p:q], preferred_element_type=F32))
            hook()
        return jnp.concatenate(pieces, axis=-1)

    y = None
    for lo, hi in FF_CHUNKS:
        a = matmul(h, wg_ref, slice(None), lo, hi)
        u = matmul(h, wu_ref, slice(None), lo, hi)
        act = (a * _sigmoid(a) * u).astype(BF16)
        part = matmul(act, wd_ref, slice(lo, hi), 0, D_MODEL)
        y = part if y is None else y + part
    return y


def _dense_ffn_kernel(x_ref, g_ref, wg_ref, wu_ref, wd_ref, o_ref):
    x = x_ref[...]
    ms = jnp.mean(x * x, axis=-1, keepdims=True)
    h = ((x * lax.rsqrt(ms + EPS)) * g_ref[...]).astype(BF16)
    o_ref[...] = x + _swiglu_tile(h, wg_ref, wu_ref, wd_ref)


def _dense_ffn(x, g, wg, wu, wd, *, tm):
    n, d = x.shape
    full = lambda a: pl.BlockSpec(a.shape, lambda i: (0, 0), pipeline_mode=pl.Buffered(1))
    return pl.pallas_call(
        _dense_ffn_kernel,
        grid=(n // tm,),
        in_specs=[pl.BlockSpec((tm, d), lambda i: (i, 0)), pl.BlockSpec((1, d), lambda i: (0, 0)),
                  full(wg), full(wu), full(wd)],
        out_specs=pl.BlockSpec((tm, d), lambda i: (i, 0)),
        out_shape=jax.ShapeDtypeStruct((n, d), F32),
        compiler_params=_cparams("arbitrary"),
        name="dense_swiglu",
    )(x, g.reshape(1, d), wg, wu, wd)


def _router_kernel(x_ref, g_ref, wr_ref, br_ref, h_ref, route_ref):
    x = x_ref[...]
    ms = jnp.mean(x * x, axis=-1, keepdims=True)
    h = (x * lax.rsqrt(ms + EPS)) * g_ref[...]
    _store_token_tiles(h_ref, h)
    h_hi = h.astype(BF16)
    h_lo = (h - h_hi.astype(F32)).astype(BF16)
    w = wr_ref[...]
    w_hi = w.astype(BF16)
    w_lo = (w - w_hi.astype(F32)).astype(BF16)
    logits = (jnp.dot(h_hi, w_hi, preferred_element_type=F32)
              + jnp.dot(h_lo, w_hi, preferred_element_type=F32)
              + jnp.dot(h_hi, w_lo, preferred_element_type=F32)) + br_ref[...]
    lane = lax.broadcasted_iota(jnp.int32, logits.shape, 1)
    m1 = jnp.max(logits, axis=-1, keepdims=True)
    i1 = jnp.min(jnp.where(logits == m1, lane, LANES), axis=-1, keepdims=True)
    rest = jnp.where(lane == i1, -jnp.inf, logits)
    m2 = jnp.max(rest, axis=-1, keepdims=True)
    i2 = jnp.min(jnp.where(rest == m2, lane, LANES), axis=-1, keepdims=True)
    e2 = jnp.exp(m2 - m1)
    g1 = 1.0 / (1.0 + e2)
    g2 = e2 / (1.0 + e2)
    route = jnp.where(lane == 0, i1.astype(F32),
                      jnp.where(lane == 1, i2.astype(F32),
                                jnp.where(lane == 2, g1, jnp.where(lane == 3, g2, 0.0))))
    route_ref[...] = route


def _router(x, g, w_router, b_router, *, tm):
    n, d = x.shape
    wr = jnp.zeros((d, LANES), F32).at[:, :N_EXPERTS].set(w_router)
    br = jnp.full((1, LANES), NEG_BIG, F32).at[0, :N_EXPERTS].set(b_router)
    return pl.pallas_call(
        _router_kernel,
        grid=(n // tm,),
        in_specs=[pl.BlockSpec((tm, d), lambda i: (i, 0)), pl.BlockSpec((1, d), lambda i: (0, 0)),
                  pl.BlockSpec((d, LANES), lambda i: (0, 0)), pl.BlockSpec((1, LANES), lambda i: (0, 0))],
        out_specs=[pl.BlockSpec((tm * ROW_TILE, LANES), lambda i: (i, 0)),
                   pl.BlockSpec((tm, LANES), lambda i: (i, 0))],
        out_shape=[jax.ShapeDtypeStruct((n * ROW_TILE, LANES), F32), jax.ShapeDtypeStruct((n, LANES), F32)],
        compiler_params=_cparams("arbitrary"),
        name="router_top2",
    )(x, g.reshape(1, d), wr, br)


ROW_TILE = D_MODEL // LANES


def _store_token_tiles(ref, x):
    rows = x.shape[0]
    for c in range(ROW_TILE):
        ref[pl.ds(c, rows, stride=ROW_TILE), :] = x[:, c * LANES:(c + 1) * LANES]


def _load_token_tiles(ref, rows):
    return jnp.concatenate([ref[pl.ds(c, rows, stride=ROW_TILE), :] for c in range(ROW_TILE)], axis=-1)


def _tile_rows(t):
    start = t * ROW_TILE
    return pl.ds(start if isinstance(start, int) else pl.multiple_of(start, ROW_TILE), ROW_TILE)


class _BlockRows:
    def __init__(self, dst_ref, n_tokens):
        self.dst_ref = dst_ref
        self.last_token = n_tokens - 1

    def gather_row(self, blk, r, h_hbm, xbuf, sem):
        tok = lax.shift_right_logical(self.dst_ref[blk, r], TOP_K.bit_length() - 1)
        tok = jnp.minimum(tok, self.last_token)
        pltpu.make_async_copy(h_hbm.at[_tile_rows(tok), :], xbuf.at[_tile_rows(r), :],
                              sem).start(priority=self._queue(r))

    def scatter_row(self, blk, r, ybuf, y_hbm, sem):
        pltpu.make_async_copy(ybuf.at[_tile_rows(r), :], y_hbm.at[_tile_rows(self.dst_ref[blk, r]), :],
                              sem).start(priority=self._queue(r))

    @staticmethod
    def _queue(r):
        return r % 2 if isinstance(r, int) else 0

    def all_rows(self, row_fn):
        def body(g, c):
            for j in range(SUBLANES):
                row_fn(g * SUBLANES + j)
            return c

        lax.fori_loop(0, MOE_TM // SUBLANES, body, 0)

    @staticmethod
    def wait_gather(h_hbm, xbuf, sem):
        pltpu.make_async_copy(h_hbm.at[pl.ds(0, MOE_TM * ROW_TILE), :], xbuf, sem).wait()

    @staticmethod
    def wait_scatter(ybuf, y_hbm, sem):
        pltpu.make_async_copy(ybuf, y_hbm.at[pl.ds(0, MOE_TM * ROW_TILE), :], sem).wait()


def _expert_kernel(be_ref, used_ref, dst_ref, h_hbm, wg_ref, wu_ref, wd_ref, y_hbm,
                   xbuf, ybuf, gsem, ssem, *, n_tokens):
    i = pl.program_id(0)
    used = used_ref[0]
    slot = i % 2
    rows = _BlockRows(dst_ref, n_tokens)
    gather_hooks = SWIGLU_HOOKS // 2
    share = lambda k, hooks: range(k * MOE_TM // hooks, (k + 1) * MOE_TM // hooks)

    def compute(scatter_previous):
        _BlockRows.wait_gather(h_hbm, xbuf.at[slot], gsem.at[slot])
        x = _load_token_tiles(xbuf.at[slot], MOE_TM).astype(BF16)
        nxt = jnp.minimum(i + 1, used - 1)
        calls = [0]

        def start_some():
            k = calls[0]
            calls[0] += 1
            if k < gather_hooks:
                for r in share(k, gather_hooks):
                    rows.gather_row(nxt, r, h_hbm, xbuf.at[1 - slot], gsem.at[1 - slot])
            if scatter_previous:
                for r in share(k, SWIGLU_HOOKS):
                    rows.scatter_row(i - 1, r, ybuf.at[1 - slot], y_hbm, ssem.at[1 - slot])

        y = _swiglu_tile(x, wg_ref, wu_ref, wd_ref, hook=start_some)
        assert calls[0] == SWIGLU_HOOKS

        @pl.when(i >= 2)
        def _():
            _BlockRows.wait_scatter(ybuf.at[slot], y_hbm, ssem.at[slot])

        _store_token_tiles(ybuf.at[slot], y)

    @pl.when(i == 0)
    def _():
        rows.all_rows(lambda r: rows.gather_row(0, r, h_hbm, xbuf.at[0], gsem.at[0]))
        ybuf[1] = jnp.zeros((MOE_TM * ROW_TILE, LANES), F32)
        for half in range(2):
            clear = pltpu.make_async_copy(
                ybuf.at[1],
                y_hbm.at[pl.ds((n_tokens * TOP_K + half * MOE_TM) * ROW_TILE, MOE_TM * ROW_TILE), :],
                ssem.at[half])
            clear.start()
            clear.wait()
        compute(False)

    @pl.when(jnp.logical_and(i >= 1, i < used))
    def _():
        compute(True)

    @pl.when(i == used)
    def _():
        _BlockRows.wait_gather(h_hbm, xbuf.at[slot], gsem.at[slot])
        _BlockRows.wait_scatter(ybuf.at[slot], y_hbm, ssem.at[slot])
        rows.all_rows(lambda r: rows.scatter_row(i - 1, r, ybuf.at[1 - slot], y_hbm, ssem.at[1 - slot]))
        _BlockRows.wait_scatter(ybuf.at[1 - slot], y_hbm, ssem.at[1 - slot])


def _experts(h, block_expert, used, row_dst, wg, wu, wd, n_tokens):
    d = D_MODEL
    n_blocks = row_dst.shape[0]
    f = wg.shape[-1]
    assert n_tokens * TOP_K >= 2 * MOE_TM
    one = pl.Buffered(1)
    grid_spec = pltpu.PrefetchScalarGridSpec(
        num_scalar_prefetch=3,
        grid=(n_blocks,),
        in_specs=[pl.BlockSpec(memory_space=pl.ANY),
                  pl.BlockSpec((None, d, f), lambda i, be, us, ds: (be[i], 0, 0), pipeline_mode=one),
                  pl.BlockSpec((None, d, f), lambda i, be, us, ds: (be[i], 0, 0), pipeline_mode=one),
                  pl.BlockSpec((None, f, d), lambda i, be, us, ds: (be[i], 0, 0), pipeline_mode=one)],
        out_specs=pl.BlockSpec(memory_space=pl.ANY),
        scratch_shapes=[pltpu.VMEM((2, MOE_TM * ROW_TILE, LANES), F32),
                        pltpu.VMEM((2, MOE_TM * ROW_TILE, LANES), F32),
                        pltpu.SemaphoreType.DMA((2,)), pltpu.SemaphoreType.DMA((2,))],
    )
    return pl.pallas_call(
        functools.partial(_expert_kernel, n_tokens=n_tokens),
        grid_spec=grid_spec,
        out_shape=jax.ShapeDtypeStruct(((n_tokens * TOP_K + 2 * MOE_TM) * ROW_TILE, LANES), F32),
        compiler_params=_cparams("arbitrary"),
        name="expert_swiglu",
    )(block_expert, used, row_dst, h, wg, wu, wd)


def _combine_kernel(y_ref, x_ref, route_ref, o_ref):
    tt = x_ref.shape[0]
    route = route_ref[...]
    out = x_ref[...]
    for k in range(TOP_K):
        yk = jnp.concatenate([y_ref[pl.ds(k * ROW_TILE + c, tt, stride=TOP_K * ROW_TILE), :]
                              for c in range(ROW_TILE)], axis=-1)
        out = out + route[:, TOP_K + k:TOP_K + k + 1] * yk
    o_ref[...] = out


def _combine(x, ys, route):
    n, d = x.shape
    tt = COMBINE_TT
    return pl.pallas_call(
        _combine_kernel,
        grid=(n // tt,),
        in_specs=[pl.BlockSpec((tt * TOP_K * ROW_TILE, LANES), lambda i: (i, 0)),
                  pl.BlockSpec((tt, d), lambda i: (i, 0)),
                  pl.BlockSpec((tt, LANES), lambda i: (i, 0))],
        out_specs=pl.BlockSpec((tt, d), lambda i: (i, 0)),
        out_shape=jax.ShapeDtypeStruct((n, d), F32),
        compiler_params=_cparams("arbitrary"),
        name="expert_combine",
    )(ys, x, route)


def _routed_ffn(x, g, w_router, b_router, wg, wu, wd, *, tm):
    n, d = x.shape
    h, route = _router(x, g, w_router, b_router, tm=tm)
    e_flat = route[:, :TOP_K].astype(jnp.int32).reshape(-1)
    onehot = (e_flat[:, None] == jnp.arange(N_EXPERTS)[None, :]).astype(jnp.int32)
    csum = jnp.cumsum(onehot, axis=0)
    counts = csum[-1]
    rank = jnp.sum((csum - onehot) * onehot, axis=1)
    padded = (counts + MOE_TM - 1) // MOE_TM * MOE_TM
    pend = jnp.cumsum(padded)
    pstart = pend - padded
    dest = (pstart[e_flat] + rank).astype(jnp.int32)
    n_blocks = (n * TOP_K) // MOE_TM + N_EXPERTS
    scratch_rows = n * TOP_K + jnp.arange(n_blocks * MOE_TM, dtype=jnp.int32) % (2 * MOE_TM)
    row_dst = scratch_rows.at[dest].set(jnp.arange(n * TOP_K, dtype=jnp.int32)).reshape(n_blocks, MOE_TM)
    block_expert = jnp.clip(jnp.searchsorted(pend, jnp.arange(n_blocks) * MOE_TM, side='right'),
                            0, N_EXPERTS - 1).astype(jnp.int32)
    used = (pend[-1:] // MOE_TM).astype(jnp.int32)
    ys = _experts(h, block_expert, used, row_dst, wg, wu, wd, n)
    return _combine(x, ys, route)


def kernel(x, attn_norm_g, w_in, q_norm_g, k_norm_g, conv_w, conv_b, conv_norm_g, conv_norm_b,
           w_branch_a, w_branch_b, w_branch_c, w_out, ffn_norm_g, w_ffn_gate, w_ffn_up,
           w_ffn_down, w_router, b_router, w_exp_gate, w_exp_up, w_exp_down):
    bsz, seq, d = x.shape
    depth = attn_norm_g.shape[0]
    n = bsz * seq
    tm = 512
    xf = x.reshape(n, d)
    s_dil, s_sb, s_glu = 3 * DIL_WIDTH, 3 * DIL_WIDTH + 3 * SB_WIDTH, 3 * DIL_WIDTH + 3 * SB_WIDTH + 2 * CONV_CH
    for layer in range(depth):
        wl = w_in[layer]
        w_perm = jnp.concatenate([wl[:, s_glu:], wl[:, s_sb:s_glu], wl[:, s_dil:s_sb], wl[:, :s_dil]],
                                 axis=1).astype(BF16)
        proj_a = _norm_matmul(xf, attn_norm_g[layer], w_perm[:, :PROJ_A_COLS], tm=tm)
        proj_b = _norm_matmul(xf, attn_norm_g[layer], w_perm[:, PROJ_A_COLS:], tm=tm)
        o_groups, lse_groups = [], []
        for gi, (window, dilation) in enumerate(DIL_GROUPS):
            o_g, lse_g = _dil_attention(proj_b, q_norm_g[layer], k_norm_g[layer], gi, window, dilation,
                                        bsz, seq)
            o_groups.append(o_g)
            lse_groups.append(lse_g)
        o_b = _sb_attention(proj_b, bsz, seq)
        o_c = _conformer_conv(proj_a, conv_w[layer], conv_b[layer], conv_norm_g[layer],
                              conv_norm_b[layer], bsz, seq)
        xf = _merge(o_groups, lse_groups, o_b, o_c, proj_a, xf,
                    w_branch_a[layer].astype(BF16), w_branch_b[layer].astype(BF16),
                    w_branch_c[layer].astype(BF16), w_out[layer].astype(BF16), tm=tm)
        i = layer // 2
        if layer % 2 == 0:
            xf = _dense_ffn(xf, ffn_norm_g[layer], w_ffn_gate[i].astype(BF16), w_ffn_up[i].astype(BF16),
                            w_ffn_down[i].astype(BF16), tm=tm)
        else:
            xf = _routed_ffn(xf, ffn_norm_g[layer], w_router[i], b_router[i],
                             w_exp_gate[i].astype(BF16), w_exp_up[i].astype(BF16),
                             w_exp_down[i].astype(BF16), tm=tm)
    return xf.reshape(bsz, seq, d)
```

```python
import functools
import math

import jax
import jax.numpy as jnp
from jax import lax
from jax.experimental import pallas as pl
from jax.experimental.pallas import tpu as pltpu

F32 = jnp.float32
BF16 = jnp.bfloat16

D_MODEL = 1024
HEAD_DIM = 64
DIL_GROUPS = ((128, 1), (512, 4), (2048, 16))
DIL_HEADS_PER_GROUP = 4
DIL_HEADS = len(DIL_GROUPS) * DIL_HEADS_PER_GROUP
DIL_WIDTH = DIL_HEADS * HEAD_DIM
DIL_OUT = DIL_HEADS_PER_GROUP * HEAD_DIM
DIL_BLOCK = 128
SB_HEADS = 8
SB_WIDTH = SB_HEADS * HEAD_DIM
CONV_CH = D_MODEL // 2
CONV_WIDTH = 31
N_BRANCH = 3
IN_COLS = 3 * DIL_WIDTH + 3 * SB_WIDTH + 2 * CONV_CH + N_BRANCH * D_MODEL
D_FF = 2816
N_EXPERTS = 8
TOP_K = 2
EPS = 1e-6
ALIBI_MAX_BIAS = 8.0
NEG_BIG = -1e30

COL_GATES = 0
COL_GLU = COL_GATES + N_BRANCH * D_MODEL
PROJ_A_COLS = COL_GLU + 2 * CONV_CH
COL_SB = 0
COL_DIL = COL_SB + 3 * SB_WIDTH

LANES = 128
SUBLANES = 8
VMEM_LIMIT = 56 * 1024 * 1024

FAST_STRIDE = 4
DIL_UNITS = 4
SB_TQ = 512
SB_TK = 128
SB_PAIRS = 2
SB_CHUNK = 2
SB_DEAD_RUN = 128.0
CONV_ROWS = 64
CONV_PAD = 32
FF_SPLIT = 1536
MOE_TM = 512
COMBINE_TT = 512


def _cparams(*sem):
    return pltpu.CompilerParams(dimension_semantics=sem, vmem_limit_bytes=VMEM_LIMIT)


def _sigmoid(x):
    return 1.0 / (1.0 + jnp.exp(-x))


def _norm_matmul_kernel(x_ref, g_ref, w_ref, o_ref):
    x = x_ref[...]
    ms = jnp.mean(x * x, axis=-1, keepdims=True)
    h = (x * lax.rsqrt(ms + EPS)) * g_ref[...]
    o_ref[...] = jnp.dot(h.astype(BF16), w_ref[...], preferred_element_type=F32).astype(o_ref.dtype)


def _norm_matmul(x, g, w, *, tm):
    n, d = x.shape
    e = w.shape[1]
    return pl.pallas_call(
        _norm_matmul_kernel,
        grid=(n // tm,),
        in_specs=[pl.BlockSpec((tm, d), lambda i: (i, 0)),
                  pl.BlockSpec((1, d), lambda i: (0, 0)),
                  pl.BlockSpec((d, e), lambda i: (0, 0), pipeline_mode=pl.Buffered(1))],
        out_specs=pl.BlockSpec((tm, e), lambda i: (i, 0)),
        out_shape=jax.ShapeDtypeStruct((n, e), BF16),
        compiler_params=_cparams("arbitrary"),
        name="norm_in_proj",
    )(x, g.reshape(1, d), w)


def _same_head_matrix(w):
    r = lax.broadcasted_iota(jnp.int32, (w, w), 0) // HEAD_DIM
    c = lax.broadcasted_iota(jnp.int32, (w, w), 1) // HEAD_DIM
    return (r == c).astype(BF16)


def _head_rms_scale(t, same_head):
    ssq = jnp.dot((t * t).astype(BF16), same_head, preferred_element_type=F32)
    return lax.rsqrt(ssq * (1.0 / HEAD_DIM) + EPS)


def _dil_attn_kernel(q_ref, k_ref, v_ref, qg_ref, kg_ref, bias_ref, o_ref, lse_ref,
                     qn_scr, kn_scr, v_scr, o_scr, tmp_scr, *, seq, dilation):
    blk, d = DIL_BLOCK, dilation
    span = blk * d
    log2d = d.bit_length() - 1
    same_head = _same_head_matrix(DIL_OUT)
    qg = qg_ref[...] * (1.0 / math.sqrt(HEAD_DIM))
    kg = kg_ref[...]
    chunk = 256

    pairs = DIL_OUT // LANES
    pair_lanes = [slice(p * LANES, (p + 1) * LANES) for p in range(pairs)]
    d1 = d // FAST_STRIDE if d > FAST_STRIDE else 1
    d2 = d // d1
    kv_rows = span + seq

    def park(scr, slot, p, first_part_row, total_rows, x):
        if d1 == 1:
            scr[p, pl.ds(pl.multiple_of(first_part_row, blk), chunk), :] = x
            return
        tmp_scr[slot, p] = x
        for m in range(d1):
            dst = pl.ds(pl.multiple_of(m * (total_rows // d1) + first_part_row, SUBLANES), chunk // d1)
            scr[p, dst, :] = tmp_scr[slot, p, pl.ds(m, chunk // d1, stride=d1), :]

    for p in range(pairs):
        for m in range(d1):
            front = pl.ds(m * (kv_rows // d1), span // d1)
            kn_scr[p, front, :] = jnp.zeros((span // d1, LANES), F32)
            v_scr[p, front, :] = jnp.zeros((span // d1, LANES), F32)

    def norm(i, c):
        rows = pl.ds(pl.multiple_of(i * chunk, chunk), chunk)
        q = q_ref[0, rows, :].astype(F32)
        qn = q * _head_rms_scale(q, same_head) * qg
        k = k_ref[0, rows, :].astype(F32)
        kn = k * _head_rms_scale(k, same_head) * kg
        v = v_ref[0, rows, :].astype(F32)
        for p in range(pairs):
            park(qn_scr, 0, p, i * (chunk // d1), seq, qn[:, pair_lanes[p]])
            park(kn_scr, 1, p, span // d1 + i * (chunk // d1), kv_rows, kn[:, pair_lanes[p]])
            park(v_scr, 2, p, span // d1 + i * (chunk // d1), kv_rows, v[:, pair_lanes[p]])
        return c

    lax.fori_loop(0, seq // chunk, norm, 0)

    low_half = lax.broadcasted_iota(jnp.int32, (blk, LANES), 1) < HEAD_DIM

    def sub_rows(base, count):
        return pl.ds(pl.multiple_of(base, blk), count) if d == 1 else pl.ds(base, count, stride=d)

    def parked_rows(base, count, total_rows):
        if d1 == 1:
            return sub_rows(base, count)
        start = (base & (d1 - 1)) * (total_rows // d1) + lax.shift_right_logical(base, d1.bit_length() - 1)
        return pl.ds(start, count, stride=d2)

    def load(u):
        if d == 1:
            n, base = u, u * span
        else:
            n = u >> log2d
            base = (u & (d - 1)) + n * span
        q = [qn_scr[p, parked_rows(base, blk, seq), :] for p in range(pairs)]
        kk = [kn_scr[p, parked_rows(base, 2 * blk, kv_rows), :].astype(BF16) for p in range(pairs)]
        vv = [v_scr[p, parked_rows(base, 2 * blk, kv_rows), :].astype(BF16) for p in range(pairs)]
        first = jnp.where(n == 0, 1, 0)
        return base, q, kk, vv, first

    def scores(unit):
        _, q, kk, _, first = unit
        out = []
        for h in range(DIL_HEADS_PER_GROUP):
            keep = low_half if h % 2 == 0 else jnp.logical_not(low_half)
            qh = jnp.where(keep, q[h // 2], 0.0).astype(BF16)
            s = lax.dot_general(qh, kk[h // 2], (((1,), (1,)), ((), ())), preferred_element_type=F32)
            out.append(s + bias_ref[h, first])
        return out

    def finish(unit, s_list):
        base, _, _, vv, _ = unit
        rows = sub_rows(base, blk)
        for p in range(pairs):
            oh, lh = [], []
            for hh in range(2):
                s = s_list[2 * p + hh]
                m = jnp.max(s, axis=-1, keepdims=True)
                e = jnp.exp(s - m)
                den = jnp.sum(e, axis=-1, keepdims=True)
                oh.append(jnp.dot(e.astype(BF16), vv[p], preferred_element_type=F32) * (1.0 / den))
                lh.append(m + jnp.log(den))
            o_scr[p, rows, :] = jnp.where(low_half, oh[0], oh[1])
            lse_ref[0, p, rows, :] = jnp.where(low_half, lh[0], lh[1])

    def body(it, c):
        units = [load(it * DIL_UNITS + i) for i in range(DIL_UNITS)]
        s = [None] * DIL_UNITS
        for step in range(DIL_UNITS + 1):
            if step < DIL_UNITS:
                s[step] = scores(units[step])
            if step >= 1:
                finish(units[step - 1], s[step - 1])
        return c

    lax.fori_loop(0, seq // blk // DIL_UNITS, body, 0)

    def emit(i, c):
        rows = pl.ds(pl.multiple_of(i * chunk, chunk), chunk)
        o_ref[0, rows, :] = jnp.concatenate([o_scr[p, rows, :] for p in range(pairs)],
                                            axis=-1).astype(o_ref.dtype)
        return c

    lax.fori_loop(0, seq // chunk, emit, 0)


def _dil_bias_table(group, window, dilation):
    reach = window // dilation
    assert reach <= DIL_BLOCK
    slopes = 2.0 ** (-ALIBI_MAX_BIAS * jnp.arange(1, DIL_HEADS + 1, dtype=F32) / DIL_HEADS)
    slopes = slopes[group * DIL_HEADS_PER_GROUP:(group + 1) * DIL_HEADS_PER_GROUP]
    qi = jnp.arange(DIL_BLOCK)[:, None] + DIL_BLOCK
    ki = jnp.arange(2 * DIL_BLOCK)[None, :]
    dist = qi - ki
    valid = (dist >= 0) & (dist <= reach)
    bias = -slopes[:, None, None] * (dist * dilation).astype(F32)[None]
    general = jnp.where(valid[None], bias, NEG_BIG)
    first = jnp.where((valid & (ki >= DIL_BLOCK))[None], bias, NEG_BIG)
    return jnp.stack([general, first], axis=1)


def _dil_attention(proj, q_gain, k_gain, group, window, dilation, bsz, seq):
    e = proj.shape[-1]
    assert (seq // DIL_BLOCK) % DIL_UNITS == 0
    assert seq % (DIL_BLOCK * dilation) == 0 and dilation & (dilation - 1) == 0
    view = proj.reshape(bsz, seq, e)
    w = DIL_OUT
    pairs = w // LANES
    pad = DIL_BLOCK * dilation

    def col(base):
        off = (base + group * w) // w
        return lambda b: (b, 0, off)

    bias = _dil_bias_table(group, window, dilation)
    gain = lambda g: jnp.tile(g, DIL_HEADS_PER_GROUP).reshape(1, w)
    o, lse = pl.pallas_call(
        functools.partial(_dil_attn_kernel, seq=seq, dilation=dilation),
        grid=(bsz,),
        in_specs=[pl.BlockSpec((1, seq, w), col(COL_DIL)),
                  pl.BlockSpec((1, seq, w), col(COL_DIL + DIL_WIDTH)),
                  pl.BlockSpec((1, seq, w), col(COL_DIL + 2 * DIL_WIDTH)),
                  pl.BlockSpec((1, w), lambda b: (0, 0)),
                  pl.BlockSpec((1, w), lambda b: (0, 0)),
                  pl.BlockSpec(bias.shape, lambda b: (0, 0, 0, 0))],
        out_specs=[pl.BlockSpec((1, seq, w), lambda b: (b, 0, 0)),
                   pl.BlockSpec((1, pairs, seq, LANES), lambda b: (b, 0, 0, 0))],
        out_shape=[jax.ShapeDtypeStruct((bsz, seq, w), BF16),
                   jax.ShapeDtypeStruct((bsz, pairs, seq, LANES), F32)],
        scratch_shapes=[pltpu.VMEM((pairs, seq, LANES), F32), pltpu.VMEM((pairs, pad + seq, LANES), F32),
                        pltpu.VMEM((pairs, pad + seq, LANES), F32), pltpu.VMEM((pairs, seq, LANES), F32),
                        pltpu.VMEM((3, pairs, 256, LANES), F32)],
        compiler_params=_cparams("arbitrary"),
        name=f"dilated_attn_g{group}",
    )(view, view, view, gain(q_gain), gain(k_gain), bias)
    return o.reshape(bsz * seq, w), lse


def _sb_attn_kernel(q_ref, k_ref, v_ref, o_ref, vcat_scr, acc_scr, *, seq):
    tq, tk, nblk = SB_TQ, SB_TK, SB_TQ // SB_TK
    qi = pl.program_id(2)
    pair_lanes = [slice(p * LANES, (p + 1) * LANES) for p in range(SB_PAIRS)]

    def key_rows(kb):
        return pl.ds(pl.multiple_of(kb * tk, tk), tk)

    @pl.when(qi == 0)
    def _():
        chan = lax.broadcasted_iota(jnp.int32, (LANES, tk), 0)

        def build(kb, c):
            for p in range(SB_PAIRS):
                vt = v_ref[0, key_rows(kb), pair_lanes[p]].astype(F32).T
                vcat_scr[p, kb] = jnp.concatenate(
                    [jnp.where(chan < HEAD_DIM, vt, 0.0), jnp.where(chan >= HEAD_DIM, vt, 0.0)],
                    axis=1).astype(BF16)
            return c

        lax.fori_loop(0, seq // tk, build, 0, unroll=4)

    lane = lax.broadcasted_iota(jnp.int32, (tq, LANES), 1)
    qcat, qcat_t = [], []
    for p in range(SB_PAIRS):
        q = q_ref[0, :, pair_lanes[p]] * (1.0 / math.sqrt(HEAD_DIM))
        zero = jnp.zeros_like(q)
        both = jnp.concatenate([jnp.where(lane < HEAD_DIM, q, zero), jnp.where(lane >= HEAD_DIM, q, zero)],
                               axis=0)
        qcat.append(both)
        qcat_t.append(both.astype(F32).T.astype(BF16))
    neg_tri = jnp.where(lax.broadcasted_iota(jnp.int32, (tk, tk), 1)
                        >= lax.broadcasted_iota(jnp.int32, (tk, tk), 0), -1.0, 0.0).astype(BF16)
    acc_scr[...] = jnp.zeros_like(acc_scr)

    def first_query(rel):
        width = tq
        while rel is not None and tq - width // 2 <= rel * tk and width // 2 >= LANES:
            width //= 2
        return tq - width

    def window(x, lo, axis):
        if lo == 0:
            return x
        return jnp.concatenate([lax.slice_in_dim(x, lo, tq, axis=axis),
                                lax.slice_in_dim(x, tq + lo, 2 * tq, axis=axis)], axis=axis)

    def scores(p, kb, rel):
        lo = first_query(rel)
        wq = tq - lo
        kk = k_ref[0, key_rows(kb), pair_lanes[p]]
        zt = lax.dot_general(kk, window(qcat[p], lo, 0), (((1,), (1,)), ((), ())),
                             preferred_element_type=F32)
        neg_abs = lax.bitcast_convert_type(
            lax.bitcast_convert_type(zt, jnp.uint32) | jnp.uint32(0x80000000), F32)
        sp = jnp.maximum(zt, 0.0) + jnp.log(1.0 + jnp.exp(neg_abs))
        before = None
        if rel is not None:
            kpos = lax.broadcasted_iota(jnp.int32, (tk, 2 * wq), 0) + rel * tk
            qpos = (lax.broadcasted_iota(jnp.int32, (tk, 2 * wq), 1) & (wq - 1)) + lo
            before = kpos < qpos
            sp = jnp.where(before, sp, 0.0)
        return kk, zt[0:1, :], sp.astype(BF16), before, lo

    def weights(p, state, run):
        kk, zt0, sp, before, lo = state
        wq = tq - lo
        arg = jnp.dot(jnp.concatenate([neg_tri, kk], axis=1),
                      jnp.concatenate([sp, window(qcat_t[p], lo, 1)], axis=0),
                      preferred_element_type=F32)
        a = jnp.exp(arg - window(run, lo, 1))
        if before is not None:
            a = jnp.where(before, a, 0.0)
        a = a.astype(BF16)
        acat = jnp.concatenate([a[:, :wq], a[:, wq:]], axis=0)
        col_sums = zt0 - arg[0:1, :]
        if lo:
            none = jnp.zeros((1, lo), F32)
            col_sums = jnp.concatenate([none, col_sums[:, :wq], none, col_sums[:, wq:]], axis=1)
        return (acat, lo), run + col_sums

    def values(p, kb, acat_lo):
        acat, lo = acat_lo
        acc_scr[p, :, lo:] += jnp.dot(vcat_scr[p, kb], acat, preferred_element_type=F32)

    def run_blocks(blocks, runs):
        n = len(blocks)
        runs = list(runs)
        st = [[None] * n for _ in range(SB_PAIRS)]
        ac = [[None] * n for _ in range(SB_PAIRS)]
        for step in range(n + 2):
            for p in range(SB_PAIRS):
                if step < n:
                    st[p][step] = scores(p, *blocks[step])
            for p in range(SB_PAIRS):
                if 0 <= step - 1 < n:
                    ac[p][step - 1], runs[p] = weights(p, st[p][step - 1], runs[p])
            for p in range(SB_PAIRS):
                if 0 <= step - 2 < n:
                    values(p, blocks[step - 2][0], ac[p][step - 2])
        return tuple(runs)

    zero_runs = tuple(jnp.zeros((1, 2 * tq), F32) for _ in range(SB_PAIRS))
    diagonal = [(qi * nblk + rel, rel) for rel in reversed(range(nblk))]
    n_chunks = qi * (nblk // SB_CHUNK)

    def chunk_blocks(it):
        base = (n_chunks - 1 - it) * SB_CHUNK
        return [(base + j, None) for j in reversed(range(SB_CHUNK))]

    @pl.when(qi == 0)
    def _():
        run_blocks(diagonal, zero_runs)

    @pl.when(qi > 0)
    def _():
        runs = run_blocks(diagonal + chunk_blocks(0), zero_runs)

        def alive(runs):
            lowest = functools.reduce(jnp.minimum, [jnp.min(r) for r in runs])
            return (lowest < SB_DEAD_RUN).astype(jnp.int32)

        def more(carry):
            it, _, go = carry
            return jnp.logical_and(it < n_chunks, go > 0)

        def chunk(carry):
            it, runs, _ = carry
            runs = run_blocks(chunk_blocks(it), runs)
            return it + 1, runs, alive(runs)

        lax.while_loop(more, chunk, (jnp.int32(1), runs, alive(runs)))

    o_ref[0] = jnp.concatenate([acc_scr[p].T for p in range(SB_PAIRS)], axis=-1).astype(o_ref.dtype)


def _sb_attention(proj, bsz, seq):
    e = proj.shape[-1]
    view = proj.reshape(bsz, seq, e)
    w = SB_PAIRS * LANES
    qo, ko, vo = COL_SB // w, (COL_SB + SB_WIDTH) // w, (COL_SB + 2 * SB_WIDTH) // w
    out = pl.pallas_call(
        functools.partial(_sb_attn_kernel, seq=seq),
        grid=(bsz, SB_WIDTH // w, seq // SB_TQ),
        in_specs=[pl.BlockSpec((1, SB_TQ, w), lambda b, g, i: (b, i, qo + g)),
                  pl.BlockSpec((1, seq, w), lambda b, g, i: (b, 0, ko + g)),
                  pl.BlockSpec((1, seq, w), lambda b, g, i: (b, 0, vo + g))],
        out_specs=pl.BlockSpec((1, SB_TQ, w), lambda b, g, i: (b, i, g)),
        out_shape=jax.ShapeDtypeStruct((bsz, seq, SB_WIDTH), BF16),
        scratch_shapes=[pltpu.VMEM((SB_PAIRS, seq // SB_TK, LANES, 2 * SB_TK), BF16),
                        pltpu.VMEM((SB_PAIRS, LANES, SB_TQ), F32)],
        compiler_params=_cparams("arbitrary", "arbitrary", "arbitrary"),
        name="stick_breaking_attn",
    )(view, view, view)
    return out.reshape(bsz * seq, SB_WIDTH)


def _conv_kernel(val_ref, gate_ref, w_ref, b_ref, g_ref, beta_ref, o_ref, u_scr, y_scr, *, seq):
    tr, pad, slabs = CONV_ROWS, CONV_PAD, CONV_CH // LANES
    lanes = [slice(j * LANES, (j + 1) * LANES) for j in range(slabs)]
    for j in range(slabs):
        u_scr[j, pl.ds(0, pad), :] = jnp.zeros((pad, LANES), F32)

    def glu(i, c):
        rows = pl.ds(pl.multiple_of(i * 256, 256), 256)
        val = val_ref[0, rows, :].astype(F32)
        gate = gate_ref[0, rows, :].astype(F32)
        u = val * _sigmoid(gate)
        for j in range(slabs):
            u_scr[j, pl.ds(pl.multiple_of(pad + i * 256, SUBLANES), 256), :] = u[:, lanes[j]]
        return c

    lax.fori_loop(0, seq // 256, glu, 0)

    first = pad - (CONV_WIDTH - 1)

    def tile(i, c):
        t0 = pl.multiple_of(i * tr, tr)
        groups = tr // 2 // SUBLANES
        acc = [[jnp.zeros((groups, SUBLANES, LANES), F32) for _ in range(slabs)] for _ in range(2)]
        for j in range(slabs):
            taps = [jnp.broadcast_to(w_ref[w:w + 1, lanes[j]], (SUBLANES, LANES))[None]
                    for w in range(CONV_WIDTH)]
            for k in range(CONV_WIDTH + 1):
                x = u_scr[j, pl.ds(t0 + (first + k), tr // 2, stride=2), :].reshape(groups, SUBLANES, LANES)
                if k < CONV_WIDTH:
                    acc[0][j] = acc[0][j] + x * taps[k]
                if k >= 1:
                    acc[1][j] = acc[1][j] + x * taps[k - 1]
        for phase in range(2):
            y = jnp.concatenate([a.reshape(tr // 2, LANES) for a in acc[phase]], axis=-1) + b_ref[...]
            mu = jnp.mean(y, axis=-1, keepdims=True)
            yc = y - mu
            var = jnp.mean(yc * yc, axis=-1, keepdims=True)
            yn = yc * lax.rsqrt(var + EPS) * g_ref[...] + beta_ref[...]
            out = yn * _sigmoid(yn)
            for j in range(slabs):
                y_scr[j, pl.ds(phase, tr // 2, stride=2), :] = out[:, lanes[j]]
        o_ref[0, pl.ds(t0, tr), :] = jnp.concatenate([y_scr[j] for j in range(slabs)],
                                                     axis=-1).astype(o_ref.dtype)
        return c

    lax.fori_loop(0, seq // tr, tile, 0)


def _conformer_conv(proj, conv_w, conv_b, norm_g, norm_b, bsz, seq):
    e = proj.shape[-1]
    view = proj.reshape(bsz, seq, e)
    c = CONV_CH
    voff, goff = COL_GLU // c, (COL_GLU + c) // c
    const = lambda b: (0, 0)
    out = pl.pallas_call(
        functools.partial(_conv_kernel, seq=seq),
        grid=(bsz,),
        in_specs=[pl.BlockSpec((1, seq, c), lambda b: (b, 0, voff)),
                  pl.BlockSpec((1, seq, c), lambda b: (b, 0, goff)),
                  pl.BlockSpec((CONV_WIDTH, c), const),
                  pl.BlockSpec((1, c), const), pl.BlockSpec((1, c), const), pl.BlockSpec((1, c), const)],
        out_specs=pl.BlockSpec((1, seq, c), lambda b: (b, 0, 0)),
        out_shape=jax.ShapeDtypeStruct((bsz, seq, c), BF16),
        scratch_shapes=[pltpu.VMEM((c // LANES, seq + CONV_PAD, LANES), F32),
                        pltpu.VMEM((c // LANES, CONV_ROWS, LANES), F32)],
        compiler_params=_cparams("arbitrary"),
        name="conformer_conv",
    )(view, view, conv_w, conv_b.reshape(1, c), norm_g.reshape(1, c), norm_b.reshape(1, c))
    return out.reshape(bsz * seq, c)


def _merge_kernel(o1_ref, o2_ref, o3_ref, l1_ref, l2_ref, l3_ref, ob_ref, oc_ref,
                  ga_ref, gb_ref, gc_ref, x_ref, wa_ref, wb_ref, wc_ref, wo_ref, out_ref):
    by_lanes = lambda ref: jnp.concatenate([ref[p] for p in range(ref.shape[0])], axis=-1)
    l1, l2, l3 = by_lanes(l1_ref), by_lanes(l2_ref), by_lanes(l3_ref)
    m = jnp.maximum(jnp.maximum(l1, l2), l3)
    e1, e2, e3 = jnp.exp(l1 - m), jnp.exp(l2 - m), jnp.exp(l3 - m)
    o_a = (e1 * o1_ref[...].astype(F32) + e2 * o2_ref[...].astype(F32)
           + e3 * o3_ref[...].astype(F32)) / (e1 + e2 + e3)
    ya = jnp.dot(o_a.astype(BF16), wa_ref[...], preferred_element_type=F32)
    yb = jnp.dot(ob_ref[...], wb_ref[...], preferred_element_type=F32)
    yc = jnp.dot(oc_ref[...], wc_ref[...], preferred_element_type=F32)
    merged = (_sigmoid(ga_ref[...].astype(F32)) * ya + _sigmoid(gb_ref[...].astype(F32)) * yb
              + _sigmoid(gc_ref[...].astype(F32)) * yc)
    out_ref[...] = x_ref[...] + jnp.dot(merged.astype(BF16), wo_ref[...], preferred_element_type=F32)


def _merge(o_groups, lse_groups, o_b, o_c, proj, x, wa, wb, wc, wo, *, tm):
    n, d = x.shape
    row = lambda w, j=0: pl.BlockSpec((tm, w), lambda i: (i, j))
    full = lambda a: pl.BlockSpec(a.shape, lambda i: (0, 0), pipeline_mode=pl.Buffered(1))
    g0 = COL_GATES // d
    _, pairs, seq, _ = lse_groups[0].shape
    per_batch = seq // tm
    lse = pl.BlockSpec((None, pairs, tm, LANES), lambda i: (i // per_batch, 0, i % per_batch, 0))
    return pl.pallas_call(
        _merge_kernel,
        grid=(n // tm,),
        in_specs=[row(DIL_OUT), row(DIL_OUT), row(DIL_OUT), lse, lse, lse,
                  row(SB_WIDTH), row(CONV_CH), row(d, g0), row(d, g0 + 1), row(d, g0 + 2), row(d),
                  full(wa), full(wb), full(wc), full(wo)],
        out_specs=row(d),
        out_shape=jax.ShapeDtypeStruct((n, d), F32),
        compiler_params=_cparams("arbitrary"),
        name="branch_merge_out_proj",
    )(*o_groups, *lse_groups, o_b, o_c, proj, proj, proj, x, wa, wb, wc, wo)


MXU_COLS = 256


def _col_pieces(lo, hi):
    return [(p, min(p + MXU_COLS, hi)) for p in range(lo, hi, MXU_COLS)]


FF_CHUNKS = ((0, FF_SPLIT), (FF_SPLIT, D_FF))
SWIGLU_HOOKS = sum(2 * len(_col_pieces(lo, hi)) + len(_col_pieces(0, D_MODEL)) for lo, hi in FF_CHUNKS)


def _swiglu_tile(h, wg_ref, wu_ref, wd_ref, hook=None):
    def matmul(x, w_ref, rows, lo, hi):
        if hook is None:
            return jnp.dot(x, w_ref[rows, lo:hi], preferred_element_type=F32)
        pieces = []
        for p, q in _col_pieces(lo, hi):
            pieces.append(jnp.dot(x, w_ref[rows, p:q], preferred_element_type=F32))
            hook()
        return jnp.concatenate(pieces, axis=-1)

    y = None
    for lo, hi in FF_CHUNKS:
        a = matmul(h, wg_ref, slice(None), lo, hi)
        u = matmul(h, wu_ref, slice(None), lo, hi)
        act = (a * _sigmoid(a) * u).astype(BF16)
        part = matmul(act, wd_ref, slice(lo, hi), 0, D_MODEL)
        y = part if y is None else y + part
    return y


def _dense_ffn_kernel(x_ref, g_ref, wg_ref, wu_ref, wd_ref, o_ref):
    x = x_ref[...]
    ms = jnp.mean(x * x, axis=-1, keepdims=True)
    h = ((x * lax.rsqrt(ms + EPS)) * g_ref[...]).astype(BF16)
    o_ref[...] = x + _swiglu_tile(h, wg_ref, wu_ref, wd_ref)


def _dense_ffn(x, g, wg, wu, wd, *, tm):
    n, d = x.shape
    full = lambda a: pl.BlockSpec(a.shape, lambda i: (0, 0), pipeline_mode=pl.Buffered(1))
    return pl.pallas_call(
        _dense_ffn_kernel,
        grid=(n // tm,),
        in_specs=[pl.BlockSpec((tm, d), lambda i: (i, 0)), pl.BlockSpec((1, d), lambda i: (0, 0)),
                  full(wg), full(wu), full(wd)],
        out_specs=pl.BlockSpec((tm, d), lambda i: (i, 0)),
        out_shape=jax.ShapeDtypeStruct((n, d), F32),
        compiler_params=_cparams("arbitrary"),
        name="dense_swiglu",
    )(x, g.reshape(1, d), wg, wu, wd)


def _router_kernel(x_ref, g_ref, wr_ref, br_ref, h_ref, route_ref):
    x = x_ref[...]
    ms = jnp.mean(x * x, axis=-1, keepdims=True)
    h = (x * lax.rsqrt(ms + EPS)) * g_ref[...]
    _store_token_tiles(h_ref, h)
    h_hi = h.astype(BF16)
    h_lo = (h - h_hi.astype(F32)).astype(BF16)
    w = wr_ref[...]
    w_hi = w.astype(BF16)
    w_lo = (w - w_hi.astype(F32)).astype(BF16)
    logits = (jnp.dot(h_hi, w_hi, preferred_element_type=F32)
              + jnp.dot(h_lo, w_hi, preferred_element_type=F32)
              + jnp.dot(h_hi, w_lo, preferred_element_type=F32)) + br_ref[...]
    lane = lax.broadcasted_iota(jnp.int32, logits.shape, 1)
    m1 = jnp.max(logits, axis=-1, keepdims=True)
    i1 = jnp.min(jnp.where(logits == m1, lane, LANES), axis=-1, keepdims=True)
    rest = jnp.where(lane == i1, -jnp.inf, logits)
    m2 = jnp.max(rest, axis=-1, keepdims=True)
    i2 = jnp.min(jnp.where(rest == m2, lane, LANES), axis=-1, keepdims=True)
    e2 = jnp.exp(m2 - m1)
    g1 = 1.0 / (1.0 + e2)
    g2 = e2 / (1.0 + e2)
    route = jnp.where(lane == 0, i1.astype(F32),
                      jnp.where(lane == 1, i2.astype(F32),
                                jnp.where(lane == 2, g1, jnp.where(lane == 3, g2, 0.0))))
    route_ref[...] = route


def _router(x, g, w_router, b_router, *, tm):
    n, d = x.shape
    wr = jnp.zeros((d, LANES), F32).at[:, :N_EXPERTS].set(w_router)
    br = jnp.full((1, LANES), NEG_BIG, F32).at[0, :N_EXPERTS].set(b_router)
    return pl.pallas_call(
        _router_kernel,
        grid=(n // tm,),
        in_specs=[pl.BlockSpec((tm, d), lambda i: (i, 0)), pl.BlockSpec((1, d), lambda i: (0, 0)),
                  pl.BlockSpec((d, LANES), lambda i: (0, 0)), pl.BlockSpec((1, LANES), lambda i: (0, 0))],
        out_specs=[pl.BlockSpec((tm * ROW_TILE, LANES), lambda i: (i, 0)),
                   pl.BlockSpec((tm, LANES), lambda i: (i, 0))],
        out_shape=[jax.ShapeDtypeStruct((n * ROW_TILE, LANES), F32), jax.ShapeDtypeStruct((n, LANES), F32)],
        compiler_params=_cparams("arbitrary"),
        name="router_top2",
    )(x, g.reshape(1, d), wr, br)


ROW_TILE = D_MODEL // LANES


def _store_token_tiles(ref, x):
    rows = x.shape[0]
    for c in range(ROW_TILE):
        ref[pl.ds(c, rows, stride=ROW_TILE), :] = x[:, c * LANES:(c + 1) * LANES]


def _load_token_tiles(ref, rows):
    return jnp.concatenate([ref[pl.ds(c, rows, stride=ROW_TILE), :] for c in range(ROW_TILE)], axis=-1)


def _tile_rows(t):
    start = t * ROW_TILE
    return pl.ds(start if isinstance(start, int) else pl.multiple_of(start, ROW_TILE), ROW_TILE)


class _BlockRows:
    def __init__(self, dst_ref, n_tokens):
        self.dst_ref = dst_ref
        self.last_token = n_tokens - 1

    def gather_row(self, blk, r, h_hbm, xbuf, sem):
        tok = lax.shift_right_logical(self.dst_ref[blk, r], TOP_K.bit_length() - 1)
        tok = jnp.minimum(tok, self.last_token)
        pltpu.make_async_copy(h_hbm.at[_tile_rows(tok), :], xbuf.at[_tile_rows(r), :],
                              sem).start(priority=self._queue(r))

    def scatter_row(self, blk, r, ybuf, y_hbm, sem):
        pltpu.make_async_copy(ybuf.at[_tile_rows(r), :], y_hbm.at[_tile_rows(self.dst_ref[blk, r]), :],
                              sem).start(priority=self._queue(r))

    @staticmethod
    def _queue(r):
        return r % 2 if isinstance(r, int) else 0

    def all_rows(self, row_fn):
        def body(g, c):
            for j in range(SUBLANES):
                row_fn(g * SUBLANES + j)
            return c

        lax.fori_loop(0, MOE_TM // SUBLANES, body, 0)

    @staticmethod
    def wait_gather(h_hbm, xbuf, sem):
        pltpu.make_async_copy(h_hbm.at[pl.ds(0, MOE_TM * ROW_TILE), :], xbuf, sem).wait()

    @staticmethod
    def wait_scatter(ybuf, y_hbm, sem):
        pltpu.make_async_copy(ybuf, y_hbm.at[pl.ds(0, MOE_TM * ROW_TILE), :], sem).wait()


def _expert_kernel(be_ref, used_ref, dst_ref, h_hbm, wg_ref, wu_ref, wd_ref, y_hbm,
                   xbuf, ybuf, gsem, ssem, *, n_tokens):
    i = pl.program_id(0)
    used = used_ref[0]
    slot = i % 2
    rows = _BlockRows(dst_ref, n_tokens)
    gather_hooks = SWIGLU_HOOKS // 2
    share = lambda k, hooks: range(k * MOE_TM // hooks, (k + 1) * MOE_TM // hooks)

    def compute(scatter_previous):
        _BlockRows.wait_gather(h_hbm, xbuf.at[slot], gsem.at[slot])
        x = _load_token_tiles(xbuf.at[slot], MOE_TM).astype(BF16)
        nxt = jnp.minimum(i + 1, used - 1)
        calls = [0]

        def start_some():
            k = calls[0]
            calls[0] += 1
            if k < gather_hooks:
                for r in share(k, gather_hooks):
                    rows.gather_row(nxt, r, h_hbm, xbuf.at[1 - slot], gsem.at[1 - slot])
            if scatter_previous:
                for r in share(k, SWIGLU_HOOKS):
                    rows.scatter_row(i - 1, r, ybuf.at[1 - slot], y_hbm, ssem.at[1 - slot])

        y = _swiglu_tile(x, wg_ref, wu_ref, wd_ref, hook=start_some)
        assert calls[0] == SWIGLU_HOOKS

        @pl.when(i >= 2)
        def _():
            _BlockRows.wait_scatter(ybuf.at[slot], y_hbm, ssem.at[slot])

        _store_token_tiles(ybuf.at[slot], y)

    @pl.when(i == 0)
    def _():
        rows.all_rows(lambda r: rows.gather_row(0, r, h_hbm, xbuf.at[0], gsem.at[0]))
        ybuf[1] = jnp.zeros((MOE_TM * ROW_TILE, LANES), F32)
        for half in range(2):
            clear = pltpu.make_async_copy(
                ybuf.at[1],
                y_hbm.at[pl.ds((n_tokens * TOP_K + half * MOE_TM) * ROW_TILE, MOE_TM * ROW_TILE), :],
                ssem.at[half])
            clear.start()
            clear.wait()
        compute(False)

    @pl.when(jnp.logical_and(i >= 1, i < used))
    def _():
        compute(True)

    @pl.when(i == used)
    def _():
        _BlockRows.wait_gather(h_hbm, xbuf.at[slot], gsem.at[slot])
        _BlockRows.wait_scatter(ybuf.at[slot], y_hbm, ssem.at[slot])
        rows.all_rows(lambda r: rows.scatter_row(i - 1, r, ybuf.at[1 - slot], y_hbm, ssem.at[1 - slot]))
        _BlockRows.wait_scatter(ybuf.at[1 - slot], y_hbm, ssem.at[1 - slot])


def _experts(h, block_expert, used, row_dst, wg, wu, wd, n_tokens):
    d = D_MODEL
    n_blocks = row_dst.shape[0]
    f = wg.shape[-1]
    assert n_tokens * TOP_K >= 2 * MOE_TM
    one = pl.Buffered(1)
    grid_spec = pltpu.PrefetchScalarGridSpec(
        num_scalar_prefetch=3,
        grid=(n_blocks,),
        in_specs=[pl.BlockSpec(memory_space=pl.ANY),
                  pl.BlockSpec((None, d, f), lambda i, be, us, ds: (be[i], 0, 0), pipeline_mode=one),
                  pl.BlockSpec((None, d, f), lambda i, be, us, ds: (be[i], 0, 0), pipeline_mode=one),
                  pl.BlockSpec((None, f, d), lambda i, be, us, ds: (be[i], 0, 0), pipeline_mode=one)],
        out_specs=pl.BlockSpec(memory_space=pl.ANY),
        scratch_shapes=[pltpu.VMEM((2, MOE_TM * ROW_TILE, LANES), F32),
                        pltpu.VMEM((2, MOE_TM * ROW_TILE, LANES), F32),
                        pltpu.SemaphoreType.DMA((2,)), pltpu.SemaphoreType.DMA((2,))],
    )
    return pl.pallas_call(
        functools.partial(_expert_kernel, n_tokens=n_tokens),
        grid_spec=grid_spec,
        out_shape=jax.ShapeDtypeStruct(((n_tokens * TOP_K + 2 * MOE_TM) * ROW_TILE, LANES), F32),
        compiler_params=_cparams("arbitrary"),
        name="expert_swiglu",
    )(block_expert, used, row_dst, h, wg, wu, wd)


def _combine_kernel(y_ref, x_ref, route_ref, o_ref):
    tt = x_ref.shape[0]
    route = route_ref[...]
    out = x_ref[...]
    for k in range(TOP_K):
        yk = jnp.concatenate([y_ref[pl.ds(k * ROW_TILE + c, tt, stride=TOP_K * ROW_TILE), :]
                              for c in range(ROW_TILE)], axis=-1)
        out = out + route[:, TOP_K + k:TOP_K + k + 1] * yk
    o_ref[...] = out


def _combine(x, ys, route):
    n, d = x.shape
    tt = COMBINE_TT
    return pl.pallas_call(
        _combine_kernel,
        grid=(n // tt,),
        in_specs=[pl.BlockSpec((tt * TOP_K * ROW_TILE, LANES), lambda i: (i, 0)),
                  pl.BlockSpec((tt, d), lambda i: (i, 0)),
                  pl.BlockSpec((tt, LANES), lambda i: (i, 0))],
        out_specs=pl.BlockSpec((tt, d), lambda i: (i, 0)),
        out_shape=jax.ShapeDtypeStruct((n, d), F32),
        compiler_params=_cparams("arbitrary"),
        name="expert_combine",
    )(ys, x, route)


def _routed_ffn(x, g, w_router, b_router, wg, wu, wd, *, tm):
    n, d = x.shape
    h, route = _router(x, g, w_router, b_router, tm=tm)
    e_flat = route[:, :TOP_K].astype(jnp.int32).reshape(-1)
    onehot = (e_flat[:, None] == jnp.arange(N_EXPERTS)[None, :]).astype(jnp.int32)
    csum = jnp.cumsum(onehot, axis=0)
    counts = csum[-1]
    rank = jnp.sum((csum - onehot) * onehot, axis=1)
    padded = (counts + MOE_TM - 1) // MOE_TM * MOE_TM
    pend = jnp.cumsum(padded)
    pstart = pend - padded
    dest = (pstart[e_flat] + rank).astype(jnp.int32)
    n_blocks = (n * TOP_K) // MOE_TM + N_EXPERTS
    scratch_rows = n * TOP_K + jnp.arange(n_blocks * MOE_TM, dtype=jnp.int32) % (2 * MOE_TM)
    row_dst = scratch_rows.at[dest].set(jnp.arange(n * TOP_K, dtype=jnp.int32)).reshape(n_blocks, MOE_TM)
    block_expert = jnp.clip(jnp.searchsorted(pend, jnp.arange(n_blocks) * MOE_TM, side='right'),
                            0, N_EXPERTS - 1).astype(jnp.int32)
    used = (pend[-1:] // MOE_TM).astype(jnp.int32)
    ys = _experts(h, block_expert, used, row_dst, wg, wu, wd, n)
    return _combine(x, ys, route)


def kernel(x, attn_norm_g, w_in, q_norm_g, k_norm_g, conv_w, conv_b, conv_norm_g, conv_norm_b,
           w_branch_a, w_branch_b, w_branch_c, w_out, ffn_norm_g, w_ffn_gate, w_ffn_up,
           w_ffn_down, w_router, b_router, w_exp_gate, w_exp_up, w_exp_down):
    bsz, seq, d = x.shape
    depth = attn_norm_g.shape[0]
    n = bsz * seq
    tm = 512
    xf = x.reshape(n, d)
    s_dil, s_sb, s_glu = 3 * DIL_WIDTH, 3 * DIL_WIDTH + 3 * SB_WIDTH, 3 * DIL_WIDTH + 3 * SB_WIDTH + 2 * CONV_CH
    for layer in range(depth):
        wl = w_in[layer]
        w_perm = jnp.concatenate([wl[:, s_glu:], wl[:, s_sb:s_glu], wl[:, s_dil:s_sb], wl[:, :s_dil]],
                                 axis=1).astype(BF16)
        proj_a = _norm_matmul(xf, attn_norm_g[layer], w_perm[:, :PROJ_A_COLS], tm=tm)
        proj_b = _norm_matmul(xf, attn_norm_g[layer], w_perm[:, PROJ_A_COLS:], tm=tm)
        o_groups, lse_groups = [], []
        for gi, (window, dilation) in enumerate(DIL_GROUPS):
            o_g, lse_g = _dil_attention(proj_b, q_norm_g[layer], k_norm_g[layer], gi, window, dilation,
                                        bsz, seq)
            o_groups.append(o_g)
            lse_groups.append(lse_g)
        o_b = _sb_attention(proj_b, bsz, seq)
        o_c = _conformer_conv(proj_a, conv_w[layer], conv_b[layer], conv_norm_g[layer],
                              conv_norm_b[layer], bsz, seq)
        xf = _merge(o_groups, lse_groups, o_b, o_c, proj_a, xf,
                    w_branch_a[layer].astype(BF16), w_branch_b[layer].astype(BF16),
                    w_branch_c[layer].astype(BF16), w_out[layer].astype(BF16), tm=tm)
        i = layer // 2
        if layer % 2 == 0:
            xf = _dense_ffn(xf, ffn_norm_g[layer], w_ffn_gate[i].astype(BF16), w_ffn_up[i].astype(BF16),
                            w_ffn_down[i].astype(BF16), tm=tm)
        else:
            xf = _routed_ffn(xf, ffn_norm_g[layer], w_router[i], b_router[i],
                             w_exp_gate[i].astype(BF16), w_exp_up[i].astype(BF16),
                             w_exp_down[i].astype(BF16), tm=tm)
    return xf.reshape(bsz, seq, d)
```

```python
import functools
import math

import jax
import jax.numpy as jnp
from jax import lax
from jax.experimental import pallas as pl
from jax.experimental.pallas import tpu as pltpu

F32 = jnp.float32
BF16 = jnp.bfloat16

D_MODEL = 1024
HEAD_DIM = 64
DIL_GROUPS = ((128, 1), (512, 4), (2048, 16))
DIL_HEADS_PER_GROUP = 4
DIL_HEADS = len(DIL_GROUPS) * DIL_HEADS_PER_GROUP
DIL_WIDTH = DIL_HEADS * HEAD_DIM
DIL_OUT = DIL_HEADS_PER_GROUP * HEAD_DIM
DIL_BLOCK = 128
SB_HEADS = 8
SB_WIDTH = SB_HEADS * HEAD_DIM
CONV_CH = D_MODEL // 2
CONV_WIDTH = 31
N_BRANCH = 3
IN_COLS = 3 * DIL_WIDTH + 3 * SB_WIDTH + 2 * CONV_CH + N_BRANCH * D_MODEL
D_FF = 2816
N_EXPERTS = 8
TOP_K = 2
EPS = 1e-6
ALIBI_MAX_BIAS = 8.0
NEG_BIG = -1e30

COL_GATES = 0
COL_GLU = COL_GATES + N_BRANCH * D_MODEL
PROJ_A_COLS = COL_GLU + 2 * CONV_CH
COL_SB = 0
COL_DIL = COL_SB + 3 * SB_WIDTH

LANES = 128
SUBLANES = 8
VMEM_LIMIT = 56 * 1024 * 1024

FAST_STRIDE = 4
DIL_UNITS = 4
SB_TQ = 512
SB_TK = 128
SB_PAIRS = 2
SB_CHUNK = 2
SB_DEAD_RUN = 128.0
CONV_ROWS = 64
CONV_PAD = 32
FF_SPLIT = 1536
MOE_TM = 512
COMBINE_TT = 512


def _cparams(*sem):
    return pltpu.CompilerParams(dimension_semantics=sem, vmem_limit_bytes=VMEM_LIMIT)


def _sigmoid(x):
    return 1.0 / (1.0 + jnp.exp(-x))


def _norm_matmul_kernel(x_ref, g_ref, w_ref, o_ref):
    x = x_ref[...]
    ms = jnp.mean(x * x, axis=-1, keepdims=True)
    h = (x * lax.rsqrt(ms + EPS)) * g_ref[...]
    o_ref[...] = jnp.dot(h.astype(BF16), w_ref[...], preferred_element_type=F32).astype(o_ref.dtype)


def _norm_matmul_qk_kernel(x_ref, g_ref, w_ref, qk_gain_ref, o_ref, *, qk_lo):
    x = x_ref[...]
    ms = jnp.mean(x * x, axis=-1, keepdims=True)
    h = (x * lax.rsqrt(ms + EPS)) * g_ref[...]
    acc = jnp.dot(h.astype(BF16), w_ref[...], preferred_element_type=F32)
    qk_hi = qk_lo + 2 * DIL_WIDTH
    o_ref[:, :qk_lo] = acc[:, :qk_lo].astype(o_ref.dtype)
    o_ref[:, qk_hi:] = acc[:, qk_hi:].astype(o_ref.dtype)
    same_head = _same_head_matrix(DIL_OUT)
    for c in range(2 * DIL_WIDTH // DIL_OUT):
        t = acc[:, qk_lo + c * DIL_OUT:qk_lo + (c + 1) * DIL_OUT]
        gain = qk_gain_ref[:, c * DIL_OUT:(c + 1) * DIL_OUT]
        o_ref[:, qk_lo + c * DIL_OUT:qk_lo + (c + 1) * DIL_OUT] = (
            t * _head_rms_scale(t, same_head) * gain).astype(o_ref.dtype)


def _norm_matmul(x, g, w, *, tm, qk_gain=None, qk_lo=None):
    n, d = x.shape
    e = w.shape[1]
    in_specs = [pl.BlockSpec((tm, d), lambda i: (i, 0)),
                pl.BlockSpec((1, d), lambda i: (0, 0)),
                pl.BlockSpec((d, e), lambda i: (0, 0), pipeline_mode=pl.Buffered(1))]
    args = [x, g.reshape(1, d), w]
    body = _norm_matmul_kernel
    if qk_gain is not None:
        in_specs.append(pl.BlockSpec(qk_gain.shape, lambda i: (0, 0)))
        args.append(qk_gain)
        body = functools.partial(_norm_matmul_qk_kernel, qk_lo=qk_lo)
    return pl.pallas_call(
        body,
        grid=(n // tm,),
        in_specs=in_specs,
        out_specs=pl.BlockSpec((tm, e), lambda i: (i, 0)),
        out_shape=jax.ShapeDtypeStruct((n, e), BF16),
        compiler_params=_cparams("arbitrary"),
        name="norm_in_proj",
    )(*args)


def _same_head_matrix(w):
    r = lax.broadcasted_iota(jnp.int32, (w, w), 0) // HEAD_DIM
    c = lax.broadcasted_iota(jnp.int32, (w, w), 1) // HEAD_DIM
    return (r == c).astype(BF16)


def _head_rms_scale(t, same_head):
    ssq = jnp.dot((t * t).astype(BF16), same_head, preferred_element_type=F32)
    return lax.rsqrt(ssq * (1.0 / HEAD_DIM) + EPS)


def _dil_attn_kernel(q_ref, k_ref, v_ref, bias_ref, o_ref, lse_ref,
                     qn_scr, kn_scr, v_scr, o_scr, tmp_scr, *, seq, dilation):
    blk, d = DIL_BLOCK, dilation
    span = blk * d
    log2d = d.bit_length() - 1
    chunk = 256

    pairs = DIL_OUT // LANES
    pair_lanes = [slice(p * LANES, (p + 1) * LANES) for p in range(pairs)]
    d1 = d // FAST_STRIDE if d > FAST_STRIDE else 1
    d2 = d // d1
    kv_rows = span + seq

    def park(scr, slot, p, first_part_row, total_rows, x):
        if d1 == 1:
            scr[p, pl.ds(pl.multiple_of(first_part_row, blk), chunk), :] = x
            return
        tmp_scr[slot, p] = x
        for m in range(d1):
            dst = pl.ds(pl.multiple_of(m * (total_rows // d1) + first_part_row, SUBLANES), chunk // d1)
            scr[p, dst, :] = tmp_scr[slot, p, pl.ds(m, chunk // d1, stride=d1), :]

    for p in range(pairs):
        for m in range(d1):
            front = pl.ds(m * (kv_rows // d1), span // d1)
            kn_scr[p, front, :] = jnp.zeros((span // d1, LANES), F32)
            v_scr[p, front, :] = jnp.zeros((span // d1, LANES), F32)

    def widen(i, c):
        rows = pl.ds(pl.multiple_of(i * chunk, chunk), chunk)
        q = q_ref[0, rows, :].astype(F32)
        k = k_ref[0, rows, :].astype(F32)
        v = v_ref[0, rows, :].astype(F32)
        for p in range(pairs):
            park(qn_scr, 0, p, i * (chunk // d1), seq, q[:, pair_lanes[p]])
            park(kn_scr, 1, p, span // d1 + i * (chunk // d1), kv_rows, k[:, pair_lanes[p]])
            park(v_scr, 2, p, span // d1 + i * (chunk // d1), kv_rows, v[:, pair_lanes[p]])
        return c

    lax.fori_loop(0, seq // chunk, widen, 0)

    low_half = lax.broadcasted_iota(jnp.int32, (blk, LANES), 1) < HEAD_DIM

    def sub_rows(base, count):
        return pl.ds(pl.multiple_of(base, blk), count) if d == 1 else pl.ds(base, count, stride=d)

    def parked_rows(base, count, total_rows):
        if d1 == 1:
            return sub_rows(base, count)
        start = (base & (d1 - 1)) * (total_rows // d1) + lax.shift_right_logical(base, d1.bit_length() - 1)
        return pl.ds(start, count, stride=d2)

    def load(u):
        if d == 1:
            n, base = u, u * span
        else:
            n = u >> log2d
            base = (u & (d - 1)) + n * span
        q = [qn_scr[p, parked_rows(base, blk, seq), :] for p in range(pairs)]
        kk = [kn_scr[p, parked_rows(base, 2 * blk, kv_rows), :].astype(BF16) for p in range(pairs)]
        vv = [v_scr[p, parked_rows(base, 2 * blk, kv_rows), :].astype(BF16) for p in range(pairs)]
        first = jnp.where(n == 0, 1, 0)
        return base, q, kk, vv, first

    def scores(unit):
        _, q, kk, _, first = unit
        out = []
        for h in range(DIL_HEADS_PER_GROUP):
            keep = low_half if h % 2 == 0 else jnp.logical_not(low_half)
            qh = jnp.where(keep, q[h // 2], 0.0).astype(BF16)
            s = lax.dot_general(qh, kk[h // 2], (((1,), (1,)), ((), ())), preferred_element_type=F32)
            out.append(s + bias_ref[h, first])
        return out

    def finish(unit, s_list):
        base, _, _, vv, _ = unit
        rows = sub_rows(base, blk)
        for p in range(pairs):
            oh, lh = [], []
            for hh in range(2):
                s = s_list[2 * p + hh]
                m = jnp.max(s, axis=-1, keepdims=True)
                e = jnp.exp(s - m)
                den = jnp.sum(e, axis=-1, keepdims=True)
                oh.append(jnp.dot(e.astype(BF16), vv[p], preferred_element_type=F32) * (1.0 / den))
                lh.append(m + jnp.log(den))
            o_scr[p, rows, :] = jnp.where(low_half, oh[0], oh[1])
            lse_ref[0, p, rows, :] = jnp.where(low_half, lh[0], lh[1])

    def body(it, c):
        units = [load(it * DIL_UNITS + i) for i in range(DIL_UNITS)]
        s = [None] * DIL_UNITS
        for step in range(DIL_UNITS + 1):
            if step < DIL_UNITS:
                s[step] = scores(units[step])
            if step >= 1:
                finish(units[step - 1], s[step - 1])
        return c

    lax.fori_loop(0, seq // blk // DIL_UNITS, body, 0)

    def emit(i, c):
        rows = pl.ds(pl.multiple_of(i * chunk, chunk), chunk)
        o_ref[0, rows, :] = jnp.concatenate([o_scr[p, rows, :] for p in range(pairs)],
                                            axis=-1).astype(o_ref.dtype)
        return c

    lax.fori_loop(0, seq // chunk, emit, 0)


def _dil_bias_table(group, window, dilation):
    reach = window // dilation
    assert reach <= DIL_BLOCK
    slopes = 2.0 ** (-ALIBI_MAX_BIAS * jnp.arange(1, DIL_HEADS + 1, dtype=F32) / DIL_HEADS)
    slopes = slopes[group * DIL_HEADS_PER_GROUP:(group + 1) * DIL_HEADS_PER_GROUP]
    qi = jnp.arange(DIL_BLOCK)[:, None] + DIL_BLOCK
    ki = jnp.arange(2 * DIL_BLOCK)[None, :]
    dist = qi - ki
    valid = (dist >= 0) & (dist <= reach)
    bias = -slopes[:, None, None] * (dist * dilation).astype(F32)[None]
    general = jnp.where(valid[None], bias, NEG_BIG)
    first = jnp.where((valid & (ki >= DIL_BLOCK))[None], bias, NEG_BIG)
    return jnp.stack([general, first], axis=1)


def _dil_attention(proj, group, window, dilation, bsz, seq):
    e = proj.shape[-1]
    assert (seq // DIL_BLOCK) % DIL_UNITS == 0
    assert seq % (DIL_BLOCK * dilation) == 0 and dilation & (dilation - 1) == 0
    view = proj.reshape(bsz, seq, e)
    w = DIL_OUT
    pairs = w // LANES
    pad = DIL_BLOCK * dilation

    def col(base):
        off = (base + group * w) // w
        return lambda b: (b, 0, off)

    bias = _dil_bias_table(group, window, dilation)
    o, lse = pl.pallas_call(
        functools.partial(_dil_attn_kernel, seq=seq, dilation=dilation),
        grid=(bsz,),
        in_specs=[pl.BlockSpec((1, seq, w), col(COL_DIL)),
                  pl.BlockSpec((1, seq, w), col(COL_DIL + DIL_WIDTH)),
                  pl.BlockSpec((1, seq, w), col(COL_DIL + 2 * DIL_WIDTH)),
                  pl.BlockSpec(bias.shape, lambda b: (0, 0, 0, 0))],
        out_specs=[pl.BlockSpec((1, seq, w), lambda b: (b, 0, 0)),
                   pl.BlockSpec((1, pairs, seq, LANES), lambda b: (b, 0, 0, 0))],
        out_shape=[jax.ShapeDtypeStruct((bsz, seq, w), BF16),
                   jax.ShapeDtypeStruct((bsz, pairs, seq, LANES), F32)],
        scratch_shapes=[pltpu.VMEM((pairs, seq, LANES), F32), pltpu.VMEM((pairs, pad + seq, LANES), F32),
                        pltpu.VMEM((pairs, pad + seq, LANES), F32), pltpu.VMEM((pairs, seq, LANES), F32),
                        pltpu.VMEM((3, pairs, 256, LANES), F32)],
        compiler_params=_cparams("arbitrary"),
        name=f"dilated_attn_g{group}",
    )(view, view, view, bias)
    return o.reshape(bsz * seq, w), lse


def _sb_attn_kernel(q_ref, k_ref, v_ref, o_ref, vcat_scr, acc_scr, *, seq):
    tq, tk, nblk = SB_TQ, SB_TK, SB_TQ // SB_TK
    qi = pl.program_id(2)
    pair_lanes = [slice(p * LANES, (p + 1) * LANES) for p in range(SB_PAIRS)]

    def key_rows(kb):
        return pl.ds(pl.multiple_of(kb * tk, tk), tk)

    @pl.when(qi == 0)
    def _():
        chan = lax.broadcasted_iota(jnp.int32, (LANES, tk), 0)

        def build(kb, c):
            for p in range(SB_PAIRS):
                vt = v_ref[0, key_rows(kb), pair_lanes[p]].astype(F32).T
                vcat_scr[p, kb] = jnp.concatenate(
                    [jnp.where(chan < HEAD_DIM, vt, 0.0), jnp.where(chan >= HEAD_DIM, vt, 0.0)],
                    axis=1).astype(BF16)
            return c

        lax.fori_loop(0, seq // tk, build, 0, unroll=4)

    lane = lax.broadcasted_iota(jnp.int32, (tq, LANES), 1)
    qcat, qcat_t = [], []
    for p in range(SB_PAIRS):
        q = q_ref[0, :, pair_lanes[p]] * (1.0 / math.sqrt(HEAD_DIM))
        zero = jnp.zeros_like(q)
        both = jnp.concatenate([jnp.where(lane < HEAD_DIM, q, zero), jnp.where(lane >= HEAD_DIM, q, zero)],
                               axis=0)
        qcat.append(both)
        qcat_t.append(both.astype(F32).T.astype(BF16))
    neg_tri = jnp.where(lax.broadcasted_iota(jnp.int32, (tk, tk), 1)
                        >= lax.broadcasted_iota(jnp.int32, (tk, tk), 0), -1.0, 0.0).astype(BF16)
    acc_scr[...] = jnp.zeros_like(acc_scr)

    def first_query(rel):
        width = tq
        while rel is not None and tq - width // 2 <= rel * tk and width // 2 >= LANES:
            width //= 2
        return tq - width

    def window(x, lo, axis):
        if lo == 0:
            return x
        return jnp.concatenate([lax.slice_in_dim(x, lo, tq, axis=axis),
                                lax.slice_in_dim(x, tq + lo, 2 * tq, axis=axis)], axis=axis)

    def scores(p, kb, rel):
        lo = first_query(rel)
        wq = tq - lo
        kk = k_ref[0, key_rows(kb), pair_lanes[p]]
        zt = lax.dot_general(kk, window(qcat[p], lo, 0), (((1,), (1,)), ((), ())),
                             preferred_element_type=F32)
        neg_abs = lax.bitcast_convert_type(
            lax.bitcast_convert_type(zt, jnp.uint32) | jnp.uint32(0x80000000), F32)
        sp = jnp.maximum(zt, 0.0) + jnp.log(1.0 + jnp.exp(neg_abs))
        before = None
        if rel is not None:
            kpos = lax.broadcasted_iota(jnp.int32, (tk, 2 * wq), 0) + rel * tk
            qpos = (lax.broadcasted_iota(jnp.int32, (tk, 2 * wq), 1) & (wq - 1)) + lo
            before = kpos < qpos
            sp = jnp.where(before, sp, 0.0)
        return kk, zt[0:1, :], sp.astype(BF16), before, lo

    def weights(p, state, run):
        kk, zt0, sp, before, lo = state
        wq = tq - lo
        arg = jnp.dot(jnp.concatenate([neg_tri, kk], axis=1),
                      jnp.concatenate([sp, window(qcat_t[p], lo, 1)], axis=0),
                      preferred_element_type=F32)
        a = jnp.exp(arg - window(run, lo, 1))
        if before is not None:
            a = jnp.where(before, a, 0.0)
        a = a.astype(BF16)
        acat = jnp.concatenate([a[:, :wq], a[:, wq:]], axis=0)
        col_sums = zt0 - arg[0:1, :]
        if lo:
            none = jnp.zeros((1, lo), F32)
            col_sums = jnp.concatenate([none, col_sums[:, :wq], none, col_sums[:, wq:]], axis=1)
        return (acat, lo), run + col_sums

    def values(p, kb, acat_lo):
        acat, lo = acat_lo
        acc_scr[p, :, lo:] += jnp.dot(vcat_scr[p, kb], acat, preferred_element_type=F32)

    def run_blocks(blocks, runs):
        n = len(blocks)
        runs = list(runs)
        st = [[None] * n for _ in range(SB_PAIRS)]
        ac = [[None] * n for _ in range(SB_PAIRS)]
        for step in range(n + 2):
            for p in range(SB_PAIRS):
                if step < n:
                    st[p][step] = scores(p, *blocks[step])
            for p in range(SB_PAIRS):
                if 0 <= step - 1 < n:
                    ac[p][step - 1], runs[p] = weights(p, st[p][step - 1], runs[p])
            for p in range(SB_PAIRS):
                if 0 <= step - 2 < n:
                    values(p, blocks[step - 2][0], ac[p][step - 2])
        return tuple(runs)

    zero_runs = tuple(jnp.zeros((1, 2 * tq), F32) for _ in range(SB_PAIRS))
    diagonal = [(qi * nblk + rel, rel) for rel in reversed(range(nblk))]
    n_chunks = qi * (nblk // SB_CHUNK)

    def chunk_blocks(it):
        base = (n_chunks - 1 - it) * SB_CHUNK
        return [(base + j, None) for j in reversed(range(SB_CHUNK))]

    @pl.when(qi == 0)
    def _():
        run_blocks(diagonal, zero_runs)

    @pl.when(qi > 0)
    def _():
        runs = run_blocks(diagonal + chunk_blocks(0), zero_runs)

        def alive(runs):
            lowest = functools.reduce(jnp.minimum, [jnp.min(r) for r in runs])
            return (lowest < SB_DEAD_RUN).astype(jnp.int32)

        def more(carry):
            it, _, go = carry
            return jnp.logical_and(it < n_chunks, go > 0)

        def chunk(carry):
            it, runs, _ = carry
            runs = run_blocks(chunk_blocks(it), runs)
            return it + 1, runs, alive(runs)

        lax.while_loop(more, chunk, (jnp.int32(1), runs, alive(runs)))

    o_ref[0] = jnp.concatenate([acc_scr[p].T for p in range(SB_PAIRS)], axis=-1).astype(o_ref.dtype)


def _sb_attention(proj, bsz, seq):
    e = proj.shape[-1]
    view = proj.reshape(bsz, seq, e)
    w = SB_PAIRS * LANES
    qo, ko, vo = COL_SB // w, (COL_SB + SB_WIDTH) // w, (COL_SB + 2 * SB_WIDTH) // w
    out = pl.pallas_call(
        functools.partial(_sb_attn_kernel, seq=seq),
        grid=(bsz, SB_WIDTH // w, seq // SB_TQ),
        in_specs=[pl.BlockSpec((1, SB_TQ, w), lambda b, g, i: (b, i, qo + g)),
                  pl.BlockSpec((1, seq, w), lambda b, g, i: (b, 0, ko + g)),
                  pl.BlockSpec((1, seq, w), lambda b, g, i: (b, 0, vo + g))],
        out_specs=pl.BlockSpec((1, SB_TQ, w), lambda b, g, i: (b, i, g)),
        out_shape=jax.ShapeDtypeStruct((bsz, seq, SB_WIDTH), BF16),
        scratch_shapes=[pltpu.VMEM((SB_PAIRS, seq // SB_TK, LANES, 2 * SB_TK), BF16),
                        pltpu.VMEM((SB_PAIRS, LANES, SB_TQ), F32)],
        compiler_params=_cparams("arbitrary", "arbitrary", "arbitrary"),
        name="stick_breaking_attn",
    )(view, view, view)
    return out.reshape(bsz * seq, SB_WIDTH)


def _conv_kernel(val_ref, gate_ref, w_ref, b_ref, g_ref, beta_ref, o_ref, u_scr, y_scr, *, seq):
    tr, pad, slabs = CONV_ROWS, CONV_PAD, CONV_CH // LANES
    lanes = [slice(j * LANES, (j + 1) * LANES) for j in range(slabs)]
    for j in range(slabs):
        u_scr[j, pl.ds(0, pad), :] = jnp.zeros((pad, LANES), F32)

    def glu(i, c):
        rows = pl.ds(pl.multiple_of(i * 256, 256), 256)
        val = val_ref[0, rows, :].astype(F32)
        gate = gate_ref[0, rows, :].astype(F32)
        u = val * _sigmoid(gate)
        for j in range(slabs):
            u_scr[j, pl.ds(pl.multiple_of(pad + i * 256, SUBLANES), 256), :] = u[:, lanes[j]]
        return c

    lax.fori_loop(0, seq // 256, glu, 0)

    first = pad - (CONV_WIDTH - 1)

    def tile(i, c):
        t0 = pl.multiple_of(i * tr, tr)
        groups = tr // 2 // SUBLANES
        acc = [[jnp.zeros((groups, SUBLANES, LANES), F32) for _ in range(slabs)] for _ in range(2)]
        for j in range(slabs):
            taps = [jnp.broadcast_to(w_ref[w:w + 1, lanes[j]], (SUBLANES, LANES))[None]
                    for w in range(CONV_WIDTH)]
            for k in range(CONV_WIDTH + 1):
                x = u_scr[j, pl.ds(t0 + (first + k), tr // 2, stride=2), :].reshape(groups, SUBLANES, LANES)
                if k < CONV_WIDTH:
                    acc[0][j] = acc[0][j] + x * taps[k]
                if k >= 1:
                    acc[1][j] = acc[1][j] + x * taps[k - 1]
        for phase in range(2):
            y = jnp.concatenate([a.reshape(tr // 2, LANES) for a in acc[phase]], axis=-1) + b_ref[...]
            mu = jnp.mean(y, axis=-1, keepdims=True)
            yc = y - mu
            var = jnp.mean(yc * yc, axis=-1, keepdims=True)
            yn = yc * lax.rsqrt(var + EPS) * g_ref[...] + beta_ref[...]
            out = yn * _sigmoid(yn)
            for j in range(slabs):
                y_scr[j, pl.ds(phase, tr // 2, stride=2), :] = out[:, lanes[j]]
        o_ref[0, pl.ds(t0, tr), :] = jnp.concatenate([y_scr[j] for j in range(slabs)],
                                                     axis=-1).astype(o_ref.dtype)
        return c

    lax.fori_loop(0, seq // tr, tile, 0)


def _conformer_conv(proj, conv_w, conv_b, norm_g, norm_b, bsz, seq):
    e = proj.shape[-1]
    view = proj.reshape(bsz, seq, e)
    c = CONV_CH
    voff, goff = COL_GLU // c, (COL_GLU + c) // c
    const = lambda b: (0, 0)
    out = pl.pallas_call(
        functools.partial(_conv_kernel, seq=seq),
        grid=(bsz,),
        in_specs=[pl.BlockSpec((1, seq, c), lambda b: (b, 0, voff)),
                  pl.BlockSpec((1, seq, c), lambda b: (b, 0, goff)),
                  pl.BlockSpec((CONV_WIDTH, c), const),
                  pl.BlockSpec((1, c), const), pl.BlockSpec((1, c), const), pl.BlockSpec((1, c), const)],
        out_specs=pl.BlockSpec((1, seq, c), lambda b: (b, 0, 0)),
        out_shape=jax.ShapeDtypeStruct((bsz, seq, c), BF16),
        scratch_shapes=[pltpu.VMEM((c // LANES, seq + CONV_PAD, LANES), F32),
                        pltpu.VMEM((c // LANES, CONV_ROWS, LANES), F32)],
        compiler_params=_cparams("arbitrary"),
        name="conformer_conv",
    )(view, view, conv_w, conv_b.reshape(1, c), norm_g.reshape(1, c), norm_b.reshape(1, c))
    return out.reshape(bsz * seq, c)


def _merge_kernel(o1_ref, o2_ref, o3_ref, l1_ref, l2_ref, l3_ref, ob_ref, oc_ref,
                  ga_ref, gb_ref, gc_ref, x_ref, wa_ref, wb_ref, wc_ref, wo_ref, out_ref):
    by_lanes = lambda ref: jnp.concatenate([ref[p] for p in range(ref.shape[0])], axis=-1)
    l1, l2, l3 = by_lanes(l1_ref), by_lanes(l2_ref), by_lanes(l3_ref)
    m = jnp.maximum(jnp.maximum(l1, l2), l3)
    e1, e2, e3 = jnp.exp(l1 - m), jnp.exp(l2 - m), jnp.exp(l3 - m)
    o_a = (e1 * o1_ref[...].astype(F32) + e2 * o2_ref[...].astype(F32)
           + e3 * o3_ref[...].astype(F32)) / (e1 + e2 + e3)
    ya = jnp.dot(o_a.astype(BF16), wa_ref[...], preferred_element_type=F32)
    yb = jnp.dot(ob_ref[...], wb_ref[...], preferred_element_type=F32)
    yc = jnp.dot(oc_ref[...], wc_ref[...], preferred_element_type=F32)
    merged = (_sigmoid(ga_ref[...].astype(F32)) * ya + _sigmoid(gb_ref[...].astype(F32)) * yb
              + _sigmoid(gc_ref[...].astype(F32)) * yc)
    out_ref[...] = x_ref[...] + jnp.dot(merged.astype(BF16), wo_ref[...], preferred_element_type=F32)


def _merge(o_groups, lse_groups, o_b, o_c, proj, x, wa, wb, wc, wo, *, tm):
    n, d = x.shape
    row = lambda w, j=0: pl.BlockSpec((tm, w), lambda i: (i, j))
    full = lambda a: pl.BlockSpec(a.shape, lambda i: (0, 0), pipeline_mode=pl.Buffered(1))
    g0 = COL_GATES // d
    _, pairs, seq, _ = lse_groups[0].shape
    per_batch = seq // tm
    lse = pl.BlockSpec((None, pairs, tm, LANES), lambda i: (i // per_batch, 0, i % per_batch, 0))
    return pl.pallas_call(
        _merge_kernel,
        grid=(n // tm,),
        in_specs=[row(DIL_OUT), row(DIL_OUT), row(DIL_OUT), lse, lse, lse,
                  row(SB_WIDTH), row(CONV_CH), row(d, g0), row(d, g0 + 1), row(d, g0 + 2), row(d),
                  full(wa), full(wb), full(wc), full(wo)],
        out_specs=row(d),
        out_shape=jax.ShapeDtypeStruct((n, d), F32),
        compiler_params=_cparams("arbitrary"),
        name="branch_merge_out_proj",
    )(*o_groups, *lse_groups, o_b, o_c, proj, proj, proj, x, wa, wb, wc, wo)


MXU_COLS = 256


def _col_pieces(lo, hi):
    return [(p, min(p + MXU_COLS, hi)) for p in range(lo, hi, MXU_COLS)]


FF_CHUNKS = ((0, FF_SPLIT), (FF_SPLIT, D_FF))
SWIGLU_HOOKS = sum(2 * len(_col_pieces(lo, hi)) + len(_col_pieces(0, D_MODEL)) for lo, hi in FF_CHUNKS)


def _swiglu_tile(h, wg_ref, wu_ref, wd_ref, hook=None):
    def matmul(x, w_ref, rows, lo, hi):
        if hook is None:
            return jnp.dot(x, w_ref[rows, lo:hi], preferred_element_type=F32)
        pieces = []
        for p, q in _col_pieces(lo, hi):
            pieces.append(jnp.dot(x, w_ref[rows, p:q], preferred_element_type=F32))
            hook()
        return jnp.concatenate(pieces, axis=-1)

    y = None
    for lo, hi in FF_CHUNKS:
        a = matmul(h, wg_ref, slice(None), lo, hi)
        u = matmul(h, wu_ref, slice(None), lo, hi)
        act = (a * _sigmoid(a) * u).astype(BF16)
        part = matmul(act, wd_ref, slice(lo, hi), 0, D_MODEL)
        y = part if y is None else y + part
    return y


def _dense_ffn_kernel(x_ref, g_ref, wg_ref, wu_ref, wd_ref, o_ref):
    x = x_ref[...]
    ms = jnp.mean(x * x, axis=-1, keepdims=True)
    h = ((x * lax.rsqrt(ms + EPS)) * g_ref[...]).astype(BF16)
    o_ref[...] = x + _swiglu_tile(h, wg_ref, wu_ref, wd_ref)


def _dense_ffn(x, g, wg, wu, wd, *, tm):
    n, d = x.shape
    full = lambda a: pl.BlockSpec(a.shape, lambda i: (0, 0), pipeline_mode=pl.Buffered(1))
    return pl.pallas_call(
        _dense_ffn_kernel,
        grid=(n // tm,),
        in_specs=[pl.BlockSpec((tm, d), lambda i: (i, 0)), pl.BlockSpec((1, d), lambda i: (0, 0)),
                  full(wg), full(wu), full(wd)],
        out_specs=pl.BlockSpec((tm, d), lambda i: (i, 0)),
        out_shape=jax.ShapeDtypeStruct((n, d), F32),
        compiler_params=_cparams("arbitrary"),
        name="dense_swiglu",
    )(x, g.reshape(1, d), wg, wu, wd)


def _router_kernel(x_ref, g_ref, wr_ref, br_ref, h_ref, route_ref):
    x = x_ref[...]
    ms = jnp.mean(x * x, axis=-1, keepdims=True)
    h = (x * lax.rsqrt(ms + EPS)) * g_ref[...]
    _store_token_tiles(h_ref, h)
    h_hi = h.astype(BF16)
    h_lo = (h - h_hi.astype(F32)).astype(BF16)
    w = wr_ref[...]
    w_hi = w.astype(BF16)
    w_lo = (w - w_hi.astype(F32)).astype(BF16)
    logits = (jnp.dot(h_hi, w_hi, preferred_element_type=F32)
              + jnp.dot(h_lo, w_hi, preferred_element_type=F32)
              + jnp.dot(h_hi, w_lo, preferred_element_type=F32)) + br_ref[...]
    lane = lax.broadcasted_iota(jnp.int32, logits.shape, 1)
    m1 = jnp.max(logits, axis=-1, keepdims=True)
    i1 = jnp.min(jnp.where(logits == m1, lane, LANES), axis=-1, keepdims=True)
    rest = jnp.where(lane == i1, -jnp.inf, logits)
    m2 = jnp.max(rest, axis=-1, keepdims=True)
    i2 = jnp.min(jnp.where(rest == m2, lane, LANES), axis=-1, keepdims=True)
    e2 = jnp.exp(m2 - m1)
    g1 = 1.0 / (1.0 + e2)
    g2 = e2 / (1.0 + e2)
    route = jnp.where(lane == 0, i1.astype(F32),
                      jnp.where(lane == 1, i2.astype(F32),
                                jnp.where(lane == 2, g1, jnp.where(lane == 3, g2, 0.0))))
    route_ref[...] = route


def _router(x, g, w_router, b_router, *, tm):
    n, d = x.shape
    wr = jnp.zeros((d, LANES), F32).at[:, :N_EXPERTS].set(w_router)
    br = jnp.full((1, LANES), NEG_BIG, F32).at[0, :N_EXPERTS].set(b_router)
    return pl.pallas_call(
        _router_kernel,
        grid=(n // tm,),
        in_specs=[pl.BlockSpec((tm, d), lambda i: (i, 0)), pl.BlockSpec((1, d), lambda i: (0, 0)),
                  pl.BlockSpec((d, LANES), lambda i: (0, 0)), pl.BlockSpec((1, LANES), lambda i: (0, 0))],
        out_specs=[pl.BlockSpec((tm * ROW_TILE, LANES), lambda i: (i, 0)),
                   pl.BlockSpec((tm, LANES), lambda i: (i, 0))],
        out_shape=[jax.ShapeDtypeStruct((n * ROW_TILE, LANES), F32), jax.ShapeDtypeStruct((n, LANES), F32)],
        compiler_params=_cparams("arbitrary"),
        name="router_top2",
    )(x, g.reshape(1, d), wr, br)


ROW_TILE = D_MODEL // LANES


def _store_token_tiles(ref, x):
    rows = x.shape[0]
    for c in range(ROW_TILE):
        ref[pl.ds(c, rows, stride=ROW_TILE), :] = x[:, c * LANES:(c + 1) * LANES]


def _load_token_tiles(ref, rows):
    return jnp.concatenate([ref[pl.ds(c, rows, stride=ROW_TILE), :] for c in range(ROW_TILE)], axis=-1)


def _tile_rows(t):
    start = t * ROW_TILE
    return pl.ds(start if isinstance(start, int) else pl.multiple_of(start, ROW_TILE), ROW_TILE)


class _BlockRows:
    def __init__(self, dst_ref, n_tokens):
        self.dst_ref = dst_ref
        self.last_token = n_tokens - 1

    def gather_row(self, blk, r, h_hbm, xbuf, sem):
        tok = lax.shift_right_logical(self.dst_ref[blk, r], TOP_K.bit_length() - 1)
        tok = jnp.minimum(tok, self.last_token)
        pltpu.make_async_copy(h_hbm.at[_tile_rows(tok), :], xbuf.at[_tile_rows(r), :],
                              sem).start(priority=self._queue(r))

    def scatter_row(self, blk, r, ybuf, y_hbm, sem):
        pltpu.make_async_copy(ybuf.at[_tile_rows(r), :], y_hbm.at[_tile_rows(self.dst_ref[blk, r]), :],
                              sem).start(priority=self._queue(r))

    @staticmethod
    def _queue(r):
        return r % 2 if isinstance(r, int) else 0

    def all_rows(self, row_fn):
        def body(g, c):
            for j in range(SUBLANES):
                row_fn(g * SUBLANES + j)
            return c

        lax.fori_loop(0, MOE_TM // SUBLANES, body, 0)

    @staticmethod
    def wait_gather(h_hbm, xbuf, sem):
        pltpu.make_async_copy(h_hbm.at[pl.ds(0, MOE_TM * ROW_TILE), :], xbuf, sem).wait()

    @staticmethod
    def wait_scatter(ybuf, y_hbm, sem):
        pltpu.make_async_copy(ybuf, y_hbm.at[pl.ds(0, MOE_TM * ROW_TILE), :], sem).wait()


def _expert_kernel(be_ref, used_ref, dst_ref, h_hbm, wg_ref, wu_ref, wd_ref, y_hbm,
                   xbuf, ybuf, gsem, ssem, *, n_tokens):
    i = pl.program_id(0)
    used = used_ref[0]
    slot = i % 2
    rows = _BlockRows(dst_ref, n_tokens)
    gather_hooks = SWIGLU_HOOKS // 2
    share = lambda k, hooks: range(k * MOE_TM // hooks, (k + 1) * MOE_TM // hooks)

    def compute(scatter_previous):
        _BlockRows.wait_gather(h_hbm, xbuf.at[slot], gsem.at[slot])
        x = _load_token_tiles(xbuf.at[slot], MOE_TM).astype(BF16)
        nxt = jnp.minimum(i + 1, used - 1)
        calls = [0]

        def start_some():
            k = calls[0]
            calls[0] += 1
            if k < gather_hooks:
                for r in share(k, gather_hooks):
                    rows.gather_row(nxt, r, h_hbm, xbuf.at[1 - slot], gsem.at[1 - slot])
            if scatter_previous:
                for r in share(k, SWIGLU_HOOKS):
                    rows.scatter_row(i - 1, r, ybuf.at[1 - slot], y_hbm, ssem.at[1 - slot])

        y = _swiglu_tile(x, wg_ref, wu_ref, wd_ref, hook=start_some)
        assert calls[0] == SWIGLU_HOOKS

        @pl.when(i >= 2)
        def _():
            _BlockRows.wait_scatter(ybuf.at[slot], y_hbm, ssem.at[slot])

        _store_token_tiles(ybuf.at[slot], y)

    @pl.when(i == 0)
    def _():
        rows.all_rows(lambda r: rows.gather_row(0, r, h_hbm, xbuf.at[0], gsem.at[0]))
        ybuf[1] = jnp.zeros((MOE_TM * ROW_TILE, LANES), F32)
        for half in range(2):
            clear = pltpu.make_async_copy(
                ybuf.at[1],
                y_hbm.at[pl.ds((n_tokens * TOP_K + half * MOE_TM) * ROW_TILE, MOE_TM * ROW_TILE), :],
                ssem.at[half])
            clear.start()
            clear.wait()
        compute(False)

    @pl.when(jnp.logical_and(i >= 1, i < used))
    def _():
        compute(True)

    @pl.when(i == used)
    def _():
        _BlockRows.wait_gather(h_hbm, xbuf.at[slot], gsem.at[slot])
        _BlockRows.wait_scatter(ybuf.at[slot], y_hbm, ssem.at[slot])
        rows.all_rows(lambda r: rows.scatter_row(i - 1, r, ybuf.at[1 - slot], y_hbm, ssem.at[1 - slot]))
        _BlockRows.wait_scatter(ybuf.at[1 - slot], y_hbm, ssem.at[1 - slot])


def _experts(h, block_expert, used, row_dst, wg, wu, wd, n_tokens):
    d = D_MODEL
    n_blocks = row_dst.shape[0]
    f = wg.shape[-1]
    assert n_tokens * TOP_K >= 2 * MOE_TM
    one = pl.Buffered(1)
    grid_spec = pltpu.PrefetchScalarGridSpec(
        num_scalar_prefetch=3,
        grid=(n_blocks,),
        in_specs=[pl.BlockSpec(memory_space=pl.ANY),
                  pl.BlockSpec((None, d, f), lambda i, be, us, ds: (be[i], 0, 0), pipeline_mode=one),
                  pl.BlockSpec((None, d, f), lambda i, be, us, ds: (be[i], 0, 0), pipeline_mode=one),
                  pl.BlockSpec((None, f, d), lambda i, be, us, ds: (be[i], 0, 0), pipeline_mode=one)],
        out_specs=pl.BlockSpec(memory_space=pl.ANY),
        scratch_shapes=[pltpu.VMEM((2, MOE_TM * ROW_TILE, LANES), F32),
                        pltpu.VMEM((2, MOE_TM * ROW_TILE, LANES), F32),
                        pltpu.SemaphoreType.DMA((2,)), pltpu.SemaphoreType.DMA((2,))],
    )
    return pl.pallas_call(
        functools.partial(_expert_kernel, n_tokens=n_tokens),
        grid_spec=grid_spec,
        out_shape=jax.ShapeDtypeStruct(((n_tokens * TOP_K + 2 * MOE_TM) * ROW_TILE, LANES), F32),
        compiler_params=_cparams("arbitrary"),
        name="expert_swiglu",
    )(block_expert, used, row_dst, h, wg, wu, wd)


def _combine_kernel(y_ref, x_ref, route_ref, o_ref):
    tt = x_ref.shape[0]
    route = route_ref[...]
    out = x_ref[...]
    for k in range(TOP_K):
        yk = jnp.concatenate([y_ref[pl.ds(k * ROW_TILE + c, tt, stride=TOP_K * ROW_TILE), :]
                              for c in range(ROW_TILE)], axis=-1)
        out = out + route[:, TOP_K + k:TOP_K + k + 1] * yk
    o_ref[...] = out


def _combine(x, ys, route):
    n, d = x.shape
    tt = COMBINE_TT
    return pl.pallas_call(
        _combine_kernel,
        grid=(n // tt,),
        in_specs=[pl.BlockSpec((tt * TOP_K * ROW_TILE, LANES), lambda i: (i, 0)),
                  pl.BlockSpec((tt, d), lambda i: (i, 0)),
                  pl.BlockSpec((tt, LANES), lambda i: (i, 0))],
        out_specs=pl.BlockSpec((tt, d), lambda i: (i, 0)),
        out_shape=jax.ShapeDtypeStruct((n, d), F32),
        compiler_params=_cparams("arbitrary"),
        name="expert_combine",
    )(ys, x, route)


def _routed_ffn(x, g, w_router, b_router, wg, wu, wd, *, tm):
    n, d = x.shape
    h, route = _router(x, g, w_router, b_router, tm=tm)
    e_flat = route[:, :TOP_K].astype(jnp.int32).reshape(-1)
    onehot = (e_flat[:, None] == jnp.arange(N_EXPERTS)[None, :]).astype(jnp.int32)
    csum = jnp.cumsum(onehot, axis=0)
    counts = csum[-1]
    rank = jnp.sum((csum - onehot) * onehot, axis=1)
    padded = (counts + MOE_TM - 1) // MOE_TM * MOE_TM
    pend = jnp.cumsum(padded)
    pstart = pend - padded
    dest = (pstart[e_flat] + rank).astype(jnp.int32)
    n_blocks = (n * TOP_K) // MOE_TM + N_EXPERTS
    scratch_rows = n * TOP_K + jnp.arange(n_blocks * MOE_TM, dtype=jnp.int32) % (2 * MOE_TM)
    row_dst = scratch_rows.at[dest].set(jnp.arange(n * TOP_K, dtype=jnp.int32), unique_indices=True,
                                        mode='promise_in_bounds').reshape(n_blocks, MOE_TM)
    block_expert = jnp.clip(jnp.searchsorted(pend, jnp.arange(n_blocks) * MOE_TM, side='right'),
                            0, N_EXPERTS - 1).astype(jnp.int32)
    used = (pend[-1:] // MOE_TM).astype(jnp.int32)
    ys = _experts(h, block_expert, used, row_dst, wg, wu, wd, n)
    return _combine(x, ys, route)


def kernel(x, attn_norm_g, w_in, q_norm_g, k_norm_g, conv_w, conv_b, conv_norm_g, conv_norm_b,
           w_branch_a, w_branch_b, w_branch_c, w_out, ffn_norm_g, w_ffn_gate, w_ffn_up,
           w_ffn_down, w_router, b_router, w_exp_gate, w_exp_up, w_exp_down):
    bsz, seq, d = x.shape
    depth = attn_norm_g.shape[0]
    n = bsz * seq
    tm = 512
    xf = x.reshape(n, d)
    s_dil, s_sb, s_glu = 3 * DIL_WIDTH, 3 * DIL_WIDTH + 3 * SB_WIDTH, 3 * DIL_WIDTH + 3 * SB_WIDTH + 2 * CONV_CH
    for layer in range(depth):
        wl = w_in[layer]
        w_perm = jnp.concatenate([wl[:, s_glu:], wl[:, s_sb:s_glu], wl[:, s_dil:s_sb], wl[:, :s_dil]],
                                 axis=1).astype(BF16)
        proj_a = _norm_matmul(xf, attn_norm_g[layer], w_perm[:, :PROJ_A_COLS], tm=tm)
        qk_gain = jnp.concatenate([jnp.tile(q_norm_g[layer], DIL_HEADS) * (1.0 / math.sqrt(HEAD_DIM)),
                                   jnp.tile(k_norm_g[layer], DIL_HEADS)]).reshape(1, 2 * DIL_WIDTH)
        proj_b = _norm_matmul(xf, attn_norm_g[layer], w_perm[:, PROJ_A_COLS:], tm=tm,
                              qk_gain=qk_gain, qk_lo=COL_DIL)
        o_groups, lse_groups = [], []
        for gi, (window, dilation) in enumerate(DIL_GROUPS):
            o_g, lse_g = _dil_attention(proj_b, gi, window, dilation, bsz, seq)
            o_groups.append(o_g)
            lse_groups.append(lse_g)
        o_b = _sb_attention(proj_b, bsz, seq)
        o_c = _conformer_conv(proj_a, conv_w[layer], conv_b[layer], conv_norm_g[layer],
                              conv_norm_b[layer], bsz, seq)
        xf = _merge(o_groups, lse_groups, o_b, o_c, proj_a, xf,
                    w_branch_a[layer].astype(BF16), w_branch_b[layer].astype(BF16),
                    w_branch_c[layer].astype(BF16), w_out[layer].astype(BF16), tm=tm)
        i = layer // 2
        if layer % 2 == 0:
            xf = _dense_ffn(xf, ffn_norm_g[layer], w_ffn_gate[i].astype(BF16), w_ffn_up[i].astype(BF16),
                            w_ffn_down[i].astype(BF16), tm=tm)
        else:
            xf = _routed_ffn(xf, ffn_norm_g[layer], w_router[i], b_router[i],
                             w_exp_gate[i].astype(BF16), w_exp_up[i].astype(BF16),
                             w_exp_down[i].astype(BF16), tm=tm)
    return xf.reshape(bsz, seq, d)
```

```python
import functools
import math

import jax
import jax.numpy as jnp
from jax import lax
from jax.experimental import pallas as pl
from jax.experimental.pallas import tpu as pltpu

F32 = jnp.float32
BF16 = jnp.bfloat16

D_MODEL = 1024
HEAD_DIM = 64
DIL_GROUPS = ((128, 1), (512, 4), (2048, 16))
DIL_HEADS_PER_GROUP = 4
DIL_HEADS = len(DIL_GROUPS) * DIL_HEADS_PER_GROUP
DIL_WIDTH = DIL_HEADS * HEAD_DIM
DIL_OUT = DIL_HEADS_PER_GROUP * HEAD_DIM
DIL_BLOCK = 128
SB_HEADS = 8
SB_WIDTH = SB_HEADS * HEAD_DIM
CONV_CH = D_MODEL // 2
CONV_WIDTH = 31
N_BRANCH = 3
D_FF = 2816
N_EXPERTS = 8
TOP_K = 2
EPS = 1e-6
ALIBI_MAX_BIAS = 8.0
NEG_BIG = -1e30

COL_GATES = 0
COL_GLU = COL_GATES + N_BRANCH * D_MODEL
PROJ_A_COLS = COL_GLU + 2 * CONV_CH
COL_SB = 0
COL_DIL = COL_SB + 3 * SB_WIDTH

LANES = 128
SUBLANES = 8
VMEM_LIMIT = 56 * 1024 * 1024

FAST_STRIDE = 4
DIL_UNITS = 4
SB_TQ = 512
SB_TK = 128
SB_PAIRS = 2
SB_CHUNK = 2
SB_DEAD_RUN = 128.0
CONV_ROWS = 64
CONV_PAD = 32
FF_SPLIT = 1536
MOE_TM = 512
COMBINE_TT = 512


def _cparams(*sem):
    return pltpu.CompilerParams(dimension_semantics=sem, vmem_limit_bytes=VMEM_LIMIT)


def _sigmoid(x):
    return 1.0 / (1.0 + jnp.exp(-x))


def _norm_matmul_kernel(x_ref, g_ref, w_ref, o_ref):
    x = x_ref[...]
    ms = jnp.mean(x * x, axis=-1, keepdims=True)
    h = (x * lax.rsqrt(ms + EPS)) * g_ref[...]
    o_ref[...] = jnp.dot(h.astype(BF16), w_ref[...], preferred_element_type=F32).astype(o_ref.dtype)


def _norm_matmul_qk_kernel(x_ref, g_ref, w_ref, qk_gain_ref, o_ref, *, qk_lo):
    x = x_ref[...]
    ms = jnp.mean(x * x, axis=-1, keepdims=True)
    h = (x * lax.rsqrt(ms + EPS)) * g_ref[...]
    acc = jnp.dot(h.astype(BF16), w_ref[...], preferred_element_type=F32)
    qk_hi = qk_lo + 2 * DIL_WIDTH
    o_ref[:, :qk_lo] = acc[:, :qk_lo].astype(o_ref.dtype)
    o_ref[:, qk_hi:] = acc[:, qk_hi:].astype(o_ref.dtype)
    same_head = _same_head_matrix(DIL_OUT)
    for c in range(2 * DIL_WIDTH // DIL_OUT):
        t = acc[:, qk_lo + c * DIL_OUT:qk_lo + (c + 1) * DIL_OUT]
        gain = qk_gain_ref[:, c * DIL_OUT:(c + 1) * DIL_OUT]
        o_ref[:, qk_lo + c * DIL_OUT:qk_lo + (c + 1) * DIL_OUT] = (
            t * _head_rms_scale(t, same_head) * gain).astype(o_ref.dtype)


def _norm_matmul(x, g, w, *, tm, qk_gain=None, qk_lo=None):
    n, d = x.shape
    e = w.shape[1]
    in_specs = [pl.BlockSpec((tm, d), lambda i: (i, 0)),
                pl.BlockSpec((1, d), lambda i: (0, 0)),
                pl.BlockSpec((d, e), lambda i: (0, 0), pipeline_mode=pl.Buffered(1))]
    args = [x, g.reshape(1, d), w]
    body = _norm_matmul_kernel
    if qk_gain is not None:
        in_specs.append(pl.BlockSpec(qk_gain.shape, lambda i: (0, 0)))
        args.append(qk_gain)
        body = functools.partial(_norm_matmul_qk_kernel, qk_lo=qk_lo)
    return pl.pallas_call(
        body,
        grid=(n // tm,),
        in_specs=in_specs,
        out_specs=pl.BlockSpec((tm, e), lambda i: (i, 0)),
        out_shape=jax.ShapeDtypeStruct((n, e), BF16),
        compiler_params=_cparams("arbitrary"),
        name="norm_in_proj",
    )(*args)


def _same_head_matrix(w):
    r = lax.broadcasted_iota(jnp.int32, (w, w), 0) // HEAD_DIM
    c = lax.broadcasted_iota(jnp.int32, (w, w), 1) // HEAD_DIM
    return (r == c).astype(BF16)


def _head_rms_scale(t, same_head):
    ssq = jnp.dot((t * t).astype(BF16), same_head, preferred_element_type=F32)
    return lax.rsqrt(ssq * (1.0 / HEAD_DIM) + EPS)


def _dil_attn_kernel(q_ref, k_ref, v_ref, bias_ref, o_ref, lse_ref,
                     qn_scr, kn_scr, v_scr, o_scr, tmp_scr, *, seq, dilation):
    blk, d = DIL_BLOCK, dilation
    span = blk * d
    log2d = d.bit_length() - 1
    chunk = 256

    pairs = DIL_OUT // LANES
    pair_lanes = [slice(p * LANES, (p + 1) * LANES) for p in range(pairs)]
    d1 = d // FAST_STRIDE if d > FAST_STRIDE else 1
    d2 = d // d1
    kv_rows = span + seq

    def park(scr, slot, p, first_part_row, total_rows, x):
        if d1 == 1:
            scr[p, pl.ds(pl.multiple_of(first_part_row, blk), chunk), :] = x
            return
        tmp_scr[slot, p] = x
        for m in range(d1):
            dst = pl.ds(pl.multiple_of(m * (total_rows // d1) + first_part_row, SUBLANES), chunk // d1)
            scr[p, dst, :] = tmp_scr[slot, p, pl.ds(m, chunk // d1, stride=d1), :]

    for p in range(pairs):
        for m in range(d1):
            front = pl.ds(m * (kv_rows // d1), span // d1)
            kn_scr[p, front, :] = jnp.zeros((span // d1, LANES), F32)
            v_scr[p, front, :] = jnp.zeros((span // d1, LANES), F32)

    def widen(i, c):
        rows = pl.ds(pl.multiple_of(i * chunk, chunk), chunk)
        q = q_ref[0, rows, :].astype(F32)
        k = k_ref[0, rows, :].astype(F32)
        v = v_ref[0, rows, :].astype(F32)
        for p in range(pairs):
            park(qn_scr, 0, p, i * (chunk // d1), seq, q[:, pair_lanes[p]])
            park(kn_scr, 1, p, span // d1 + i * (chunk // d1), kv_rows, k[:, pair_lanes[p]])
            park(v_scr, 2, p, span // d1 + i * (chunk // d1), kv_rows, v[:, pair_lanes[p]])
        return c

    lax.fori_loop(0, seq // chunk, widen, 0)

    low_half = lax.broadcasted_iota(jnp.int32, (blk, LANES), 1) < HEAD_DIM

    def sub_rows(base, count):
        return pl.ds(pl.multiple_of(base, blk), count) if d == 1 else pl.ds(base, count, stride=d)

    def parked_rows(base, count, total_rows):
        if d1 == 1:
            return sub_rows(base, count)
        start = (base & (d1 - 1)) * (total_rows // d1) + lax.shift_right_logical(base, d1.bit_length() - 1)
        return pl.ds(start, count, stride=d2)

    def load(u):
        if d == 1:
            n, base = u, u * span
        else:
            n = u >> log2d
            base = (u & (d - 1)) + n * span
        q = [qn_scr[p, parked_rows(base, blk, seq), :] for p in range(pairs)]
        kk = [kn_scr[p, parked_rows(base, 2 * blk, kv_rows), :].astype(BF16) for p in range(pairs)]
        vv = [v_scr[p, parked_rows(base, 2 * blk, kv_rows), :].astype(BF16) for p in range(pairs)]
        first = jnp.where(n == 0, 1, 0)
        return base, q, kk, vv, first

    def scores(unit):
        _, q, kk, _, first = unit
        out = []
        for h in range(DIL_HEADS_PER_GROUP):
            keep = low_half if h % 2 == 0 else jnp.logical_not(low_half)
            qh = jnp.where(keep, q[h // 2], 0.0).astype(BF16)
            s = lax.dot_general(qh, kk[h // 2], (((1,), (1,)), ((), ())), preferred_element_type=F32)
            out.append(s + bias_ref[h, first])
        return out

    def finish(unit, s_list):
        base, _, _, vv, _ = unit
        rows = sub_rows(base, blk)
        for p in range(pairs):
            oh, lh = [], []
            for hh in range(2):
                s = s_list[2 * p + hh]
                m = jnp.max(s, axis=-1, keepdims=True)
                e = jnp.exp(s - m)
                den = jnp.sum(e, axis=-1, keepdims=True)
                oh.append(jnp.dot(e.astype(BF16), vv[p], preferred_element_type=F32) * (1.0 / den))
                lh.append(m + jnp.log(den))
            o_scr[p, rows, :] = jnp.where(low_half, oh[0], oh[1])
            lse_ref[0, p, rows, :] = jnp.where(low_half, lh[0], lh[1])

    def body(it, c):
        units = [load(it * DIL_UNITS + i) for i in range(DIL_UNITS)]
        s = [None] * DIL_UNITS
        for step in range(DIL_UNITS + 1):
            if step < DIL_UNITS:
                s[step] = scores(units[step])
            if step >= 1:
                finish(units[step - 1], s[step - 1])
        return c

    lax.fori_loop(0, seq // blk // DIL_UNITS, body, 0)

    def emit(i, c):
        rows = pl.ds(pl.multiple_of(i * chunk, chunk), chunk)
        o_ref[0, rows, :] = jnp.concatenate([o_scr[p, rows, :] for p in range(pairs)],
                                            axis=-1).astype(o_ref.dtype)
        return c

    lax.fori_loop(0, seq // chunk, emit, 0)


def _dil_bias_table(group, window, dilation):
    reach = window // dilation
    assert reach <= DIL_BLOCK
    slopes = 2.0 ** (-ALIBI_MAX_BIAS * jnp.arange(1, DIL_HEADS + 1, dtype=F32) / DIL_HEADS)
    slopes = slopes[group * DIL_HEADS_PER_GROUP:(group + 1) * DIL_HEADS_PER_GROUP]
    qi = jnp.arange(DIL_BLOCK)[:, None] + DIL_BLOCK
    ki = jnp.arange(2 * DIL_BLOCK)[None, :]
    dist = qi - ki
    valid = (dist >= 0) & (dist <= reach)
    bias = -slopes[:, None, None] * (dist * dilation).astype(F32)[None]
    general = jnp.where(valid[None], bias, NEG_BIG)
    first = jnp.where((valid & (ki >= DIL_BLOCK))[None], bias, NEG_BIG)
    return jnp.stack([general, first], axis=1)


def _dil_attention(proj, group, window, dilation, bsz, seq):
    e = proj.shape[-1]
    assert (seq // DIL_BLOCK) % DIL_UNITS == 0
    assert seq % (DIL_BLOCK * dilation) == 0 and dilation & (dilation - 1) == 0
    view = proj.reshape(bsz, seq, e)
    w = DIL_OUT
    pairs = w // LANES
    pad = DIL_BLOCK * dilation

    def col(base):
        off = (base + group * w) // w
        return lambda b: (b, 0, off)

    bias = _dil_bias_table(group, window, dilation)
    o, lse = pl.pallas_call(
        functools.partial(_dil_attn_kernel, seq=seq, dilation=dilation),
        grid=(bsz,),
        in_specs=[pl.BlockSpec((1, seq, w), col(COL_DIL)),
                  pl.BlockSpec((1, seq, w), col(COL_DIL + DIL_WIDTH)),
                  pl.BlockSpec((1, seq, w), col(COL_DIL + 2 * DIL_WIDTH)),
                  pl.BlockSpec(bias.shape, lambda b: (0, 0, 0, 0))],
        out_specs=[pl.BlockSpec((1, seq, w), lambda b: (b, 0, 0)),
                   pl.BlockSpec((1, pairs, seq, LANES), lambda b: (b, 0, 0, 0))],
        out_shape=[jax.ShapeDtypeStruct((bsz, seq, w), BF16),
                   jax.ShapeDtypeStruct((bsz, pairs, seq, LANES), F32)],
        scratch_shapes=[pltpu.VMEM((pairs, seq, LANES), F32), pltpu.VMEM((pairs, pad + seq, LANES), F32),
                        pltpu.VMEM((pairs, pad + seq, LANES), F32), pltpu.VMEM((pairs, seq, LANES), F32),
                        pltpu.VMEM((3, pairs, 256, LANES), F32)],
        compiler_params=_cparams("arbitrary"),
        name=f"dilated_attn_g{group}",
    )(view, view, view, bias)
    return o.reshape(bsz * seq, w), lse


def _sb_attn_kernel(q_ref, k_ref, v_ref, o_ref, vcat_scr, acc_scr, *, seq):
    tq, tk, nblk = SB_TQ, SB_TK, SB_TQ // SB_TK
    qi = pl.program_id(2)
    pair_lanes = [slice(p * LANES, (p + 1) * LANES) for p in range(SB_PAIRS)]

    def key_rows(kb):
        return pl.ds(pl.multiple_of(kb * tk, tk), tk)

    @pl.when(qi == 0)
    def _():
        chan = lax.broadcasted_iota(jnp.int32, (LANES, tk), 0)

        def build(kb, c):
            for p in range(SB_PAIRS):
                vt = v_ref[0, key_rows(kb), pair_lanes[p]].astype(F32).T
                vcat_scr[p, kb] = jnp.concatenate(
                    [jnp.where(chan < HEAD_DIM, vt, 0.0), jnp.where(chan >= HEAD_DIM, vt, 0.0)],
                    axis=1).astype(BF16)
            return c

        lax.fori_loop(0, seq // tk, build, 0, unroll=4)

    lane = lax.broadcasted_iota(jnp.int32, (tq, LANES), 1)
    qcat, qcat_t = [], []
    for p in range(SB_PAIRS):
        q = q_ref[0, :, pair_lanes[p]] * (1.0 / math.sqrt(HEAD_DIM))
        zero = jnp.zeros_like(q)
        both = jnp.concatenate([jnp.where(lane < HEAD_DIM, q, zero), jnp.where(lane >= HEAD_DIM, q, zero)],
                               axis=0)
        qcat.append(both)
        qcat_t.append(both.astype(F32).T.astype(BF16))
    neg_tri = jnp.where(lax.broadcasted_iota(jnp.int32, (tk, tk), 1)
                        >= lax.broadcasted_iota(jnp.int32, (tk, tk), 0), -1.0, 0.0).astype(BF16)
    acc_scr[...] = jnp.zeros_like(acc_scr)

    def first_query(rel):
        width = tq
        while rel is not None and tq - width // 2 <= rel * tk and width // 2 >= LANES:
            width //= 2
        return tq - width

    def window(x, lo, axis):
        if lo == 0:
            return x
        return jnp.concatenate([lax.slice_in_dim(x, lo, tq, axis=axis),
                                lax.slice_in_dim(x, tq + lo, 2 * tq, axis=axis)], axis=axis)

    def scores(p, kb, rel):
        lo = first_query(rel)
        wq = tq - lo
        kk = k_ref[0, key_rows(kb), pair_lanes[p]]
        zt = lax.dot_general(kk, window(qcat[p], lo, 0), (((1,), (1,)), ((), ())),
                             preferred_element_type=F32)
        neg_abs = lax.bitcast_convert_type(
            lax.bitcast_convert_type(zt, jnp.uint32) | jnp.uint32(0x80000000), F32)
        sp = jnp.maximum(zt, 0.0) + jnp.log(1.0 + jnp.exp(neg_abs))
        before = None
        if rel is not None:
            kpos = lax.broadcasted_iota(jnp.int32, (tk, 2 * wq), 0) + rel * tk
            qpos = (lax.broadcasted_iota(jnp.int32, (tk, 2 * wq), 1) & (wq - 1)) + lo
            before = kpos < qpos
            sp = jnp.where(before, sp, 0.0)
        return kk, zt[0:1, :], sp.astype(BF16), before, lo

    def weights(p, state, run):
        kk, zt0, sp, before, lo = state
        wq = tq - lo
        arg = jnp.dot(jnp.concatenate([neg_tri, kk], axis=1),
                      jnp.concatenate([sp, window(qcat_t[p], lo, 1)], axis=0),
                      preferred_element_type=F32)
        a = jnp.exp(arg - window(run, lo, 1))
        if before is not None:
            a = jnp.where(before, a, 0.0)
        a = a.astype(BF16)
        acat = jnp.concatenate([a[:, :wq], a[:, wq:]], axis=0)
        col_sums = zt0 - arg[0:1, :]
        if lo:
            none = jnp.zeros((1, lo), F32)
            col_sums = jnp.concatenate([none, col_sums[:, :wq], none, col_sums[:, wq:]], axis=1)
        return (acat, lo), run + col_sums

    def values(p, kb, acat_lo):
        acat, lo = acat_lo
        acc_scr[p, :, lo:] += jnp.dot(vcat_scr[p, kb], acat, preferred_element_type=F32)

    def run_blocks(blocks, runs):
        n = len(blocks)
        runs = list(runs)
        st = [[None] * n for _ in range(SB_PAIRS)]
        ac = [[None] * n for _ in range(SB_PAIRS)]
        for step in range(n + 2):
            for p in range(SB_PAIRS):
                if step < n:
                    st[p][step] = scores(p, *blocks[step])
            for p in range(SB_PAIRS):
                if 0 <= step - 1 < n:
                    ac[p][step - 1], runs[p] = weights(p, st[p][step - 1], runs[p])
            for p in range(SB_PAIRS):
                if 0 <= step - 2 < n:
                    values(p, blocks[step - 2][0], ac[p][step - 2])
        return tuple(runs)

    zero_runs = tuple(jnp.zeros((1, 2 * tq), F32) for _ in range(SB_PAIRS))
    diagonal = [(qi * nblk + rel, rel) for rel in reversed(range(nblk))]
    n_chunks = qi * (nblk // SB_CHUNK)

    def chunk_blocks(it):
        base = (n_chunks - 1 - it) * SB_CHUNK
        return [(base + j, None) for j in reversed(range(SB_CHUNK))]

    @pl.when(qi == 0)
    def _():
        run_blocks(diagonal, zero_runs)

    @pl.when(qi > 0)
    def _():
        runs = run_blocks(diagonal + chunk_blocks(0), zero_runs)

        def alive(runs):
            lowest = functools.reduce(jnp.minimum, [jnp.min(r) for r in runs])
            return (lowest < SB_DEAD_RUN).astype(jnp.int32)

        def more(carry):
            it, _, go = carry
            return jnp.logical_and(it < n_chunks, go > 0)

        def chunk(carry):
            it, runs, _ = carry
            runs = run_blocks(chunk_blocks(it), runs)
            return it + 1, runs, alive(runs)

        lax.while_loop(more, chunk, (jnp.int32(1), runs, alive(runs)))

    o_ref[0] = jnp.concatenate([acc_scr[p].T for p in range(SB_PAIRS)], axis=-1).astype(o_ref.dtype)


def _sb_attention(proj, bsz, seq):
    e = proj.shape[-1]
    view = proj.reshape(bsz, seq, e)
    w = SB_PAIRS * LANES
    qo, ko, vo = COL_SB // w, (COL_SB + SB_WIDTH) // w, (COL_SB + 2 * SB_WIDTH) // w
    out = pl.pallas_call(
        functools.partial(_sb_attn_kernel, seq=seq),
        grid=(bsz, SB_WIDTH // w, seq // SB_TQ),
        in_specs=[pl.BlockSpec((1, SB_TQ, w), lambda b, g, i: (b, i, qo + g)),
                  pl.BlockSpec((1, seq, w), lambda b, g, i: (b, 0, ko + g)),
                  pl.BlockSpec((1, seq, w), lambda b, g, i: (b, 0, vo + g))],
        out_specs=pl.BlockSpec((1, SB_TQ, w), lambda b, g, i: (b, i, g)),
        out_shape=jax.ShapeDtypeStruct((bsz, seq, SB_WIDTH), BF16),
        scratch_shapes=[pltpu.VMEM((SB_PAIRS, seq // SB_TK, LANES, 2 * SB_TK), BF16),
                        pltpu.VMEM((SB_PAIRS, LANES, SB_TQ), F32)],
        compiler_params=_cparams("arbitrary", "arbitrary", "arbitrary"),
        name="stick_breaking_attn",
    )(view, view, view)
    return out.reshape(bsz * seq, SB_WIDTH)


def _conv_kernel(val_ref, gate_ref, w_ref, b_ref, g_ref, beta_ref, o_ref, u_scr, y_scr, *, seq):
    tr, pad, slabs = CONV_ROWS, CONV_PAD, CONV_CH // LANES
    lanes = [slice(j * LANES, (j + 1) * LANES) for j in range(slabs)]
    for j in range(slabs):
        u_scr[j, pl.ds(0, pad), :] = jnp.zeros((pad, LANES), F32)

    def glu(i, c):
        rows = pl.ds(pl.multiple_of(i * 256, 256), 256)
        val = val_ref[0, rows, :].astype(F32)
        gate = gate_ref[0, rows, :].astype(F32)
        u = val * _sigmoid(gate)
        for j in range(slabs):
            u_scr[j, pl.ds(pl.multiple_of(pad + i * 256, SUBLANES), 256), :] = u[:, lanes[j]]
        return c

    lax.fori_loop(0, seq // 256, glu, 0)

    first = pad - (CONV_WIDTH - 1)

    def tile(i, c):
        t0 = pl.multiple_of(i * tr, tr)
        groups = tr // 2 // SUBLANES
        acc = [[jnp.zeros((groups, SUBLANES, LANES), F32) for _ in range(slabs)] for _ in range(2)]
        for j in range(slabs):
            taps = [jnp.broadcast_to(w_ref[w:w + 1, lanes[j]], (SUBLANES, LANES))[None]
                    for w in range(CONV_WIDTH)]
            for k in range(CONV_WIDTH + 1):
                x = u_scr[j, pl.ds(t0 + (first + k), tr // 2, stride=2), :].reshape(groups, SUBLANES, LANES)
                if k < CONV_WIDTH:
                    acc[0][j] = acc[0][j] + x * taps[k]
                if k >= 1:
                    acc[1][j] = acc[1][j] + x * taps[k - 1]
        for phase in range(2):
            y = jnp.concatenate([a.reshape(tr // 2, LANES) for a in acc[phase]], axis=-1) + b_ref[...]
            mu = jnp.mean(y, axis=-1, keepdims=True)
            yc = y - mu
            var = jnp.mean(yc * yc, axis=-1, keepdims=True)
            yn = yc * lax.rsqrt(var + EPS) * g_ref[...] + beta_ref[...]
            out = yn * _sigmoid(yn)
            for j in range(slabs):
                y_scr[j, pl.ds(phase, tr // 2, stride=2), :] = out[:, lanes[j]]
        o_ref[0, pl.ds(t0, tr), :] = jnp.concatenate([y_scr[j] for j in range(slabs)],
                                                     axis=-1).astype(o_ref.dtype)
        return c

    lax.fori_loop(0, seq // tr, tile, 0)


def _conformer_conv(proj, conv_w, conv_b, norm_g, norm_b, bsz, seq):
    e = proj.shape[-1]
    view = proj.reshape(bsz, seq, e)
    c = CONV_CH
    voff, goff = COL_GLU // c, (COL_GLU + c) // c
    const = lambda b: (0, 0)
    out = pl.pallas_call(
        functools.partial(_conv_kernel, seq=seq),
        grid=(bsz,),
        in_specs=[pl.BlockSpec((1, seq, c), lambda b: (b, 0, voff)),
                  pl.BlockSpec((1, seq, c), lambda b: (b, 0, goff)),
                  pl.BlockSpec((CONV_WIDTH, c), const),
                  pl.BlockSpec((1, c), const), pl.BlockSpec((1, c), const), pl.BlockSpec((1, c), const)],
        out_specs=pl.BlockSpec((1, seq, c), lambda b: (b, 0, 0)),
        out_shape=jax.ShapeDtypeStruct((bsz, seq, c), BF16),
        scratch_shapes=[pltpu.VMEM((c // LANES, seq + CONV_PAD, LANES), F32),
                        pltpu.VMEM((c // LANES, CONV_ROWS, LANES), F32)],
        compiler_params=_cparams("arbitrary"),
        name="conformer_conv",
    )(view, view, conv_w, conv_b.reshape(1, c), norm_g.reshape(1, c), norm_b.reshape(1, c))
    return out.reshape(bsz * seq, c)


def _merge_kernel(o1_ref, o2_ref, o3_ref, l1_ref, l2_ref, l3_ref, ob_ref, oc_ref,
                  ga_ref, gb_ref, gc_ref, x_ref, wa_ref, wb_ref, wc_ref, wo_ref, out_ref):
    by_lanes = lambda ref: jnp.concatenate([ref[p] for p in range(ref.shape[0])], axis=-1)
    l1, l2, l3 = by_lanes(l1_ref), by_lanes(l2_ref), by_lanes(l3_ref)
    m = jnp.maximum(jnp.maximum(l1, l2), l3)
    e1, e2, e3 = jnp.exp(l1 - m), jnp.exp(l2 - m), jnp.exp(l3 - m)
    o_a = (e1 * o1_ref[...].astype(F32) + e2 * o2_ref[...].astype(F32)
           + e3 * o3_ref[...].astype(F32)) / (e1 + e2 + e3)
    ya = jnp.dot(o_a.astype(BF16), wa_ref[...], preferred_element_type=F32)
    yb = jnp.dot(ob_ref[...], wb_ref[...], preferred_element_type=F32)
    yc = jnp.dot(oc_ref[...], wc_ref[...], preferred_element_type=F32)
    merged = (_sigmoid(ga_ref[...].astype(F32)) * ya + _sigmoid(gb_ref[...].astype(F32)) * yb
              + _sigmoid(gc_ref[...].astype(F32)) * yc)
    out_ref[...] = x_ref[...] + jnp.dot(merged.astype(BF16), wo_ref[...], preferred_element_type=F32)


def _merge(o_groups, lse_groups, o_b, o_c, proj, x, wa, wb, wc, wo, *, tm):
    n, d = x.shape
    row = lambda w, j=0: pl.BlockSpec((tm, w), lambda i: (i, j))
    full = lambda a: pl.BlockSpec(a.shape, lambda i: (0, 0), pipeline_mode=pl.Buffered(1))
    g0 = COL_GATES // d
    _, pairs, seq, _ = lse_groups[0].shape
    per_batch = seq // tm
    lse = pl.BlockSpec((None, pairs, tm, LANES), lambda i: (i // per_batch, 0, i % per_batch, 0))
    return pl.pallas_call(
        _merge_kernel,
        grid=(n // tm,),
        in_specs=[row(DIL_OUT), row(DIL_OUT), row(DIL_OUT), lse, lse, lse,
                  row(SB_WIDTH), row(CONV_CH), row(d, g0), row(d, g0 + 1), row(d, g0 + 2), row(d),
                  full(wa), full(wb), full(wc), full(wo)],
        out_specs=row(d),
        out_shape=jax.ShapeDtypeStruct((n, d), F32),
        compiler_params=_cparams("arbitrary"),
        name="branch_merge_out_proj",
    )(*o_groups, *lse_groups, o_b, o_c, proj, proj, proj, x, wa, wb, wc, wo)


MXU_COLS = 256


def _col_pieces(lo, hi):
    return [(p, min(p + MXU_COLS, hi)) for p in range(lo, hi, MXU_COLS)]


FF_CHUNKS = ((0, FF_SPLIT), (FF_SPLIT, D_FF))
SWIGLU_HOOKS = sum(2 * len(_col_pieces(lo, hi)) + len(_col_pieces(0, D_MODEL)) for lo, hi in FF_CHUNKS)


def _swiglu_tile(h, wg_ref, wu_ref, wd_ref, hook=None):
    def matmul(x, w_ref, rows, lo, hi):
        if hook is None:
            return jnp.dot(x, w_ref[rows, lo:hi], preferred_element_type=F32)
        pieces = []
        for p, q in _col_pieces(lo, hi):
            pieces.append(jnp.dot(x, w_ref[rows, p:q], preferred_element_type=F32))
            hook()
        return jnp.concatenate(pieces, axis=-1)

    y = None
    for lo, hi in FF_CHUNKS:
        a = matmul(h, wg_ref, slice(None), lo, hi)
        u = matmul(h, wu_ref, slice(None), lo, hi)
        act = (a * _sigmoid(a) * u).astype(BF16)
        part = matmul(act, wd_ref, slice(lo, hi), 0, D_MODEL)
        y = part if y is None else y + part
    return y


def _dense_ffn_kernel(x_ref, g_ref, wg_ref, wu_ref, wd_ref, o_ref):
    x = x_ref[...]
    ms = jnp.mean(x * x, axis=-1, keepdims=True)
    h = ((x * lax.rsqrt(ms + EPS)) * g_ref[...]).astype(BF16)
    o_ref[...] = x + _swiglu_tile(h, wg_ref, wu_ref, wd_ref)


def _dense_ffn(x, g, wg, wu, wd, *, tm):
    n, d = x.shape
    full = lambda a: pl.BlockSpec(a.shape, lambda i: (0, 0), pipeline_mode=pl.Buffered(1))
    return pl.pallas_call(
        _dense_ffn_kernel,
        grid=(n // tm,),
        in_specs=[pl.BlockSpec((tm, d), lambda i: (i, 0)), pl.BlockSpec((1, d), lambda i: (0, 0)),
                  full(wg), full(wu), full(wd)],
        out_specs=pl.BlockSpec((tm, d), lambda i: (i, 0)),
        out_shape=jax.ShapeDtypeStruct((n, d), F32),
        compiler_params=_cparams("arbitrary"),
        name="dense_swiglu",
    )(x, g.reshape(1, d), wg, wu, wd)


def _router_kernel(x_ref, g_ref, wr_ref, br_ref, h_ref, route_ref):
    x = x_ref[...]
    ms = jnp.mean(x * x, axis=-1, keepdims=True)
    h = (x * lax.rsqrt(ms + EPS)) * g_ref[...]
    _store_token_tiles(h_ref, h)
    h_hi = h.astype(BF16)
    h_lo = (h - h_hi.astype(F32)).astype(BF16)
    w = wr_ref[...]
    w_hi = w.astype(BF16)
    w_lo = (w - w_hi.astype(F32)).astype(BF16)
    logits = (jnp.dot(h_hi, w_hi, preferred_element_type=F32)
              + jnp.dot(h_lo, w_hi, preferred_element_type=F32)
              + jnp.dot(h_hi, w_lo, preferred_element_type=F32)) + br_ref[...]
    lane = lax.broadcasted_iota(jnp.int32, logits.shape, 1)
    m1 = jnp.max(logits, axis=-1, keepdims=True)
    i1 = jnp.min(jnp.where(logits == m1, lane, LANES), axis=-1, keepdims=True)
    rest = jnp.where(lane == i1, -jnp.inf, logits)
    m2 = jnp.max(rest, axis=-1, keepdims=True)
    i2 = jnp.min(jnp.where(rest == m2, lane, LANES), axis=-1, keepdims=True)
    e2 = jnp.exp(m2 - m1)
    g1 = 1.0 / (1.0 + e2)
    g2 = e2 / (1.0 + e2)
    route = jnp.where(lane == 0, i1.astype(F32),
                      jnp.where(lane == 1, i2.astype(F32),
                                jnp.where(lane == 2, g1, jnp.where(lane == 3, g2, 0.0))))
    route_ref[...] = route


def _router(x, g, w_router, b_router, *, tm):
    n, d = x.shape
    wr = jnp.zeros((d, LANES), F32).at[:, :N_EXPERTS].set(w_router)
    br = jnp.full((1, LANES), NEG_BIG, F32).at[0, :N_EXPERTS].set(b_router)
    return pl.pallas_call(
        _router_kernel,
        grid=(n // tm,),
        in_specs=[pl.BlockSpec((tm, d), lambda i: (i, 0)), pl.BlockSpec((1, d), lambda i: (0, 0)),
                  pl.BlockSpec((d, LANES), lambda i: (0, 0)), pl.BlockSpec((1, LANES), lambda i: (0, 0))],
        out_specs=[pl.BlockSpec((tm * ROW_TILE, LANES), lambda i: (i, 0)),
                   pl.BlockSpec((tm, LANES), lambda i: (i, 0))],
        out_shape=[jax.ShapeDtypeStruct((n * ROW_TILE, LANES), F32), jax.ShapeDtypeStruct((n, LANES), F32)],
        compiler_params=_cparams("arbitrary"),
        name="router_top2",
    )(x, g.reshape(1, d), wr, br)


ROW_TILE = D_MODEL // LANES


def _store_token_tiles(ref, x):
    rows = x.shape[0]
    for c in range(ROW_TILE):
        ref[pl.ds(c, rows, stride=ROW_TILE), :] = x[:, c * LANES:(c + 1) * LANES]


def _load_token_tiles(ref, rows):
    return jnp.concatenate([ref[pl.ds(c, rows, stride=ROW_TILE), :] for c in range(ROW_TILE)], axis=-1)


def _tile_rows(t):
    start = t * ROW_TILE
    return pl.ds(start if isinstance(start, int) else pl.multiple_of(start, ROW_TILE), ROW_TILE)


class _BlockRows:
    def __init__(self, dst_ref, n_tokens):
        self.dst_ref = dst_ref
        self.last_token = n_tokens - 1

    def gather_row(self, blk, r, h_hbm, xbuf, sem):
        tok = lax.shift_right_logical(self.dst_ref[blk, r], TOP_K.bit_length() - 1)
        tok = jnp.minimum(tok, self.last_token)
        pltpu.make_async_copy(h_hbm.at[_tile_rows(tok), :], xbuf.at[_tile_rows(r), :], sem).start()

    def scatter_row(self, blk, r, ybuf, y_hbm, sem):
        pltpu.make_async_copy(ybuf.at[_tile_rows(r), :], y_hbm.at[_tile_rows(self.dst_ref[blk, r]), :],
                              sem).start()

    def all_rows(self, row_fn):
        def body(g, c):
            for j in range(SUBLANES):
                row_fn(g * SUBLANES + j)
            return c

        lax.fori_loop(0, MOE_TM // SUBLANES, body, 0)

    @staticmethod
    def wait_gather(h_hbm, xbuf, sem):
        pltpu.make_async_copy(h_hbm.at[pl.ds(0, MOE_TM * ROW_TILE), :], xbuf, sem).wait()

    @staticmethod
    def wait_scatter(ybuf, y_hbm, sem):
        pltpu.make_async_copy(ybuf, y_hbm.at[pl.ds(0, MOE_TM * ROW_TILE), :], sem).wait()


def _expert_kernel(be_ref, used_ref, dst_ref, h_hbm, wg_ref, wu_ref, wd_ref, y_hbm,
                   xbuf, ybuf, gsem, ssem, *, n_tokens):
    i = pl.program_id(0)
    used = used_ref[0]
    slot = i % 2
    rows = _BlockRows(dst_ref, n_tokens)
    gather_hooks = SWIGLU_HOOKS // 2
    share = lambda k, hooks: range(k * MOE_TM // hooks, (k + 1) * MOE_TM // hooks)

    def compute(scatter_previous):
        _BlockRows.wait_gather(h_hbm, xbuf.at[slot], gsem.at[slot])
        x = _load_token_tiles(xbuf.at[slot], MOE_TM).astype(BF16)
        nxt = jnp.minimum(i + 1, used - 1)
        calls = [0]

        def start_some():
            k = calls[0]
            calls[0] += 1
            if k < gather_hooks:
                for r in share(k, gather_hooks):
                    rows.gather_row(nxt, r, h_hbm, xbuf.at[1 - slot], gsem.at[1 - slot])
            if scatter_previous:
                for r in share(k, SWIGLU_HOOKS):
                    rows.scatter_row(i - 1, r, ybuf.at[1 - slot], y_hbm, ssem.at[1 - slot])

        y = _swiglu_tile(x, wg_ref, wu_ref, wd_ref, hook=start_some)
        assert calls[0] == SWIGLU_HOOKS

        @pl.when(i >= 2)
        def _():
            _BlockRows.wait_scatter(ybuf.at[slot], y_hbm, ssem.at[slot])

        _store_token_tiles(ybuf.at[slot], y)

    @pl.when(i == 0)
    def _():
        rows.all_rows(lambda r: rows.gather_row(0, r, h_hbm, xbuf.at[0], gsem.at[0]))
        ybuf[1] = jnp.zeros((MOE_TM * ROW_TILE, LANES), F32)
        for half in range(2):
            clear = pltpu.make_async_copy(
                ybuf.at[1],
                y_hbm.at[pl.ds((n_tokens * TOP_K + half * MOE_TM) * ROW_TILE, MOE_TM * ROW_TILE), :],
                ssem.at[half])
            clear.start()
            clear.wait()
        compute(False)

    @pl.when(jnp.logical_and(i >= 1, i < used))
    def _():
        compute(True)

    @pl.when(i == used)
    def _():
        _BlockRows.wait_gather(h_hbm, xbuf.at[slot], gsem.at[slot])
        _BlockRows.wait_scatter(ybuf.at[slot], y_hbm, ssem.at[slot])
        rows.all_rows(lambda r: rows.scatter_row(i - 1, r, ybuf.at[1 - slot], y_hbm, ssem.at[1 - slot]))
        _BlockRows.wait_scatter(ybuf.at[1 - slot], y_hbm, ssem.at[1 - slot])


def _experts(h, block_expert, used, row_dst, wg, wu, wd, n_tokens):
    d = D_MODEL
    n_blocks = row_dst.shape[0]
    f = wg.shape[-1]
    assert n_tokens * TOP_K >= 2 * MOE_TM
    one = pl.Buffered(1)
    grid_spec = pltpu.PrefetchScalarGridSpec(
        num_scalar_prefetch=3,
        grid=(n_blocks,),
        in_specs=[pl.BlockSpec(memory_space=pl.ANY),
                  pl.BlockSpec((None, d, f), lambda i, be, us, ds: (be[i], 0, 0), pipeline_mode=one),
                  pl.BlockSpec((None, d, f), lambda i, be, us, ds: (be[i], 0, 0), pipeline_mode=one),
                  pl.BlockSpec((None, f, d), lambda i, be, us, ds: (be[i], 0, 0), pipeline_mode=one)],
        out_specs=pl.BlockSpec(memory_space=pl.ANY),
        scratch_shapes=[pltpu.VMEM((2, MOE_TM * ROW_TILE, LANES), F32),
                        pltpu.VMEM((2, MOE_TM * ROW_TILE, LANES), F32),
                        pltpu.SemaphoreType.DMA((2,)), pltpu.SemaphoreType.DMA((2,))],
    )
    return pl.pallas_call(
        functools.partial(_expert_kernel, n_tokens=n_tokens),
        grid_spec=grid_spec,
        out_shape=jax.ShapeDtypeStruct(((n_tokens * TOP_K + 2 * MOE_TM) * ROW_TILE, LANES), F32),
        compiler_params=_cparams("arbitrary"),
        name="expert_swiglu",
    )(block_expert, used, row_dst, h, wg, wu, wd)


def _combine_kernel(y_ref, x_ref, route_ref, o_ref):
    tt = x_ref.shape[0]
    route = route_ref[...]
    out = x_ref[...]
    for k in range(TOP_K):
        yk = jnp.concatenate([y_ref[pl.ds(k * ROW_TILE + c, tt, stride=TOP_K * ROW_TILE), :]
                              for c in range(ROW_TILE)], axis=-1)
        out = out + route[:, TOP_K + k:TOP_K + k + 1] * yk
    o_ref[...] = out


def _combine(x, ys, route):
    n, d = x.shape
    tt = COMBINE_TT
    return pl.pallas_call(
        _combine_kernel,
        grid=(n // tt,),
        in_specs=[pl.BlockSpec((tt * TOP_K * ROW_TILE, LANES), lambda i: (i, 0)),
                  pl.BlockSpec((tt, d), lambda i: (i, 0)),
                  pl.BlockSpec((tt, LANES), lambda i: (i, 0))],
        out_specs=pl.BlockSpec((tt, d), lambda i: (i, 0)),
        out_shape=jax.ShapeDtypeStruct((n, d), F32),
        compiler_params=_cparams("arbitrary"),
        name="expert_combine",
    )(ys, x, route)


def _routed_ffn(x, g, w_router, b_router, wg, wu, wd, *, tm):
    n, d = x.shape
    h, route = _router(x, g, w_router, b_router, tm=tm)
    e_flat = route[:, :TOP_K].astype(jnp.int32).reshape(-1)
    onehot = (e_flat[:, None] == jnp.arange(N_EXPERTS)[None, :]).astype(jnp.int32)
    csum = jnp.cumsum(onehot, axis=0)
    counts = csum[-1]
    rank = jnp.sum((csum - onehot) * onehot, axis=1)
    padded = (counts + MOE_TM - 1) // MOE_TM * MOE_TM
    pend = jnp.cumsum(padded)
    pstart = pend - padded
    dest = (pstart[e_flat] + rank).astype(jnp.int32)
    n_blocks = (n * TOP_K) // MOE_TM + N_EXPERTS
    scratch_rows = n * TOP_K + jnp.arange(n_blocks * MOE_TM, dtype=jnp.int32) % (2 * MOE_TM)
    row_dst = scratch_rows.at[dest].set(jnp.arange(n * TOP_K, dtype=jnp.int32), unique_indices=True,
                                        mode='promise_in_bounds').reshape(n_blocks, MOE_TM)
    block_expert = jnp.clip(jnp.searchsorted(pend, jnp.arange(n_blocks) * MOE_TM, side='right'),
                            0, N_EXPERTS - 1).astype(jnp.int32)
    used = (pend[-1:] // MOE_TM).astype(jnp.int32)
    ys = _experts(h, block_expert, used, row_dst, wg, wu, wd, n)
    return _combine(x, ys, route)


def kernel(x, attn_norm_g, w_in, q_norm_g, k_norm_g, conv_w, conv_b, conv_norm_g, conv_norm_b,
           w_branch_a, w_branch_b, w_branch_c, w_out, ffn_norm_g, w_ffn_gate, w_ffn_up,
           w_ffn_down, w_router, b_router, w_exp_gate, w_exp_up, w_exp_down):
    bsz, seq, d = x.shape
    depth = attn_norm_g.shape[0]
    n = bsz * seq
    tm = 512
    xf = x.reshape(n, d)
    s_dil, s_sb, s_glu = 3 * DIL_WIDTH, 3 * DIL_WIDTH + 3 * SB_WIDTH, 3 * DIL_WIDTH + 3 * SB_WIDTH + 2 * CONV_CH
    for layer in range(depth):
        wl = w_in[layer]
        w_perm = jnp.concatenate([wl[:, s_glu:], wl[:, s_sb:s_glu], wl[:, s_dil:s_sb], wl[:, :s_dil]],
                                 axis=1).astype(BF16)
        proj_a = _norm_matmul(xf, attn_norm_g[layer], w_perm[:, :PROJ_A_COLS], tm=tm)
        qk_gain = jnp.concatenate([jnp.tile(q_norm_g[layer], DIL_HEADS) * (1.0 / math.sqrt(HEAD_DIM)),
                                   jnp.tile(k_norm_g[layer], DIL_HEADS)]).reshape(1, 2 * DIL_WIDTH)
        proj_b = _norm_matmul(xf, attn_norm_g[layer], w_perm[:, PROJ_A_COLS:], tm=tm,
                              qk_gain=qk_gain, qk_lo=COL_DIL)
        o_groups, lse_groups = [], []
        for gi, (window, dilation) in enumerate(DIL_GROUPS):
            o_g, lse_g = _dil_attention(proj_b, gi, window, dilation, bsz, seq)
            o_groups.append(o_g)
            lse_groups.append(lse_g)
        o_b = _sb_attention(proj_b, bsz, seq)
        o_c = _conformer_conv(proj_a, conv_w[layer], conv_b[layer], conv_norm_g[layer],
                              conv_norm_b[layer], bsz, seq)
        xf = _merge(o_groups, lse_groups, o_b, o_c, proj_a, xf,
                    w_branch_a[layer].astype(BF16), w_branch_b[layer].astype(BF16),
                    w_branch_c[layer].astype(BF16), w_out[layer].astype(BF16), tm=tm)
        i = layer // 2
        if layer % 2 == 0:
            xf = _dense_ffn(xf, ffn_norm_g[layer], w_ffn_gate[i].astype(BF16), w_ffn_up[i].astype(BF16),
                            w_ffn_down[i].astype(BF16), tm=tm)
        else:
            xf = _routed_ffn(xf, ffn_norm_g[layer], w_router[i], b_router[i],
                             w_exp_gate[i].astype(BF16), w_exp_up[i].astype(BF16),
                             w_exp_down[i].astype(BF16), tm=tm)
    return xf.reshape(bsz, seq, d)
```
